```python
import math
import jax, jax.numpy as jnp
from jax import lax
import numpy as np

D_MODEL = 2048
BATCH = 8
SEQ = 2048
DEPTH = 2
DEC_BATCH = 128
DEC_SEQ = 1
PAST_LEN = 2048
PAGE_SIZE = 128

HS = 64
D_A = D_MODEL // 2
H_A = D_A // HS
R_W = 64
R_A = 64
R_V = 32
GN_EPS = 64e-5
HD = 64
D_B = D_MODEL // 2
H_B = D_B // HD
G_KV = 4
R_Q = H_B // G_KV
KV_W = G_KV * HD
CMP_LEN = 32
CMP_STRIDE = 16
CMP_HID = 128
SEL_LEN = 64
N_TOP = 16
WIN = 512
Q_BLOCK = 32
NUM_BUCKETS = 32
MAX_DIST = 128
SCALE = HD ** -0.5
A_COLS = 4 * D_A + R_W + R_A
B_COLS = 2 * D_B + 6 * KV_W + 3 * H_B
IN_COLS = A_COLS + B_COLS + 2 * D_MODEL
ALPHA = (2 * DEPTH) ** 0.25
BETA = (8 * DEPTH) ** -0.25
LN_EPS = 1e-5
NEG = -1e30

kernel_name = 'hybrid_rwkv7_nsa_decoder_step'


def split_cols(x, sizes):
    return jnp.split(x, np.cumsum(sizes)[:-1].tolist(), axis=-1)


def layer_norm(x, g, b):
    xf = x.astype(jnp.float32)
    mu = xf.mean(-1, keepdims=True)
    var = jnp.square(xf - mu).mean(-1, keepdims=True)
    return ((xf - mu) * lax.rsqrt(var + LN_EPS) * g + b).astype(x.dtype)


def masked_softmax(s, mask):
    s = jnp.where(mask, s.astype(jnp.float32), NEG)
    return jnp.where(mask, jax.nn.softmax(s, axis=-1), 0.0)


def t5_bucket(dist):
    n = jnp.maximum(dist, 0)
    max_exact = NUM_BUCKETS // 2
    nf = jnp.maximum(n, 1).astype(jnp.float32)
    large = max_exact + (jnp.log(nf / max_exact) / math.log(MAX_DIST / max_exact)
                         * (NUM_BUCKETS - max_exact)).astype(jnp.int32)
    large = jnp.minimum(large, NUM_BUCKETS - 1)
    return jnp.where(n < max_exact, n, large)


def t5_bias(table, dist):
    b = t5_bucket(dist)
    tg = table.reshape(NUM_BUCKETS, G_KV, R_Q)
    out = tg[b, jnp.arange(G_KV)[:, None]]
    return jnp.swapaxes(out, -1, -2).astype(jnp.float32)


def rwkv_scan(r, w, k, v, a, b, s0):
    def step(S, inp):
        r_t, w_t, k_t, v_t, a_t, b_t = inp
        sa = jnp.einsum('bhij,bhj->bhi', S, a_t)
        S = S * w_t[:, :, None, :] + sa[..., None] * b_t[:, :, None, :] + v_t[..., None] * k_t[:, :, None, :]
        return S, jnp.einsum('bhij,bhj->bhi', S, r_t)
    xs = (r, w, k, v, a, b)
    xs = tuple(jnp.moveaxis(u, 1, 0) for u in xs)
    s_fin, ys = lax.scan(step, s0, xs)
    return jnp.moveaxis(ys, 0, 1), s_fin


def rwkv_mixer(fa, s0, v_first, l, p):
    bsz, t = fa.shape[:2]
    r, k, v, w_lo, a_lo, z = split_cols(fa, [D_A, D_A, D_A, R_W, R_A, D_A])
    w_log = -jax.nn.softplus(-(p['rw_w0'][l] + jnp.tanh(w_lo) @ p['rw_w2'][l])) - 0.5
    decay = jnp.exp(-jnp.exp(w_log.astype(jnp.float32)))
    if l == 0:
        v_first = v
    else:
        vg = jax.nn.sigmoid(p['rw_v0'][l - 1] + (v @ p['rw_v1'][l - 1]) @ p['rw_v2'][l - 1])
        v = v + (v_first - v) * vg
    a = jax.nn.sigmoid(p['rw_a0'][l] + a_lo @ p['rw_a2'][l])
    heads = lambda u: u.reshape(bsz, t, H_A, HS).astype(jnp.float32)
    kk = heads(k * p['rw_kk'][l])
    kk = kk * lax.rsqrt(jnp.maximum(jnp.sum(kk * kk, -1, keepdims=True), 1e-24))
    k = k * (1.0 + (a - 1.0) * p['rw_ka'][l])
    rh, kh, vh, ah = heads(r), heads(k), heads(v), heads(a)
    y, s_fin = rwkv_scan(rh, heads(decay), kh, vh, -kk, kk * ah, s0)
    mu = y.mean(-1, keepdims=True)
    var = jnp.square(y - mu).mean(-1, keepdims=True)
    yn = ((y - mu) * lax.rsqrt(var + GN_EPS)).reshape(bsz, t, D_A) * p['rw_gn_g'][l] + p['rw_gn_b'][l]
    bonus = (jnp.sum(rh * kh * p['rw_rk'][l], -1, keepdims=True) * vh).reshape(bsz, t, D_A)
    o = (yn + bonus) * jax.nn.silu(z.astype(jnp.float32))
    return o.astype(fa.dtype), s_fin, v_first


def compress(rows, w1, b1, w2, b2):
    bsz, t = rows.shape[:2]
    nch = t // CMP_STRIDE
    ch = rows[:, :nch * CMP_STRIDE].reshape(bsz, nch, CMP_STRIDE, G_KV, HD)
    ch = ch.transpose(0, 1, 3, 2, 4).reshape(bsz, nch, G_KV, CMP_STRIDE * HD)
    half = CMP_STRIDE * HD
    h = ch[:, :-1] @ w1[:half] + ch[:, 1:] @ w1[half:] + b1
    return jax.nn.gelu(h) @ w2 + b2


def to_blocks(rows):
    bsz, t = rows.shape[:2]
    nsel = -(-t // SEL_LEN)
    rows = jnp.pad(rows, ((0, 0), (0, nsel * SEL_LEN - t), (0, 0), (0, 0)))
    return rows.reshape(bsz, nsel, SEL_LEN, G_KV, HD)


def nsa_attend(q, q_pos, k_cmp, v_cmp, ks_blk, vs_blk, kw, vw, kw_pos, gates, table):
    bsz, tq = q.shape[:2]
    nc, nsel = k_cmp.shape[1], ks_blk.shape[1]
    cmp_end = jnp.arange(nc, dtype=jnp.int32) * CMP_STRIDE + CMP_LEN - 1
    dist_c = q_pos[:, None] - cmp_end[None, :]
    s_c = jnp.einsum('btgrd,bngd->btgrn', q, k_cmp) * SCALE
    s_c = s_c + t5_bias(table, jnp.broadcast_to(dist_c[:, None, :], (tq, G_KV, nc)))[None]
    p_c = masked_softmax(s_c, (dist_c >= 0)[None, :, None, None, :])
    o_c = jnp.einsum('btgrn,bngd->btgrd', p_c, v_cmp)
    ci = jnp.arange(nc)[:, None] * CMP_STRIDE
    sj = jnp.arange(nsel)[None, :] * SEL_LEN
    overlap = ((ci < sj + SEL_LEN) & (ci + CMP_LEN > sj)).astype(jnp.float32)
    imp = jnp.einsum('btgrn,ns->btgs', p_c, overlap)
    blk = jnp.arange(nsel, dtype=jnp.int32)[None, :]
    cur = (q_pos // SEL_LEN)[:, None]
    valid = blk * SEL_LEN <= q_pos[:, None]
    forced = (blk == 0) | (blk == cur) | (blk == cur - 1)
    imp = jnp.where(valid[None, :, None, :], imp, NEG)
    imp = jnp.where(forced[None, :, None, :], -NEG, imp)
    top_v, top_i = lax.top_k(imp, min(N_TOP, nsel))
    n_sel = top_i.shape[-1]
    top_ok = top_v > 0.5 * NEG
    kb = ks_blk.transpose(0, 3, 1, 2, 4).reshape(bsz, G_KV, nsel, SEL_LEN * HD)
    vb = vs_blk.transpose(0, 3, 1, 2, 4).reshape(bsz, G_KV, nsel, SEL_LEN * HD)
    idx = top_i.transpose(0, 2, 1, 3).reshape(bsz, G_KV, tq * n_sel)[..., None]
    gather = lambda src: jnp.take_along_axis(src, idx, axis=2).reshape(
        bsz, G_KV, tq, n_sel * SEL_LEN, HD).transpose(0, 2, 1, 3, 4)
    k_sel, v_sel = gather(kb), gather(vb)
    pos_sel = (top_i[..., None] * SEL_LEN + jnp.arange(SEL_LEN, dtype=jnp.int32)).reshape(bsz, tq, G_KV, n_sel * SEL_LEN)
    dist_s = q_pos[None, :, None, None] - pos_sel
    mask_s = (dist_s >= 0) & jnp.repeat(top_ok, SEL_LEN, axis=-1)
    s_s = jnp.einsum('btgrd,btgkd->btgrk', q, k_sel) * SCALE + t5_bias(table, dist_s)
    p_s = masked_softmax(s_s, mask_s[..., None, :])
    o_s = jnp.einsum('btgrk,btgkd->btgrd', p_s, v_sel)
    dist_w = q_pos[:, None] - kw_pos[None, :]
    mask_w = (dist_w >= 0) & (dist_w < WIN) & (kw_pos[None, :] >= 0)
    s_w = jnp.einsum('btgrd,bkgd->btgrk', q, kw) * SCALE
    s_w = s_w + t5_bias(table, jnp.broadcast_to(dist_w[:, None, :], (tq, G_KV, kw.shape[1])))[None]
    p_w = masked_softmax(s_w, mask_w[None, :, None, None, :])
    o_w = jnp.einsum('btgrk,bkgd->btgrd', p_w, vw)
    return gates[..., 0:1] * o_c + gates[..., 1:2] * o_s + gates[..., 2:3] * o_w


def nsa_prompt(q, kc, vc, ks, vs, kw, vw, gates, cw1, cb1, cw2, cb2, table):
    bsz, t = q.shape[:2]
    k_cmp = compress(kc, cw1[0], cb1[0], cw2[0], cb2[0])
    v_cmp = compress(vc, cw1[1], cb1[1], cw2[1], cb2[1])
    ks_blk, vs_blk = to_blocks(ks), to_blocks(vs)
    pad = ((0, 0), (WIN, 0), (0, 0), (0, 0))
    kw_pad, vw_pad = jnp.pad(kw, pad), jnp.pad(vw, pad)
    nqb = t // Q_BLOCK
    qb = jnp.swapaxes(q.reshape(bsz, nqb, Q_BLOCK, G_KV, R_Q, HD), 0, 1)
    gb = jnp.swapaxes(gates.reshape(bsz, nqb, Q_BLOCK, G_KV, R_Q, 3), 0, 1)
    starts = jnp.arange(nqb, dtype=jnp.int32) * Q_BLOCK

    def one_block(args):
        q_i, g_i, s = args
        kw_i = lax.dynamic_slice_in_dim(kw_pad, s, WIN + Q_BLOCK, axis=1)
        vw_i = lax.dynamic_slice_in_dim(vw_pad, s, WIN + Q_BLOCK, axis=1)
        kw_pos = s - WIN + jnp.arange(WIN + Q_BLOCK, dtype=jnp.int32)
        q_pos = s + jnp.arange(Q_BLOCK, dtype=jnp.int32)
        return nsa_attend(q_i, q_pos, k_cmp, v_cmp, ks_blk, vs_blk, kw_i, vw_i, kw_pos, g_i, table)

    o = lax.map(one_block, (qb, gb, starts))
    return jnp.swapaxes(o, 0, 1).reshape(bsz, t, G_KV, R_Q, HD)


def trunk_layer(x, l, v_first, past, p):
    bsz, t = x.shape[:2]
    h = x @ p['w_in'][l]
    fa, fb, fm = split_cols(h, [A_COLS, B_COLS, 2 * D_MODEL])
    if past is None:
        prev = jnp.pad(fa[:, :-1], ((0, 0), (1, 0), (0, 0)))
        s0 = jnp.zeros((bsz, H_A, HS, HS), jnp.float32)
    else:
        prev = jnp.concatenate([past['shift'][:, None].astype(fa.dtype), fa[:, :-1]], axis=1)
        s0 = past['wkv'].astype(jnp.float32)
    new_shift = fa[:, -1]
    fa_mix = fa + (prev - fa) * p['mu_shift'][l]
    o_a, wkv, v_first = rwkv_mixer(fa_mix, s0, v_first, l, p)
    q, kc, vc, ks, vs, kw, vw, gl, z_b = split_cols(fb, [D_B] + [KV_W] * 6 + [3 * H_B, D_B])
    q = q.reshape(bsz, t, G_KV, R_Q, HD)
    rows = lambda u: u.reshape(bsz, t, G_KV, HD)
    kc, vc, ks, vs, kw, vw = rows(kc), rows(vc), rows(ks), rows(vs), rows(kw), rows(vw)
    gates = jax.nn.sigmoid(gl.reshape(bsz, t, G_KV, R_Q, 3))
    new_rows = jnp.stack([kc, vc, ks, vs], axis=2)
    new_win = jnp.stack([kw, vw], axis=2)
    cw1, cb1, cw2, cb2 = p['cmp_w1'][l], p['cmp_b1'][l], p['cmp_w2'][l], p['cmp_b2'][l]
    if past is None:
        o_b = nsa_prompt(q, kc, vc, ks, vs, kw, vw, gates, cw1, cb1, cw2, cb2, p['rel_bias'])
        win_state = new_win[:, -min(WIN, t):]
    else:
        past_len = past['kv'].shape[1]
        all_rows = jnp.concatenate([past['kv'].astype(new_rows.dtype), new_rows], axis=1)
        k_cmp = compress(all_rows[:, :, 0], cw1[0], cb1[0], cw2[0], cb2[0])
        v_cmp = compress(all_rows[:, :, 1], cw1[1], cb1[1], cw2[1], cb2[1])
        ks_blk, vs_blk = to_blocks(all_rows[:, :, 2]), to_blocks(all_rows[:, :, 3])
        nbuf = past['win'].shape[1]
        win_all = jnp.concatenate([past['win'].astype(new_win.dtype), new_win], axis=1)
        kw_pos = past_len - nbuf + jnp.arange(nbuf + t, dtype=jnp.int32)
        q_pos = past_len + jnp.arange(t, dtype=jnp.int32)
        o_b = nsa_attend(q, q_pos, k_cmp, v_cmp, ks_blk, vs_blk, win_all[:, :, 0], win_all[:, :, 1],
                         kw_pos, gates, p['rel_bias'])
        win_state = win_all[:, -nbuf:]
    o_b = o_b.reshape(bsz, t, D_B).astype(x.dtype) * jax.nn.silu(z_b)
    g_a, g_b = split_cols(fm, [D_MODEL, D_MODEL])
    merged = jax.nn.sigmoid(g_a) * (o_a @ p['w_up_a'][l]) + jax.nn.sigmoid(g_b) * (o_b @ p['w_up_b'][l])
    y = layer_norm(ALPHA * x + merged @ p['w_out'][l], p['ln_g'][l], p['ln_b'][l])
    return y, v_first, (new_rows, win_state, wkv, new_shift)


def setup_inputs(seed: int = 0) -> dict:
    key = jax.random.key(seed)
    k = jax.random.split(key, 32)
    f32 = jnp.float32
    nrm = lambda kk, shape, s: jax.random.normal(kk, shape, f32) * s
    n_pages = PAST_LEN // PAGE_SIZE
    n_used = DEC_BATCH * n_pages
    n_phys = n_used + max(1, n_used // 4)
    perm = jax.random.permutation(k[6], n_phys)
    page_table = perm[:n_used].reshape(DEC_BATCH, n_pages).astype(jnp.int32)
    return {
        'x_prompt': nrm(k[0], (BATCH, SEQ, D_MODEL), 1.0),
        'x_sample': nrm(k[1], (DEC_BATCH, DEC_SEQ, D_MODEL), 1.0),
        'cache_kv': nrm(k[2], (DEPTH, n_phys, PAGE_SIZE, 4, G_KV, HD), 1.0),
        'cache_win_kv': nrm(k[3], (DEPTH, DEC_BATCH, min(WIN, PAST_LEN), 2, G_KV, HD), 1.0),
        'state_wkv': nrm(k[4], (DEPTH, DEC_BATCH, H_A, HS, HS), 1.0),
        'state_shift': nrm(k[5], (DEPTH, DEC_BATCH, A_COLS), 1.0),
        'page_table': page_table,
        'w_in': nrm(k[7], (DEPTH, D_MODEL, IN_COLS), D_MODEL ** -0.5),
        'mu_shift': jax.random.uniform(k[8], (DEPTH, A_COLS), f32),
        'rw_w0': -1.0 + nrm(k[9], (DEPTH, D_A), 0.5),
        'rw_w2': nrm(k[10], (DEPTH, R_W, D_A), R_W ** -0.5),
        'rw_a0': nrm(k[11], (DEPTH, D_A), 0.1),
        'rw_a2': nrm(k[12], (DEPTH, R_A, D_A), R_A ** -0.5),
        'rw_kk': 0.85 + nrm(k[13], (DEPTH, D_A), 0.05),
        'rw_ka': 1.0 + nrm(k[14], (DEPTH, D_A), 0.05),
        'rw_rk': nrm(k[15], (DEPTH, H_A, HS), 0.1),
        'rw_gn_g': 1.0 + nrm(k[16], (DEPTH, D_A), 0.05),
        'rw_gn_b': nrm(k[17], (DEPTH, D_A), 0.01),
        'rw_v0': nrm(k[18], (DEPTH - 1, D_A), 0.1),
        'rw_v1': nrm(k[19], (DEPTH - 1, D_A, R_V), D_A ** -0.5),
        'rw_v2': nrm(k[20], (DEPTH - 1, R_V, D_A), R_V ** -0.5),
        'cmp_w1': nrm(k[21], (DEPTH, 2, CMP_LEN * HD, CMP_HID), (CMP_LEN * HD) ** -0.5),
        'cmp_b1': nrm(k[22], (DEPTH, 2, CMP_HID), 0.01),
        'cmp_w2': nrm(k[23], (DEPTH, 2, CMP_HID, HD), CMP_HID ** -0.5),
        'cmp_b2': nrm(k[24], (DEPTH, 2, HD), 0.01),
        'rel_bias': nrm(k[25], (NUM_BUCKETS, H_B), 0.5),
        'w_up_a': nrm(k[26], (DEPTH, D_A, D_MODEL), BETA * D_A ** -0.5),
        'w_up_b': nrm(k[27], (DEPTH, D_B, D_MODEL), BETA * D_B ** -0.5),
        'w_out': nrm(k[28], (DEPTH, D_MODEL, D_MODEL), BETA * D_MODEL ** -0.5),
        'ln_g': 1.0 + nrm(k[29], (DEPTH, D_MODEL), 0.05),
        'ln_b': nrm(k[30], (DEPTH, D_MODEL), 0.01),
    }


def reference(x_prompt, x_sample, cache_kv, cache_win_kv, state_wkv, state_shift, page_table,
              w_in, mu_shift, rw_w0, rw_w2, rw_a0, rw_a2, rw_kk, rw_ka, rw_rk, rw_gn_g, rw_gn_b,
              rw_v0, rw_v1, rw_v2, cmp_w1, cmp_b1, cmp_w2, cmp_b2, rel_bias,
              w_up_a, w_up_b, w_out, ln_g, ln_b):
    p = {'w_in': w_in, 'mu_shift': mu_shift, 'rw_w0': rw_w0, 'rw_w2': rw_w2, 'rw_a0': rw_a0,
         'rw_a2': rw_a2, 'rw_kk': rw_kk, 'rw_ka': rw_ka, 'rw_rk': rw_rk, 'rw_gn_g': rw_gn_g,
         'rw_gn_b': rw_gn_b, 'rw_v0': rw_v0, 'rw_v1': rw_v1, 'rw_v2': rw_v2, 'cmp_w1': cmp_w1,
         'cmp_b1': cmp_b1, 'cmp_w2': cmp_w2, 'cmp_b2': cmp_b2, 'rel_bias': rel_bias,
         'w_up_a': w_up_a, 'w_up_b': w_up_b, 'w_out': w_out, 'ln_g': ln_g, 'ln_b': ln_b}
    dec_b, n_pages = page_table.shape
    y_p, y_s = x_prompt, x_sample
    vf_p, vf_s = None, None
    st_p, st_s = [], []
    for l in range(DEPTH):
        y_p, vf_p, sp = trunk_layer(y_p, l, vf_p, None, p)
        past = {'kv': cache_kv[l][page_table].reshape(dec_b, n_pages * PAGE_SIZE, 4, G_KV, HD),
                'win': cache_win_kv[l], 'wkv': state_wkv[l], 'shift': state_shift[l]}
        y_s, vf_s, ss = trunk_layer(y_s, l, vf_s, past, p)
        st_p.append(sp)
        st_s.append(ss)
    kv_rows_prompt = jnp.stack([s[0] for s in st_p])
    win_kv_prompt = jnp.stack([s[1] for s in st_p])
    wkv_prompt = jnp.stack([s[2] for s in st_p])
    shift_prompt = jnp.stack([s[3] for s in st_p])
    kv_rows_sample = jnp.stack([s[0] for s in st_s])
    win_kv_sample = jnp.stack([s[1] for s in st_s])
    wkv_sample = jnp.stack([s[2] for s in st_s])
    shift_sample = jnp.stack([s[3] for s in st_s])
    return (y_p, y_s, kv_rows_prompt, win_kv_prompt, wkv_prompt, shift_prompt,
            kv_rows_sample, win_kv_sample, wkv_sample, shift_sample)
```

```python
import functools
import math

import numpy as np
import jax
import jax.numpy as jnp
from jax import lax
from jax.experimental import pallas as pl
from jax.experimental.pallas import tpu as pltpu

D_MODEL = 2048
DEPTH = 2
PAGE_SIZE = 128
HS = 64
D_A = D_MODEL // 2
H_A = D_A // HS
R_W = 64
R_A = 64
R_V = 32
GN_EPS = 64e-5
HD = 64
D_B = D_MODEL // 2
H_B = D_B // HD
G_KV = 4
R_Q = H_B // G_KV
KV_W = G_KV * HD
CMP_LEN = 32
CMP_STRIDE = 16
CMP_HID = 128
SEL_LEN = 64
N_TOP = 16
WIN = 512
NUM_BUCKETS = 32
MAX_DIST = 128
SCALE = HD ** -0.5
A_COLS = 4 * D_A + R_W + R_A
ALPHA = (2 * DEPTH) ** 0.25
LN_EPS = 1e-5
NEG = -1e30

F32 = jnp.float32
BF16 = jnp.bfloat16

LANES = 128
VMEM_LIMIT = 56 * 1024 * 1024
CHUNK = 64
TQ = 128
KB = 128
GL_PAD = LANES
HB_COLS = D_B + 6 * KV_W + G_KV * GL_PAD + D_B

NT = (((1,), (1,)), ((), ()))
TN = (((0,), (0,)), ((), ()))


def _params(sem):
    return pltpu.CompilerParams(dimension_semantics=sem, vmem_limit_bytes=VMEM_LIMIT)


def _bdot(a, b, dims=None):
    a = a.astype(BF16)
    b = b.astype(BF16)
    if dims is None:
        return jnp.dot(a, b, preferred_element_type=F32)
    return lax.dot_general(a, b, dims, preferred_element_type=F32)


def _sigmoid(x):
    return 1.0 / (1.0 + jnp.exp(-x))


def _silu(x):
    return x * _sigmoid(x)


def _gelu_tanh(x):
    return 0.5 * x * (1.0 + jnp.tanh(math.sqrt(2.0 / math.pi) * (x + 0.044715 * (x * x * x))))


def _mm_kernel(x_ref, w_ref, o_ref):
    o_ref[...] = _bdot(x_ref[...], w_ref[...])


def _matmul(x, w, tm, tn):
    m, k = x.shape
    n = w.shape[1]
    tm = min(tm, m)
    return pl.pallas_call(
        _mm_kernel,
        grid=(m // tm, n // tn),
        in_specs=[pl.BlockSpec((tm, k), lambda i, j: (i, 0)),
                  pl.BlockSpec((k, tn), lambda i, j: (0, j))],
        out_specs=pl.BlockSpec((tm, tn), lambda i, j: (i, j)),
        out_shape=jax.ShapeDtypeStruct((m, n), F32),
        compiler_params=_params(("parallel", "parallel")),
    )(x, w)


def _up_kernel(oa_ref, ob_ref, wa_ref, wb_ref, ga_ref, gb_ref, o_ref):
    ua = _bdot(oa_ref[...], wa_ref[...])
    ub = _bdot(ob_ref[...], wb_ref[...])
    o_ref[...] = (_sigmoid(ga_ref[...]) * ua + _sigmoid(gb_ref[...]) * ub).astype(o_ref.dtype)


def _up_gate(o_a, o_b, w_up_a, w_up_b, hg, tm=512, tn=1024):
    m = o_a.shape[0]
    tm = min(tm, m)
    nb = D_MODEL // tn
    return pl.pallas_call(
        _up_kernel,
        grid=(m // tm, nb),
        in_specs=[pl.BlockSpec((tm, D_A), lambda i, j: (i, 0)),
                  pl.BlockSpec((tm, D_B), lambda i, j: (i, 0)),
                  pl.BlockSpec((D_A, tn), lambda i, j: (0, j)),
                  pl.BlockSpec((D_B, tn), lambda i, j: (0, j)),
                  pl.BlockSpec((tm, tn), lambda i, j: (i, j)),
                  pl.BlockSpec((tm, tn), lambda i, j: (i, j + nb))],
        out_specs=pl.BlockSpec((tm, tn), lambda i, j: (i, j)),
        out_shape=jax.ShapeDtypeStruct((m, D_MODEL), BF16),
        compiler_params=_params(("parallel", "parallel")),
    )(o_a, o_b, w_up_a, w_up_b, hg, hg)


def _out_ln_kernel(m_ref, w_ref, x_ref, g_ref, b_ref, o_ref):
    u = ALPHA * x_ref[...] + _bdot(m_ref[...], w_ref[...])
    mu = jnp.mean(u, axis=-1, keepdims=True)
    d = u - mu
    var = jnp.mean(d * d, axis=-1, keepdims=True)
    o_ref[...] = d * lax.rsqrt(var + LN_EPS) * g_ref[...] + b_ref[...]


def _out_ln(merged, w_out, x, ln_g, ln_b, tm=256):
    m = x.shape[0]
    tm = min(tm, m)
    return pl.pallas_call(
        _out_ln_kernel,
        grid=(m // tm,),
        in_specs=[pl.BlockSpec((tm, D_MODEL), lambda i: (i, 0)),
                  pl.BlockSpec((D_MODEL, D_MODEL), lambda i: (0, 0)),
                  pl.BlockSpec((tm, D_MODEL), lambda i: (i, 0)),
                  pl.BlockSpec((1, D_MODEL), lambda i: (0, 0)),
                  pl.BlockSpec((1, D_MODEL), lambda i: (0, 0))],
        out_specs=pl.BlockSpec((tm, D_MODEL), lambda i: (i, 0)),
        out_shape=jax.ShapeDtypeStruct((m, D_MODEL), F32),
        compiler_params=_params(("parallel",)),
    )(merged, w_out, x, ln_g, ln_b)


def _rwkv_premix(xm, vfirst, w0, w2, a0, a2, kkp, ka, vgate):
    r = xm[:, 0:D_A]
    k = xm[:, D_A:2 * D_A]
    v = xm[:, 2 * D_A:3 * D_A]
    w_lo = xm[:, 3 * D_A:3 * D_A + R_W]
    a_lo = xm[:, 3 * D_A + R_W:3 * D_A + R_W + R_A]
    z = xm[:, 3 * D_A + R_W + R_A:A_COLS]
    t = w0 + _bdot(jnp.tanh(w_lo), w2)
    lw = -math.exp(-0.5) * _sigmoid(t)
    if vgate is not None:
        v0, v1, v2 = vgate
        vg = _sigmoid(v0 + _bdot(_bdot(v, v1), v2))
        v = v + (vfirst - v) * vg
    lr = _sigmoid(a0 + _bdot(a_lo, a2))
    kkr = k * kkp
    k2 = k * (1.0 + (lr - 1.0) * ka)
    return r, lw, k2, v, lr, kkr, z


def _head_post(y, r, k2, v, z, rk, gn_g, gn_b):
    mu = jnp.mean(y, axis=-1, keepdims=True)
    d = y - mu
    var = jnp.mean(d * d, axis=-1, keepdims=True)
    yn = d * lax.rsqrt(var + GN_EPS) * gn_g + gn_b
    bonus = jnp.sum(r * k2 * rk, axis=-1, keepdims=True) * v
    return (yn + bonus) * _silu(z)


def _rwkv_prompt_kernel(*refs, has_vgate):
    if has_vgate:
        (fa_ref, vf_ref, mu_ref, w0_ref, w2_ref, a0_ref, a2_ref, kkp_ref, ka_ref, rk_ref, gg_ref, gb_ref,
         v0_ref, v1_ref, v2_ref, o_ref, s_out_ref, s_ref, last_ref) = refs
    else:
        (fa_ref, mu_ref, w0_ref, w2_ref, a0_ref, a2_ref, kkp_ref, ka_ref, rk_ref, gg_ref, gb_ref,
         o_ref, vf_out_ref, s_out_ref, s_ref, last_ref) = refs
    c = pl.program_id(1)
    nc = pl.num_programs(1)
    C = CHUNK

    @pl.when(c == 0)
    def _():
        s_ref[...] = jnp.zeros_like(s_ref)
        last_ref[...] = jnp.zeros_like(last_ref)

    x = fa_ref[...]
    row = lax.broadcasted_iota(jnp.int32, (C, 1), 0)
    prev = jnp.where(row == 0, last_ref[0:1, :], pltpu.roll(x, 1, axis=0))
    last_ref[0:1, :] = x[C - 1:C, :]
    xm = x + (prev - x) * mu_ref[...]
    if has_vgate:
        vgate = (v0_ref[...], v1_ref[...], v2_ref[...])
        vfirst = vf_ref[...]
    else:
        vgate, vfirst = None, None
    r, lw, k2, v, lr, kkr, z = _rwkv_premix(xm, vfirst, w0_ref[...], w2_ref[...], a0_ref[...], a2_ref[...],
                                            kkp_ref[...], ka_ref[...], vgate)
    if not has_vgate:
        vf_out_ref[...] = v

    ri = lax.broadcasted_iota(jnp.int32, (C, C), 0)
    ci = lax.broadcasted_iota(jnp.int32, (C, C), 1)
    tri_i = ri >= ci
    tri_s = ri > ci
    tri_b = jnp.where(tri_i, 1.0, 0.0).astype(BF16)
    eye = jnp.where(ri == ci, 1.0, 0.0).astype(F32)
    lw_hi = lw.astype(BF16)
    lw_lo = (lw - lw_hi.astype(F32)).astype(BF16)
    L = jnp.dot(tri_b, lw_hi, preferred_element_type=F32) + jnp.dot(tri_b, lw_lo, preferred_element_type=F32)
    LC = L[C - 1:C, :]
    e_in = jnp.exp(L)
    e_ex = jnp.exp(L - lw)
    e_neg = jnp.exp(-L)
    e_rem = jnp.exp(LC - L)
    pc = jnp.exp(LC)
    rk = rk_ref[...]
    gg = gg_ref[...]
    gb = gb_ref[...]

    for h in range(H_A):
        sl = slice(h * HS, (h + 1) * HS)
        kk = kkr[:, sl]
        kk = kk * lax.rsqrt(jnp.maximum(jnp.sum(kk * kk, axis=-1, keepdims=True), 1e-24))
        r_h, k_h, v_h = r[:, sl], k2[:, sl], v[:, sl]
        b_h = kk * lr[:, sl]
        at = (-kk) * e_ex[:, sl]
        rt = r_h * e_in[:, sl]
        bt = b_h * e_neg[:, sl]
        kt = k_h * e_neg[:, sl]
        bh = b_h * e_rem[:, sl]
        kh = k_h * e_rem[:, sl]
        g = _bdot(jnp.concatenate([at, rt], axis=0), jnp.concatenate([bt, kt], axis=0), NT)
        a_ab = jnp.where(tri_s, g[:C, :C], 0.0)
        a_ak = jnp.where(tri_s, g[:C, C:], 0.0)
        a_rb = jnp.where(tri_i, g[C:, :C], 0.0)
        a_rk = jnp.where(tri_i, g[C:, C:], 0.0)
        tm = eye + a_ab
        ap = a_ab
        n = 1
        while 2 * n < C:
            ap = _bdot(ap, ap)
            tm = tm + _bdot(tm, ap)
            n *= 2
        akv = _bdot(a_ak, v_h)
        wu = _bdot(tm, jnp.concatenate([at, akv], axis=1))
        arw = _bdot(a_rb, wu)
        rh = rt + arw[:, :HS]
        yh = arw[:, HS:] + _bdot(a_rk, v_h)
        s = s_ref[h]
        y = _bdot(rh, s, NT) + yh
        bw = _bdot(bh, wu[:, :HS], TN)
        nt = _bdot(jnp.concatenate([wu[:, HS:], v_h], axis=0), jnp.concatenate([bh, kh], axis=0), TN)
        s_ref[h] = s * pc[:, sl] + _bdot(s, bw, NT) + nt
        o_ref[:, sl] = _head_post(y, r_h, k_h, v_h, z[:, sl], rk[:, sl], gg[:, sl], gb[:, sl]).astype(o_ref.dtype)

    @pl.when(c == nc - 1)
    def _():
        s_out_ref[0] = s_ref[...]


def _rwkv_prompt(ha, vfirst, p, bsz, t):
    has_vgate = vfirst is not None
    nc = t // CHUNK
    row_spec = lambda w: pl.BlockSpec((CHUNK, w), lambda b, c: (b * nc + c, 0))
    full = lambda a: pl.BlockSpec(a.shape, lambda b, c: (0,) * a.ndim)
    ins = [ha]
    in_specs = [row_spec(A_COLS)]
    if has_vgate:
        ins.append(vfirst)
        in_specs.append(row_spec(D_A))
    names = ['mu', 'w0', 'w2', 'a0', 'a2', 'kk', 'ka', 'rk', 'gn_g', 'gn_b'] + (['v0', 'v1', 'v2'] if has_vgate else [])
    for nme in names:
        ins.append(p[nme])
        in_specs.append(full(p[nme]))
    out_shape = [jax.ShapeDtypeStruct((bsz * t, D_A), BF16)]
    out_specs = [row_spec(D_A)]
    if not has_vgate:
        out_shape.append(jax.ShapeDtypeStruct((bsz * t, D_A), F32))
        out_specs.append(row_spec(D_A))
    out_shape.append(jax.ShapeDtypeStruct((bsz, H_A, HS, HS), F32))
    out_specs.append(pl.BlockSpec((1, H_A, HS, HS), lambda b, c: (b, 0, 0, 0)))
    outs = pl.pallas_call(
        functools.partial(_rwkv_prompt_kernel, has_vgate=has_vgate),
        grid=(bsz, nc),
        in_specs=in_specs,
        out_specs=out_specs,
        out_shape=out_shape,
        scratch_shapes=[pltpu.VMEM((H_A, HS, HS), F32), pltpu.VMEM((8, A_COLS), F32)],
        compiler_params=_params(("parallel", "arbitrary")),
    )(*ins)
    if has_vgate:
        return outs[0], vfirst, outs[1]
    return outs[0], outs[1], outs[2]


def _rwkv_sample_kernel(*refs, has_vgate, bt):
    if has_vgate:
        (fa_ref, prev_ref, s_in_ref, vf_ref, mu_ref, w0_ref, w2_ref, a0_ref, a2_ref, kkp_ref, ka_ref, rk_ref,
         gg_ref, gb_ref, v0_ref, v1_ref, v2_ref, o_ref, s_out_ref, ops_ref, y_ref) = refs
    else:
        (fa_ref, prev_ref, s_in_ref, mu_ref, w0_ref, w2_ref, a0_ref, a2_ref, kkp_ref, ka_ref, rk_ref,
         gg_ref, gb_ref, o_ref, vf_out_ref, s_out_ref, ops_ref, y_ref) = refs
    x = fa_ref[...]
    xm = x + (prev_ref[...] - x) * mu_ref[...]
    if has_vgate:
        vgate = (v0_ref[...], v1_ref[...], v2_ref[...])
        vfirst = vf_ref[...]
    else:
        vgate, vfirst = None, None
    r, lw, k2, v, lr, kkr, z = _rwkv_premix(xm, vfirst, w0_ref[...], w2_ref[...], a0_ref[...], a2_ref[...],
                                            kkp_ref[...], ka_ref[...], vgate)
    if not has_vgate:
        vf_out_ref[...] = v
    w = jnp.exp(lw)
    for h in range(H_A):
        sl = slice(h * HS, (h + 1) * HS)
        kk = kkr[:, sl]
        kk = kk * lax.rsqrt(jnp.maximum(jnp.sum(kk * kk, axis=-1, keepdims=True), 1e-24))
        ops_ref[0, :, sl] = -kk
        ops_ref[1, :, sl] = kk * lr[:, sl]
    ops_ref[2] = w
    ops_ref[3] = k2
    ops_ref[4] = v
    ops_ref[5] = r
    ri = lax.broadcasted_iota(jnp.int32, (HS, HS), 0)
    ci = lax.broadcasted_iota(jnp.int32, (HS, HS), 1)
    eye = jnp.where(ri == ci, 1.0, 0.0).astype(F32)

    for b in range(bt):
        for h in range(H_A):
            sl = slice(h * HS, (h + 1) * HS)
            a_row = ops_ref[0, b:b + 1, sl]
            b_row = ops_ref[1, b:b + 1, sl]
            w_row = ops_ref[2, b:b + 1, sl]
            k_row = ops_ref[3, b:b + 1, sl]
            v_row = ops_ref[4, b:b + 1, sl]
            r_row = ops_ref[5, b:b + 1, sl]
            s = s_in_ref[b, h]
            sa = jnp.sum(s * a_row, axis=-1, keepdims=True)
            v_col = jnp.sum(eye * v_row, axis=-1, keepdims=True)
            s_new = s * w_row + sa * b_row + v_col * k_row
            s_out_ref[b, h] = s_new
            y_col = jnp.sum(s_new * r_row, axis=-1, keepdims=True)
            y_ref[b:b + 1, sl] = jnp.sum(eye * y_col, axis=0, keepdims=True)
    y = y_ref[...]
    rk = rk_ref[...]
    gg = gg_ref[...]
    gb = gb_ref[...]
    for h in range(H_A):
        sl = slice(h * HS, (h + 1) * HS)
        o_ref[:, sl] = _head_post(y[:, sl], r[:, sl], k2[:, sl], v[:, sl], z[:, sl], rk[:, sl], gg[:, sl],
                                  gb[:, sl]).astype(o_ref.dtype)


def _rwkv_sample(ha, prev, s_in, vfirst, p, bt=8):
    bsz = ha.shape[0]
    has_vgate = vfirst is not None
    row_spec = lambda w: pl.BlockSpec((bt, w), lambda i: (i, 0))
    full = lambda a: pl.BlockSpec(a.shape, lambda i: (0,) * a.ndim)
    st_spec = pl.BlockSpec((bt, H_A, HS, HS), lambda i: (i, 0, 0, 0))
    ins = [ha, prev, s_in]
    in_specs = [row_spec(A_COLS), row_spec(A_COLS), st_spec]
    if has_vgate:
        ins.append(vfirst)
        in_specs.append(row_spec(D_A))
    names = ['mu', 'w0', 'w2', 'a0', 'a2', 'kk', 'ka', 'rk', 'gn_g', 'gn_b'] + (['v0', 'v1', 'v2'] if has_vgate else [])
    for nme in names:
        ins.append(p[nme])
        in_specs.append(full(p[nme]))
    out_shape = [jax.ShapeDtypeStruct((bsz, D_A), BF16)]
    out_specs = [row_spec(D_A)]
    if not has_vgate:
        out_shape.append(jax.ShapeDtypeStruct((bsz, D_A), F32))
        out_specs.append(row_spec(D_A))
    out_shape.append(jax.ShapeDtypeStruct((bsz, H_A, HS, HS), F32))
    out_specs.append(st_spec)
    outs = pl.pallas_call(
        functools.partial(_rwkv_sample_kernel, has_vgate=has_vgate, bt=bt),
        grid=(bsz // bt,),
        in_specs=in_specs,
        out_specs=out_specs,
        out_shape=out_shape,
        scratch_shapes=[pltpu.VMEM((6, bt, D_A), F32), pltpu.VMEM((bt, D_A), F32)],
        compiler_params=_params(("parallel",)),
    )(*ins)
    if has_vgate:
        return outs[0], vfirst, outs[1]
    return outs[0], outs[1], outs[2]


def _compress_rows(load_rows, w1_ref, b1, w2, b2, kv, nch):
    accs = [jnp.zeros((nch, 2 * CMP_HID), F32) for _ in range(G_KV)]
    for tau in range(CMP_STRIDE):
        for pair in range(G_KV // 2):
            rows = load_rows(tau, pair).astype(BF16)
            for parity in range(2):
                g = 2 * pair + parity
                accs[g] = accs[g] + jnp.dot(rows, w1_ref[kv, tau, parity], preferred_element_type=F32)
    outs = []
    for acc in accs:
        h = acc[:, :CMP_HID] + pltpu.roll(acc[:, CMP_HID:], nch - 1, axis=0) + b1
        outs.append(_bdot(_gelu_tanh(h), w2) + b2)
    return outs


def _kv_prep_kernel(kc_ref, ks_ref, kw_ref, w1_ref, b1_ref, w2_ref, b2_ref,
                    ks_o, vs_o, kw_o, vw_o, kc_o, vc_o, *, nch):
    for g in range(G_KV):
        sl = slice(g * HD, (g + 1) * HD)
        sl2 = slice(KV_W + g * HD, KV_W + (g + 1) * HD)
        ks_o[0, g] = ks_ref[:, sl].astype(BF16)
        vs_o[0, g] = ks_ref[:, sl2].astype(BF16)
        kw_o[0, g] = kw_ref[:, sl].astype(BF16)
        vw_o[0, g] = kw_ref[:, sl2].astype(BF16)
    npair = KV_W // LANES
    for kv, out in ((0, kc_o), (1, vc_o)):
        load = lambda tau, pair, kv=kv: kc_ref[pl.ds(2 * npair * tau + kv * npair + pair, nch,
                                                     stride=2 * npair * CMP_STRIDE), :]
        res = _compress_rows(load, w1_ref, b1_ref[kv], w2_ref[kv], b2_ref[kv], kv, nch)
        for g in range(G_KV):
            out[0, g] = res[g].astype(BF16)


def _kv_prep(hb, hc, cp, bsz, t):
    nch = t // CMP_STRIDE
    blk = lambda j: pl.BlockSpec((t, 2 * KV_W), lambda b: (b, j))
    full = lambda a: pl.BlockSpec(a.shape, lambda b: (0,) * a.ndim)
    c0 = D_B // (2 * KV_W)
    lane_rows = 2 * KV_W // LANES
    hc = hc.reshape(bsz * t * lane_rows, LANES)
    tok = jax.ShapeDtypeStruct((bsz, G_KV, t, HD), BF16)
    cmp = jax.ShapeDtypeStruct((bsz, G_KV, nch, HD), BF16)
    tok_spec = pl.BlockSpec((1, G_KV, t, HD), lambda b: (b, 0, 0, 0))
    cmp_spec = pl.BlockSpec((1, G_KV, nch, HD), lambda b: (b, 0, 0, 0))
    return pl.pallas_call(
        functools.partial(_kv_prep_kernel, nch=nch),
        grid=(bsz,),
        in_specs=[pl.BlockSpec((t * lane_rows, LANES), lambda b: (b, 0)), blk(c0 + 1), blk(c0 + 2),
                  full(cp['w1']), full(cp['b1']), full(cp['w2']), full(cp['b2'])],
        out_specs=[tok_spec] * 4 + [cmp_spec] * 2,
        out_shape=[tok] * 4 + [cmp] * 2,
        compiler_params=_params(("parallel",)),
    )(hc, hb, hb, cp['w1'], cp['b1'], cp['w2'], cp['b2'])


def _softmax_block(qs, k, v, bias, mask, m_ref, l_ref, acc_ref):
    s = lax.dot_general(qs, k, NT, preferred_element_type=F32) + bias
    if mask is not None:
        s = jnp.where(mask, s, NEG)
    m_prev = m_ref[...]
    m_new = jnp.maximum(m_prev, jnp.max(s, axis=-1, keepdims=True))
    p = jnp.exp(s - m_new)
    if mask is not None:
        p = jnp.where(mask, p, 0.0)
    alpha = jnp.exp(m_prev - m_new)
    l_ref[...] = alpha * l_ref[...] + jnp.sum(p, axis=-1, keepdims=True)
    acc_ref[...] = alpha * acc_ref[...] + jnp.dot(p.astype(BF16), v, preferred_element_type=F32)
    m_ref[...] = m_new


def _nsa_prompt_kernel(q_ref, gl_ref, zb_ref, kc_ref, vc_ref, ks_ref, vs_ref, kw_ref, vw_ref,
                       bc_ref, d0_ref, d1_ref, far_ref, ov_ref, o_ref,
                       m_ref, l_ref, acc_ref, sel_ref, *, t, nch):
    g = pl.program_id(1)
    qt = pl.program_id(2)
    nsel = t // SEL_LEN
    rows = R_Q * TQ
    q = q_ref[...] * SCALE
    qs = jnp.concatenate([q[:, r * HD:(r + 1) * HD] for r in range(R_Q)], axis=0).astype(BF16)
    qpos = qt * TQ + lax.broadcasted_iota(jnp.int32, (TQ, 1), 0)
    rep = lambda x: jnp.concatenate([x] * R_Q, axis=0)

    def reset():
        m_ref[...] = jnp.full_like(m_ref, NEG)
        l_ref[...] = jnp.zeros_like(l_ref)
        acc_ref[...] = jnp.zeros_like(acc_ref)

    def result():
        l = l_ref[...]
        return acc_ref[...] / jnp.where(l > 0.0, l, 1.0)

    ncol = lax.broadcasted_iota(jnp.int32, (TQ, nch), 1)
    mask_c = (qpos >= ncol * CMP_STRIDE + (CMP_LEN - 1)) & (ncol < nch - 1)
    mask_c4 = rep(mask_c)
    bias_c = jnp.concatenate([bc_ref[r] for r in range(R_Q)], axis=0)
    s = lax.dot_general(qs, kc_ref[0, 0], NT, preferred_element_type=F32) + bias_c
    s = jnp.where(mask_c4, s, NEG)
    p = jnp.where(mask_c4, jnp.exp(s - jnp.max(s, axis=-1, keepdims=True)), 0.0)
    l = jnp.sum(p, axis=-1, keepdims=True)
    p = p / jnp.where(l > 0.0, l, 1.0)
    o_c = jnp.dot(p.astype(BF16), vc_ref[0, 0], preferred_element_type=F32)
    psum = p[0:TQ]
    for r in range(1, R_Q):
        psum = psum + p[r * TQ:(r + 1) * TQ]
    p_hi = psum.astype(BF16)
    p_lo = (psum - p_hi.astype(F32)).astype(BF16)
    ov = ov_ref[...]
    imp = jnp.dot(p_hi, ov, preferred_element_type=F32) + jnp.dot(p_lo, ov, preferred_element_type=F32)
    blk = lax.broadcasted_iota(jnp.int32, (TQ, nsel), 1)
    cur = qpos // SEL_LEN
    imp = jnp.where(blk * SEL_LEN <= qpos, imp, NEG)
    imp = jnp.where((blk == 0) | (blk == cur) | (blk == cur - 1), -NEG, imp)
    rank = jnp.zeros((TQ, nsel), F32)
    for s2 in range(nsel):
        col = imp[:, s2:s2 + 1]
        rank = rank + jnp.where((col > imp) | ((col == imp) & (blk > s2)), 1.0, 0.0)
    sel = jnp.where(rank < float(min(N_TOP, nsel)), 1.0, 0.0).astype(BF16)
    er = lax.broadcasted_iota(jnp.int32, (nsel, t), 0)
    ec = lax.broadcasted_iota(jnp.int32, (nsel, t), 1)
    expand = jnp.where(ec // SEL_LEN == er, 1.0, 0.0).astype(BF16)
    sel_ref[...] = jnp.dot(sel, expand, preferred_element_type=F32)

    iq = lax.broadcasted_iota(jnp.int32, (TQ, KB), 0)
    ik = lax.broadcasted_iota(jnp.int32, (TQ, KB), 1)
    causal4 = rep(ik <= iq)
    upper4 = rep(ik > iq)
    bias_d0 = jnp.concatenate([d0_ref[r] for r in range(R_Q)], axis=0)
    bias_d1 = jnp.concatenate([d1_ref[r] for r in range(R_Q)], axis=0)
    bias_far = jnp.concatenate([jnp.full((TQ, KB), far_ref[g * R_Q + r], F32) for r in range(R_Q)], axis=0)

    def sel_mask(kb):
        return rep(sel_ref[:, pl.ds(pl.multiple_of(kb * KB, KB), KB)] > 0.5)

    def kv(kref, vref, kb):
        off = pl.multiple_of(kb * KB, KB)
        return kref[0, 0, pl.ds(off, KB), :], vref[0, 0, pl.ds(off, KB), :]

    reset()

    def far_sel(kb, carry):
        k, v = kv(ks_ref, vs_ref, kb)
        _softmax_block(qs, k, v, bias_far, sel_mask(kb), m_ref, l_ref, acc_ref)
        return carry

    lax.fori_loop(0, jnp.maximum(qt - 1, 0), far_sel, 0)

    @pl.when(qt >= 1)
    def _():
        k, v = kv(ks_ref, vs_ref, qt - 1)
        _softmax_block(qs, k, v, bias_d1, sel_mask(qt - 1), m_ref, l_ref, acc_ref)

    k, v = kv(ks_ref, vs_ref, qt)
    _softmax_block(qs, k, v, bias_d0, sel_mask(qt) & causal4, m_ref, l_ref, acc_ref)
    o_s = result()

    reset()
    nwin = WIN // KB

    @pl.when(qt >= nwin)
    def _():
        k, v = kv(kw_ref, vw_ref, qt - nwin)
        _softmax_block(qs, k, v, bias_far, upper4, m_ref, l_ref, acc_ref)

    def far_win(kb, carry):
        k, v = kv(kw_ref, vw_ref, kb)
        _softmax_block(qs, k, v, bias_far, None, m_ref, l_ref, acc_ref)
        return carry

    lax.fori_loop(jnp.maximum(qt - nwin + 1, 0), jnp.maximum(qt - 1, 0), far_win, 0)

    @pl.when(qt >= 1)
    def _():
        k, v = kv(kw_ref, vw_ref, qt - 1)
        _softmax_block(qs, k, v, bias_d1, None, m_ref, l_ref, acc_ref)

    k, v = kv(kw_ref, vw_ref, qt)
    _softmax_block(qs, k, v, bias_d0, causal4, m_ref, l_ref, acc_ref)
    o_w = result()

    gate = _sigmoid(gl_ref[...])
    outs = []
    for r in range(R_Q):
        rs = slice(r * TQ, (r + 1) * TQ)
        outs.append(gate[:, 3 * r:3 * r + 1] * o_c[rs] + gate[:, 3 * r + 1:3 * r + 2] * o_s[rs]
                    + gate[:, 3 * r + 2:3 * r + 3] * o_w[rs])
    o = jnp.concatenate(outs, axis=1)
    o_ref[...] = (o * _silu(zb_ref[...])).astype(o_ref.dtype)


def _nsa_prompt(hb, kvp, tabs, bsz, t):
    ks_t, vs_t, kw_t, vw_t, kc_t, vc_t = kvp
    nch = t // CMP_STRIDE
    nqt = t // TQ
    gw = R_Q * HD
    tok_spec = pl.BlockSpec((1, 1, t, HD), lambda b, g, i: (b, g, 0, 0))
    cmp_spec = pl.BlockSpec((1, 1, nch, HD), lambda b, g, i: (b, g, 0, 0))
    gl0 = (D_B + 6 * KV_W) // GL_PAD
    zb0 = (D_B + 6 * KV_W + G_KV * GL_PAD) // gw
    return pl.pallas_call(
        functools.partial(_nsa_prompt_kernel, t=t, nch=nch),
        grid=(bsz, G_KV, nqt),
        in_specs=[pl.BlockSpec((TQ, gw), lambda b, g, i: (b * nqt + i, g)),
                  pl.BlockSpec((TQ, GL_PAD), lambda b, g, i: (b * nqt + i, gl0 + g)),
                  pl.BlockSpec((TQ, gw), lambda b, g, i: (b * nqt + i, zb0 + g)),
                  cmp_spec, cmp_spec, tok_spec, tok_spec, tok_spec, tok_spec,
                  pl.BlockSpec((R_Q, TQ, nch), lambda b, g, i: (g, i, 0)),
                  pl.BlockSpec((R_Q, TQ, KB), lambda b, g, i: (g, 0, 0)),
                  pl.BlockSpec((R_Q, TQ, KB), lambda b, g, i: (g, 0, 0)),
                  pl.BlockSpec(memory_space=pltpu.SMEM),
                  pl.BlockSpec(tabs['overlap'].shape, lambda b, g, i: (0, 0))],
        out_specs=pl.BlockSpec((TQ, gw), lambda b, g, i: (b * nqt + i, g)),
        out_shape=jax.ShapeDtypeStruct((bsz * t, D_B), BF16),
        scratch_shapes=[pltpu.VMEM((R_Q * TQ, 1), F32), pltpu.VMEM((R_Q * TQ, 1), F32),
                        pltpu.VMEM((R_Q * TQ, HD), F32), pltpu.VMEM((TQ, t), F32)],
        compiler_params=_params(("parallel", "parallel", "arbitrary")),
    )(hb, hb, hb, kc_t, vc_t, ks_t, vs_t, kw_t, vw_t, tabs['bias_c'], tabs['d0'], tabs['d1'], tabs['far'],
      tabs['overlap'])


def _col_softmax(s, mask, s_new):
    sm = jnp.where(mask, s, NEG)
    m = jnp.maximum(jnp.max(sm, axis=0, keepdims=True), s_new)
    p = jnp.where(mask, jnp.exp(sm - m), 0.0)
    p_new = jnp.exp(s_new - m)
    l = jnp.sum(p, axis=0, keepdims=True) + p_new
    return p / l, p_new / l


def _nsa_sample_kernel(pt_ref, *refs, n_pages, past):
    page_refs = refs[:n_pages]
    (win_ref, qbd_ref, new_ref, gl_ref, zb_ref, w1_ref, b1_ref, w2_ref, b2_ref,
     bc_ref, bs_ref, bw_ref, b0_ref, ovt_ref, gsum_ref, bdm_ref, o_ref, kc_s, vc_s, sc_s) = refs[n_pages:]
    del pt_ref
    nch = past // CMP_STRIDE
    ppc = PAGE_SIZE // CMP_STRIDE
    nsel = past // SEL_LEN + 1
    qbd = qbd_ref[0]

    npair = KV_W // LANES
    lane_rows = 4 * npair

    def tok_rows(pr, c, pair):
        return pr[0, 0, pl.ds(c * npair + pair, PAGE_SIZE, stride=lane_rows), :]

    for kv, dst in ((0, kc_s), (1, vc_s)):
        def load(tau, pair, kv=kv):
            return jnp.concatenate(
                [pr[0, 0, pl.ds(lane_rows * tau + kv * npair + pair, ppc, stride=lane_rows * CMP_STRIDE), :]
                 for pr in page_refs], axis=0)
        res = _compress_rows(load, w1_ref, b1_ref[kv], w2_ref[kv], b2_ref[kv], kv, nch)
        for g in range(G_KV):
            dst[:, g * HD:(g + 1) * HD] = res[g]
    nrow = lax.broadcasted_iota(jnp.int32, (nch, H_B), 0)
    s_c = _bdot(kc_s[...], qbd, NT) + bc_ref[...]
    mask_c = nrow < nch - 1
    sm = jnp.where(mask_c, s_c, NEG)
    p_c = jnp.where(mask_c, jnp.exp(sm - jnp.max(sm, axis=0, keepdims=True)), 0.0)
    p_c = p_c / jnp.sum(p_c, axis=0, keepdims=True)
    o_c = _bdot(p_c, vc_s[...], TN)
    p_hi = p_c.astype(BF16)
    p_lo = (p_c - p_hi.astype(F32)).astype(BF16)
    ovt = ovt_ref[...]
    imp = jnp.dot(ovt, p_hi, preferred_element_type=F32) + jnp.dot(ovt, p_lo, preferred_element_type=F32)
    i_hi = imp.astype(BF16)
    i_lo = (imp - i_hi.astype(F32)).astype(BF16)
    gs = gsum_ref[...]
    imp = jnp.dot(i_hi, gs, preferred_element_type=F32) + jnp.dot(i_lo, gs, preferred_element_type=F32)
    nsp = imp.shape[0]
    blk = lax.broadcasted_iota(jnp.int32, (nsp, H_B), 0)
    cur = past // SEL_LEN
    imp = jnp.where((blk == 0) | (blk == cur) | (blk == cur - 1), -NEG, imp)
    imp = jnp.where(blk < nsel, imp, 2.0 * NEG)
    rank = jnp.zeros((nsp, H_B), F32)
    for s2 in range(nsel):
        rowv = imp[s2:s2 + 1, :]
        rank = rank + jnp.where((rowv > imp) | ((rowv == imp) & (blk > s2)), 1.0, 0.0)
    sel = jnp.where(rank < float(N_TOP), 1.0, 0.0).astype(BF16)
    er = lax.broadcasted_iota(jnp.int32, (past, nsp), 0)
    ec = lax.broadcasted_iota(jnp.int32, (past, nsp), 1)
    expand = jnp.where(er // SEL_LEN == ec, 1.0, 0.0).astype(BF16)
    mask_s = jnp.dot(expand, sel, preferred_element_type=F32) > 0.5

    new = new_ref[0]
    ks_new = new[:, 2 * KV_W:3 * KV_W]
    vs_new = new[:, 3 * KV_W:4 * KV_W]
    kw_new = new[:, 4 * KV_W:5 * KV_W]
    vw_new = new[:, 5 * KV_W:6 * KV_W]
    rows8 = lambda x: jnp.broadcast_to(x, (8, H_B))
    b0 = b0_ref[...]
    for pi, pr in enumerate(page_refs):
        sc = _bdot(tok_rows(pr, 2, 0), qbd[:, 0:LANES], NT)
        for pair in range(1, npair):
            sc = sc + _bdot(tok_rows(pr, 2, pair), qbd[:, pair * LANES:(pair + 1) * LANES], NT)
        sc_s[pi * PAGE_SIZE:(pi + 1) * PAGE_SIZE, :] = sc
    s_s = sc_s[...] + bs_ref[...]
    s_new = _bdot(ks_new, qbd, NT)[0:1] + b0
    p_s, p_new = _col_softmax(s_s, mask_s, s_new)
    o_parts = [jnp.zeros((H_B, LANES), F32) for _ in range(npair)]
    for pi, pr in enumerate(page_refs):
        p_pg = p_s[pi * PAGE_SIZE:(pi + 1) * PAGE_SIZE, :].astype(BF16)
        for pair in range(npair):
            o_parts[pair] = o_parts[pair] + _bdot(p_pg, tok_rows(pr, 3, pair), TN)
    o_s = jnp.concatenate(o_parts, axis=1) + _bdot(rows8(p_new), vs_new, TN)
    win = win_ref[0, 0]
    nbuf = win.shape[0]
    wrow = lax.broadcasted_iota(jnp.int32, (nbuf, H_B), 0)
    s_w = _bdot(win[:, :KV_W], qbd, NT) + bw_ref[...]
    s_wn = _bdot(kw_new, qbd, NT)[0:1] + b0
    p_w, p_wn = _col_softmax(s_w, wrow >= nbuf + 1 - WIN, s_wn)
    o_w = _bdot(p_w, win[:, KV_W:], TN) + _bdot(rows8(p_wn), vw_new, TN)
    gate = _sigmoid(gl_ref[0])
    o = gate[:, 0:1] * o_c + gate[:, 1:2] * o_s + gate[:, 2:3] * o_w
    o = o * bdm_ref[...]
    o16 = o[:, 0:HD]
    for g in range(1, G_KV):
        o16 = o16 + o[:, g * HD:(g + 1) * HD]
    o_ref[0] = (o16 * _silu(zb_ref[0])).astype(o_ref.dtype)


def _nsa_sample(cache_l, l, page_table, win, qbd, new_rows, gl, zb, cp, tabs):
    bsz, n_pages = page_table.shape
    past = n_pages * PAGE_SIZE
    nch = past // CMP_STRIDE
    nbuf = win.shape[2]
    full = lambda a: pl.BlockSpec(a.shape, lambda b, pt: (0,) * a.ndim)
    page_specs = [pl.BlockSpec((1, 1) + cache_l.shape[2:], functools.partial(lambda b, pt, j: (l, pt[b, j], 0, 0), j=j))
                  for j in range(n_pages)]
    consts = [cp['w1'], cp['b1'], cp['w2'], cp['b2'], tabs['bias_c'], tabs['bias_s'], tabs['bias_w'], tabs['bias_0'],
              tabs['overlap_t'], tabs['gsum'], tabs['bdmask']]
    grid_spec = pltpu.PrefetchScalarGridSpec(
        num_scalar_prefetch=1,
        grid=(bsz,),
        in_specs=page_specs + [
            pl.BlockSpec((1, 1, nbuf, 2 * KV_W), lambda b, pt: (l, b, 0, 0)),
            pl.BlockSpec((1, H_B, KV_W), lambda b, pt: (b, 0, 0)),
            pl.BlockSpec((1, 8, 6 * KV_W), lambda b, pt: (b, 0, 0)),
            pl.BlockSpec((1, H_B, 3), lambda b, pt: (b, 0, 0)),
            pl.BlockSpec((1, H_B, HD), lambda b, pt: (b, 0, 0)),
        ] + [full(a) for a in consts],
        out_specs=pl.BlockSpec((1, H_B, HD), lambda b, pt: (b, 0, 0)),
        scratch_shapes=[pltpu.VMEM((nch, KV_W), F32), pltpu.VMEM((nch, KV_W), F32), pltpu.VMEM((past, H_B), F32)],
    )
    return pl.pallas_call(
        functools.partial(_nsa_sample_kernel, n_pages=n_pages, past=past),
        grid_spec=grid_spec,
        out_shape=jax.ShapeDtypeStruct((bsz, H_B, HD), BF16),
        compiler_params=_params(("arbitrary",)),
    )(page_table, *([cache_l] * n_pages), win, qbd, new_rows, gl, zb, *consts)


def _t5_bucket(dist):
    n = jnp.maximum(dist, 0)
    max_exact = NUM_BUCKETS // 2
    nf = jnp.maximum(n, 1).astype(F32)
    large = max_exact + (jnp.log(nf / max_exact) / math.log(MAX_DIST / max_exact)
                         * (NUM_BUCKETS - max_exact)).astype(jnp.int32)
    large = jnp.minimum(large, NUM_BUCKETS - 1)
    return jnp.where(n < max_exact, n, large)


def _bias_of(rel_bias, dist):
    return jnp.moveaxis(rel_bias[_t5_bucket(dist)], -1, 0).astype(F32)


def _overlap(nch, nsel):
    ci = np.arange(nch)[:, None] * CMP_STRIDE
    sj = np.arange(nsel)[None, :] * SEL_LEN
    ov = ((ci < sj + SEL_LEN) & (ci + CMP_LEN > sj)).astype(np.float32)
    ov[nch - 1:, :] = 0.0
    return ov


def _prompt_tables(rel_bias, t):
    nch = t // CMP_STRIDE
    tq = jnp.arange(t, dtype=jnp.int32)
    cmp_end = jnp.arange(nch, dtype=jnp.int32) * CMP_STRIDE + CMP_LEN - 1
    iq = jnp.arange(TQ, dtype=jnp.int32)[:, None]
    ik = jnp.arange(KB, dtype=jnp.int32)[None, :]
    return {
        'bias_c': _bias_of(rel_bias, tq[:, None] - cmp_end[None, :]),
        'd0': _bias_of(rel_bias, iq - ik),
        'd1': _bias_of(rel_bias, KB + iq - ik),
        'far': rel_bias[NUM_BUCKETS - 1].astype(F32),
        'overlap': jnp.asarray(_overlap(nch, t // SEL_LEN), BF16),
    }


def _sample_tables(rel_bias, past, nbuf):
    nch = past // CMP_STRIDE
    nsel = past // SEL_LEN + 1
    nsp = -(-nsel // 8) * 8
    cmp_end = jnp.arange(nch, dtype=jnp.int32) * CMP_STRIDE + CMP_LEN - 1
    ovt = np.zeros((nsp, nch), np.float32)
    ovt[:nsel] = _overlap(nch, nsel).T
    hh = np.arange(H_B)
    gsum = (hh[:, None] // R_Q == hh[None, :] // R_Q).astype(np.float32)
    bdm = (hh[:, None] // R_Q == np.arange(KV_W)[None, :] // HD).astype(np.float32)
    return {
        'bias_c': _bias_of(rel_bias, past - cmp_end).T,
        'bias_s': _bias_of(rel_bias, past - jnp.arange(past, dtype=jnp.int32)).T,
        'bias_w': _bias_of(rel_bias, nbuf - jnp.arange(nbuf, dtype=jnp.int32)).T,
        'bias_0': _bias_of(rel_bias, jnp.zeros((1,), jnp.int32)).T,
        'overlap_t': jnp.asarray(ovt, BF16),
        'gsum': jnp.asarray(gsum, BF16),
        'bdmask': jnp.asarray(bdm, F32),
    }


def _layer_params(l, w_in, mu_shift, rw_w0, rw_w2, rw_a0, rw_a2, rw_kk, rw_ka, rw_rk, rw_gn_g, rw_gn_b,
                  rw_v0, rw_v1, rw_v2, cmp_w1, cmp_b1, cmp_w2, cmp_b2, w_up_a, w_up_b, w_out, ln_g, ln_b):
    w = w_in[l]
    b0 = A_COLS
    q_kv = w[:, b0:b0 + D_B + 6 * KV_W]
    gl = w[:, b0 + D_B + 6 * KV_W:b0 + D_B + 6 * KV_W + 3 * H_B].reshape(D_MODEL, G_KV, 3 * R_Q)
    gl = jnp.pad(gl, ((0, 0), (0, 0), (0, GL_PAD - 3 * R_Q))).reshape(D_MODEL, G_KV * GL_PAD)
    zb = w[:, b0 + D_B + 6 * KV_W + 3 * H_B:b0 + D_B + 6 * KV_W + 3 * H_B + D_B]
    half = CMP_STRIDE * HD
    w1 = cmp_w1[l]
    w1r = jnp.concatenate([w1[:, :half].reshape(2, CMP_STRIDE, HD, CMP_HID),
                           w1[:, half:].reshape(2, CMP_STRIDE, HD, CMP_HID)], axis=-1)
    zero = jnp.zeros_like(w1r)
    w1r = jnp.stack([jnp.concatenate([w1r, zero], axis=2), jnp.concatenate([zero, w1r], axis=2)], axis=2)
    row = lambda a: a.reshape(1, -1).astype(F32)
    p = {
        'wa': w[:, :A_COLS].astype(BF16),
        'wb': jnp.concatenate([q_kv, gl, zb], axis=1).astype(BF16),
        'wg': w[:, b0 + D_B + 6 * KV_W + 3 * H_B + D_B:].astype(BF16),
        'wc': w[:, b0 + D_B:b0 + D_B + 2 * KV_W].astype(BF16),
        'mu': row(mu_shift[l]), 'w0': row(rw_w0[l]), 'w2': rw_w2[l].astype(BF16), 'a0': row(rw_a0[l]),
        'a2': rw_a2[l].astype(BF16), 'kk': row(rw_kk[l]), 'ka': row(rw_ka[l]), 'rk': row(rw_rk[l]),
        'gn_g': row(rw_gn_g[l]), 'gn_b': row(rw_gn_b[l]),
        'cmp': {'w1': w1r.astype(BF16), 'b1': cmp_b1[l].reshape(2, 1, CMP_HID).astype(F32),
                'w2': cmp_w2[l].astype(BF16), 'b2': cmp_b2[l].reshape(2, 1, HD).astype(F32)},
        'w_up_a': w_up_a[l].astype(BF16), 'w_up_b': w_up_b[l].astype(BF16), 'w_out': w_out[l].astype(BF16),
        'ln_g': row(ln_g[l]), 'ln_b': row(ln_b[l]),
    }
    if l > 0:
        p['v0'] = row(rw_v0[l - 1])
        p['v1'] = rw_v1[l - 1].astype(BF16)
        p['v2'] = rw_v2[l - 1].astype(BF16)
    return p


def _project(x2, p):
    m = x2.shape[0]
    tm = 512
    ha = _matmul(x2, p['wa'], tm, A_COLS // 3)
    hb = _matmul(x2, p['wb'], tm, 1024)
    hg = _matmul(x2, p['wg'], tm, 1024)
    return ha, hb, hg


def _finish(x2, o_a, o_b, hg, p):
    merged = _up_gate(o_a, o_b, p['w_up_a'], p['w_up_b'], hg)
    return _out_ln(merged, p['w_out'], x2, p['ln_g'], p['ln_b'])


def _prompt_layer(x2, vfirst, p, tabs, bsz, t):
    ha, hb, hg = _project(x2, p)
    o_a, vfirst, wkv = _rwkv_prompt(ha, vfirst, p, bsz, t)
    hc = _matmul(x2, p['wc'], 512, 2 * KV_W)
    kvp = _kv_prep(hb, hc, p['cmp'], bsz, t)
    o_b = _nsa_prompt(hb, kvp, tabs, bsz, t)
    y = _finish(x2, o_a, o_b, hg, p)
    kvc = D_B
    new_rows = hb[:, kvc:kvc + 4 * KV_W].reshape(bsz, t, 4, G_KV, HD)
    nwin = min(WIN, t)
    win_state = hb[:, kvc + 4 * KV_W:kvc + 6 * KV_W].reshape(bsz, t, 2, G_KV, HD)[:, t - nwin:]
    shift = ha.reshape(bsz, t, A_COLS)[:, t - 1]
    return y, vfirst, (new_rows, win_state, wkv, shift)


def _sample_layer(x2, vfirst, l, p, tabs, cache_l, cache_win_kv, state_wkv, state_shift, page_table):
    bsz = x2.shape[0]
    ha, hb, hg = _project(x2, p)
    o_a, vfirst, wkv = _rwkv_sample(ha, state_shift[l], state_wkv[l], vfirst, p)
    kvc = D_B
    q = hb[:, :D_B].reshape(bsz, G_KV, R_Q, HD) * SCALE
    eye = jnp.eye(G_KV, dtype=F32)
    qbd = (q[:, :, :, None, :] * eye[None, :, None, :, None]).reshape(bsz, H_B, KV_W).astype(BF16)
    new6 = hb[:, kvc:kvc + 6 * KV_W]
    new_rows8 = jnp.pad(new6[:, None, :], ((0, 0), (0, 7), (0, 0)))
    gl0 = kvc + 6 * KV_W
    gl = hb[:, gl0:gl0 + G_KV * GL_PAD].reshape(bsz, G_KV, GL_PAD)[:, :, :3 * R_Q].reshape(bsz, H_B, 3)
    zb = hb[:, gl0 + G_KV * GL_PAD:].reshape(bsz, H_B, HD)
    nbuf = cache_win_kv.shape[2]
    win = cache_win_kv.reshape(DEPTH, bsz, nbuf, 2 * KV_W)
    o_b = _nsa_sample(cache_l, l, page_table, win, qbd, new_rows8, gl, zb, p['cmp'], tabs)
    y = _finish(x2, o_a, o_b.reshape(bsz, D_B), hg, p)
    new_rows = new6[:, :4 * KV_W].reshape(bsz, 1, 4, G_KV, HD)
    new_win = new6[:, 4 * KV_W:].reshape(bsz, 1, 2, G_KV, HD)
    win_state = jnp.concatenate([cache_win_kv[l], new_win], axis=1)[:, -nbuf:]
    return y, vfirst, (new_rows, win_state, wkv, ha)


def kernel(x_prompt, x_sample, cache_kv, cache_win_kv, state_wkv, state_shift, page_table, w_in, mu_shift, rw_w0, rw_w2, rw_a0, rw_a2, rw_kk, rw_ka, rw_rk, rw_gn_g, rw_gn_b, rw_v0, rw_v1, rw_v2, cmp_w1, cmp_b1, cmp_w2, cmp_b2, rel_bias, w_up_a, w_up_b, w_out, ln_g, ln_b):
    bsz, t, _ = x_prompt.shape
    dec_b = x_sample.shape[0]
    n_pages = page_table.shape[1]
    depth, n_phys = cache_kv.shape[:2]
    cache_l = cache_kv.reshape(depth, n_phys, PAGE_SIZE * 4 * KV_W // LANES, LANES)
    tabs_p = _prompt_tables(rel_bias, t)
    tabs_s = _sample_tables(rel_bias, n_pages * PAGE_SIZE, cache_win_kv.shape[2])
    y_p = x_prompt.reshape(bsz * t, D_MODEL)
    y_s = x_sample.reshape(dec_b, D_MODEL)
    vf_p, vf_s = None, None
    st_p, st_s = [], []
    for l in range(depth):
        p = _layer_params(l, w_in, mu_shift, rw_w0, rw_w2, rw_a0, rw_a2, rw_kk, rw_ka, rw_rk, rw_gn_g, rw_gn_b,
                          rw_v0, rw_v1, rw_v2, cmp_w1, cmp_b1, cmp_w2, cmp_b2, w_up_a, w_up_b, w_out, ln_g, ln_b)
        y_p, vf_p, sp = _prompt_layer(y_p, vf_p, p, tabs_p, bsz, t)
        y_s, vf_s, ss = _sample_layer(y_s, vf_s, l, p, tabs_s, cache_l, cache_win_kv, state_wkv, state_shift,
                                      page_table)
        st_p.append(sp)
        st_s.append(ss)
    stack = lambda st, i: jnp.stack([s[i] for s in st])
    return (y_p.reshape(bsz, t, D_MODEL), y_s.reshape(dec_b, 1, D_MODEL),
            stack(st_p, 0), stack(st_p, 1), stack(st_p, 2), stack(st_p, 3),
            stack(st_s, 0), stack(st_s, 1), stack(st_s, 2), stack(st_s, 3))
```

```python
import functools
import math

import numpy as np
import jax
import jax.numpy as jnp
from jax import lax
from jax.experimental import pallas as pl
from jax.experimental.pallas import tpu as pltpu

D_MODEL = 2048
DEPTH = 2
PAGE_SIZE = 128
HS = 64
D_A = D_MODEL // 2
H_A = D_A // HS
R_W = 64
R_A = 64
R_V = 32
GN_EPS = 64e-5
HD = 64
D_B = D_MODEL // 2
H_B = D_B // HD
G_KV = 4
R_Q = H_B // G_KV
KV_W = G_KV * HD
CMP_LEN = 32
CMP_STRIDE = 16
CMP_HID = 128
SEL_LEN = 64
N_TOP = 16
WIN = 512
NUM_BUCKETS = 32
MAX_DIST = 128
SCALE = HD ** -0.5
A_COLS = 4 * D_A + R_W + R_A
ALPHA = (2 * DEPTH) ** 0.25
LN_EPS = 1e-5
NEG = -1e30

F32 = jnp.float32
BF16 = jnp.bfloat16

LANES = 128
VMEM_LIMIT = 56 * 1024 * 1024
CHUNK = 64
TQ = 128
KB = 128
GL_PAD = LANES
HB_COLS = D_B + 6 * KV_W + G_KV * GL_PAD + D_B

NT = (((1,), (1,)), ((), ()))
TN = (((0,), (0,)), ((), ()))


def _params(sem):
    return pltpu.CompilerParams(dimension_semantics=sem, vmem_limit_bytes=VMEM_LIMIT)


def _bdot(a, b, dims=None):
    a = a.astype(BF16)
    b = b.astype(BF16)
    if dims is None:
        return jnp.dot(a, b, preferred_element_type=F32)
    return lax.dot_general(a, b, dims, preferred_element_type=F32)


def _sigmoid(x):
    return 1.0 / (1.0 + jnp.exp(-x))


def _silu(x):
    return x * _sigmoid(x)


def _gelu_tanh(x):
    return 0.5 * x * (1.0 + jnp.tanh(math.sqrt(2.0 / math.pi) * (x + 0.044715 * (x * x * x))))


def _mm_kernel(x_ref, w_ref, o_ref):
    o_ref[...] = _bdot(x_ref[...], w_ref[...])


def _matmul(x, w, tm, tn):
    m, k = x.shape
    n = w.shape[1]
    tm = min(tm, m)
    return pl.pallas_call(
        _mm_kernel,
        grid=(m // tm, n // tn),
        in_specs=[pl.BlockSpec((tm, k), lambda i, j: (i, 0)),
                  pl.BlockSpec((k, tn), lambda i, j: (0, j))],
        out_specs=pl.BlockSpec((tm, tn), lambda i, j: (i, j)),
        out_shape=jax.ShapeDtypeStruct((m, n), F32),
        compiler_params=_params(("parallel", "parallel")),
    )(x, w)


def _up_kernel(oa_ref, ob_ref, wa_ref, wb_ref, ga_ref, gb_ref, o_ref):
    ua = _bdot(oa_ref[...], wa_ref[...])
    ub = _bdot(ob_ref[...], wb_ref[...])
    o_ref[...] = (_sigmoid(ga_ref[...]) * ua + _sigmoid(gb_ref[...]) * ub).astype(o_ref.dtype)


def _up_gate(o_a, o_b, w_up_a, w_up_b, hg, tm=512, tn=1024):
    m = o_a.shape[0]
    tm = min(tm, m)
    nb = D_MODEL // tn
    return pl.pallas_call(
        _up_kernel,
        grid=(m // tm, nb),
        in_specs=[pl.BlockSpec((tm, D_A), lambda i, j: (i, 0)),
                  pl.BlockSpec((tm, D_B), lambda i, j: (i, 0)),
                  pl.BlockSpec((D_A, tn), lambda i, j: (0, j)),
                  pl.BlockSpec((D_B, tn), lambda i, j: (0, j)),
                  pl.BlockSpec((tm, tn), lambda i, j: (i, j)),
                  pl.BlockSpec((tm, tn), lambda i, j: (i, j + nb))],
        out_specs=pl.BlockSpec((tm, tn), lambda i, j: (i, j)),
        out_shape=jax.ShapeDtypeStruct((m, D_MODEL), BF16),
        compiler_params=_params(("parallel", "parallel")),
    )(o_a, o_b, w_up_a, w_up_b, hg, hg)


def _out_ln_kernel(m_ref, w_ref, x_ref, g_ref, b_ref, o_ref):
    u = ALPHA * x_ref[...] + _bdot(m_ref[...], w_ref[...])
    mu = jnp.mean(u, axis=-1, keepdims=True)
    d = u - mu
    var = jnp.mean(d * d, axis=-1, keepdims=True)
    o_ref[...] = d * lax.rsqrt(var + LN_EPS) * g_ref[...] + b_ref[...]


def _out_ln(merged, w_out, x, ln_g, ln_b, tm=256):
    m = x.shape[0]
    tm = min(tm, m)
    return pl.pallas_call(
        _out_ln_kernel,
        grid=(m // tm,),
        in_specs=[pl.BlockSpec((tm, D_MODEL), lambda i: (i, 0)),
                  pl.BlockSpec((D_MODEL, D_MODEL), lambda i: (0, 0)),
                  pl.BlockSpec((tm, D_MODEL), lambda i: (i, 0)),
                  pl.BlockSpec((1, D_MODEL), lambda i: (0, 0)),
                  pl.BlockSpec((1, D_MODEL), lambda i: (0, 0))],
        out_specs=pl.BlockSpec((tm, D_MODEL), lambda i: (i, 0)),
        out_shape=jax.ShapeDtypeStruct((m, D_MODEL), F32),
        compiler_params=_params(("parallel",)),
    )(merged, w_out, x, ln_g, ln_b)


def _rwkv_premix(xm, vfirst, w0, w2, a0, a2, kkp, ka, vgate):
    r = xm[:, 0:D_A]
    k = xm[:, D_A:2 * D_A]
    v = xm[:, 2 * D_A:3 * D_A]
    w_lo = xm[:, 3 * D_A:3 * D_A + R_W]
    a_lo = xm[:, 3 * D_A + R_W:3 * D_A + R_W + R_A]
    z = xm[:, 3 * D_A + R_W + R_A:A_COLS]
    t = w0 + _bdot(jnp.tanh(w_lo), w2)
    lw = -math.exp(-0.5) * _sigmoid(t)
    if vgate is not None:
        v0, v1, v2 = vgate
        vg = _sigmoid(v0 + _bdot(_bdot(v, v1), v2))
        v = v + (vfirst - v) * vg
    lr = _sigmoid(a0 + _bdot(a_lo, a2))
    kkr = k * kkp
    k2 = k * (1.0 + (lr - 1.0) * ka)
    return r, lw, k2, v, lr, kkr, z


def _head_post(y, r, k2, v, z, rk, gn_g, gn_b):
    mu = jnp.mean(y, axis=-1, keepdims=True)
    d = y - mu
    var = jnp.mean(d * d, axis=-1, keepdims=True)
    yn = d * lax.rsqrt(var + GN_EPS) * gn_g + gn_b
    bonus = jnp.sum(r * k2 * rk, axis=-1, keepdims=True) * v
    return (yn + bonus) * _silu(z)


def _rwkv_prompt_kernel(*refs, has_vgate):
    if has_vgate:
        (fa_ref, vf_ref, mu_ref, w0_ref, w2_ref, a0_ref, a2_ref, kkp_ref, ka_ref, rk_ref, gg_ref, gb_ref,
         v0_ref, v1_ref, v2_ref, o_ref, s_out_ref, s_ref, last_ref) = refs
    else:
        (fa_ref, mu_ref, w0_ref, w2_ref, a0_ref, a2_ref, kkp_ref, ka_ref, rk_ref, gg_ref, gb_ref,
         o_ref, vf_out_ref, s_out_ref, s_ref, last_ref) = refs
    c = pl.program_id(1)
    nc = pl.num_programs(1)
    C = CHUNK

    @pl.when(c == 0)
    def _():
        s_ref[...] = jnp.zeros_like(s_ref)
        last_ref[...] = jnp.zeros_like(last_ref)

    x = fa_ref[...]
    row = lax.broadcasted_iota(jnp.int32, (C, 1), 0)
    prev = jnp.where(row == 0, last_ref[0:1, :], pltpu.roll(x, 1, axis=0))
    last_ref[0:1, :] = x[C - 1:C, :]
    xm = x + (prev - x) * mu_ref[...]
    if has_vgate:
        vgate = (v0_ref[...], v1_ref[...], v2_ref[...])
        vfirst = vf_ref[...]
    else:
        vgate, vfirst = None, None
    r, lw, k2, v, lr, kkr, z = _rwkv_premix(xm, vfirst, w0_ref[...], w2_ref[...], a0_ref[...], a2_ref[...],
                                            kkp_ref[...], ka_ref[...], vgate)
    if not has_vgate:
        vf_out_ref[...] = v

    ri = lax.broadcasted_iota(jnp.int32, (C, C), 0)
    ci = lax.broadcasted_iota(jnp.int32, (C, C), 1)
    tri_i = ri >= ci
    tri_s = ri > ci
    tri_b = jnp.where(tri_i, 1.0, 0.0).astype(BF16)
    eye = jnp.where(ri == ci, 1.0, 0.0).astype(F32)
    lw_hi = lw.astype(BF16)
    lw_lo = (lw - lw_hi.astype(F32)).astype(BF16)
    L = jnp.dot(tri_b, lw_hi, preferred_element_type=F32) + jnp.dot(tri_b, lw_lo, preferred_element_type=F32)
    LC = L[C - 1:C, :]
    e_in = jnp.exp(L)
    e_ex = jnp.exp(L - lw)
    e_neg = jnp.exp(-L)
    e_rem = jnp.exp(LC - L)
    pc = jnp.exp(LC)
    rk = rk_ref[...]
    gg = gg_ref[...]
    gb = gb_ref[...]

    heads = range(H_A)
    sls = [slice(h * HS, (h + 1) * HS) for h in heads]
    bf = lambda xs: [x.astype(BF16) for x in xs]
    kk = []
    for sl in sls:
        u = kkr[:, sl]
        kk.append(u * lax.rsqrt(jnp.maximum(jnp.sum(u * u, axis=-1, keepdims=True), 1e-24)))
    r_h = [r[:, sl] for sl in sls]
    k_h = [k2[:, sl] for sl in sls]
    v_h = [v[:, sl] for sl in sls]
    v_b = bf(v_h)
    b_h = [kk[h] * lr[:, sls[h]] for h in heads]
    at = [(-kk[h]) * e_ex[:, sls[h]] for h in heads]
    rt = [r_h[h] * e_in[:, sls[h]] for h in heads]
    bt = [b_h[h] * e_neg[:, sls[h]] for h in heads]
    kt = [k_h[h] * e_neg[:, sls[h]] for h in heads]
    bh = bf([b_h[h] * e_rem[:, sls[h]] for h in heads])
    kh = bf([k_h[h] * e_rem[:, sls[h]] for h in heads])
    g = [_bdot(jnp.concatenate([at[h], rt[h]], axis=0), jnp.concatenate([bt[h], kt[h]], axis=0), NT)
         for h in heads]
    a_ab = bf([jnp.where(tri_s, x[:C, :C], 0.0) for x in g])
    a_ak = bf([jnp.where(tri_s, x[:C, C:], 0.0) for x in g])
    a_rb = bf([jnp.where(tri_i, x[C:, :C], 0.0) for x in g])
    a_rk = bf([jnp.where(tri_i, x[C:, C:], 0.0) for x in g])
    tm = [eye + x.astype(F32) for x in a_ab]
    ap = bf([_bdot(x, x) for x in a_ab])
    n = 2
    while n < C:
        tm_next = [tm[h] + _bdot(tm[h], ap[h]) for h in heads]
        if 2 * n < C:
            ap = bf([_bdot(x, x) for x in ap])
        tm = tm_next
        n *= 2
    akv = [_bdot(a_ak[h], v_b[h]) for h in heads]
    wu = bf([_bdot(tm[h], jnp.concatenate([at[h], akv[h]], axis=1)) for h in heads])
    arw = [_bdot(a_rb[h], wu[h]) for h in heads]
    yh = [arw[h][:, HS:] + _bdot(a_rk[h], v_b[h]) for h in heads]
    s_old = [s_ref[h] for h in heads]
    s_b = bf(s_old)
    y = [_bdot(rt[h] + arw[h][:, :HS], s_b[h], NT) + yh[h] for h in heads]
    bw = [_bdot(bh[h], wu[h][:, :HS], TN) for h in heads]
    nt = [_bdot(jnp.concatenate([wu[h][:, HS:], v_b[h]], axis=0), jnp.concatenate([bh[h], kh[h]], axis=0), TN)
          for h in heads]
    for h in heads:
        s_ref[h] = s_old[h] * pc[:, sls[h]] + _bdot(s_b[h], bw[h], NT) + nt[h]
    for h in heads:
        sl = sls[h]
        o_ref[:, sl] = _head_post(y[h], r_h[h], k_h[h], v_h[h], z[:, sl], rk[:, sl], gg[:, sl],
                                  gb[:, sl]).astype(o_ref.dtype)

    @pl.when(c == nc - 1)
    def _():
        s_out_ref[0] = s_ref[...]


def _rwkv_prompt(ha, vfirst, p, bsz, t):
    has_vgate = vfirst is not None
    nc = t // CHUNK
    row_spec = lambda w: pl.BlockSpec((CHUNK, w), lambda b, c: (b * nc + c, 0))
    full = lambda a: pl.BlockSpec(a.shape, lambda b, c: (0,) * a.ndim)
    ins = [ha]
    in_specs = [row_spec(A_COLS)]
    if has_vgate:
        ins.append(vfirst)
        in_specs.append(row_spec(D_A))
    names = ['mu', 'w0', 'w2', 'a0', 'a2', 'kk', 'ka', 'rk', 'gn_g', 'gn_b'] + (['v0', 'v1', 'v2'] if has_vgate else [])
    for nme in names:
        ins.append(p[nme])
        in_specs.append(full(p[nme]))
    out_shape = [jax.ShapeDtypeStruct((bsz * t, D_A), BF16)]
    out_specs = [row_spec(D_A)]
    if not has_vgate:
        out_shape.append(jax.ShapeDtypeStruct((bsz * t, D_A), F32))
        out_specs.append(row_spec(D_A))
    out_shape.append(jax.ShapeDtypeStruct((bsz, H_A, HS, HS), F32))
    out_specs.append(pl.BlockSpec((1, H_A, HS, HS), lambda b, c: (b, 0, 0, 0)))
    outs = pl.pallas_call(
        functools.partial(_rwkv_prompt_kernel, has_vgate=has_vgate),
        grid=(bsz, nc),
        in_specs=in_specs,
        out_specs=out_specs,
        out_shape=out_shape,
        scratch_shapes=[pltpu.VMEM((H_A, HS, HS), F32), pltpu.VMEM((8, A_COLS), F32)],
        compiler_params=_params(("parallel", "arbitrary")),
    )(*ins)
    if has_vgate:
        return outs[0], vfirst, outs[1]
    return outs[0], outs[1], outs[2]


def _rwkv_sample_kernel(*refs, has_vgate, bt):
    if has_vgate:
        (fa_ref, prev_ref, s_in_ref, vf_ref, mu_ref, w0_ref, w2_ref, a0_ref, a2_ref, kkp_ref, ka_ref, rk_ref,
         gg_ref, gb_ref, v0_ref, v1_ref, v2_ref, o_ref, s_out_ref, ops_ref, y_ref) = refs
    else:
        (fa_ref, prev_ref, s_in_ref, mu_ref, w0_ref, w2_ref, a0_ref, a2_ref, kkp_ref, ka_ref, rk_ref,
         gg_ref, gb_ref, o_ref, vf_out_ref, s_out_ref, ops_ref, y_ref) = refs
    x = fa_ref[...]
    xm = x + (prev_ref[...] - x) * mu_ref[...]
    if has_vgate:
        vgate = (v0_ref[...], v1_ref[...], v2_ref[...])
        vfirst = vf_ref[...]
    else:
        vgate, vfirst = None, None
    r, lw, k2, v, lr, kkr, z = _rwkv_premix(xm, vfirst, w0_ref[...], w2_ref[...], a0_ref[...], a2_ref[...],
                                            kkp_ref[...], ka_ref[...], vgate)
    if not has_vgate:
        vf_out_ref[...] = v
    w = jnp.exp(lw)
    for h in range(H_A):
        sl = slice(h * HS, (h + 1) * HS)
        kk = kkr[:, sl]
        kk = kk * lax.rsqrt(jnp.maximum(jnp.sum(kk * kk, axis=-1, keepdims=True), 1e-24))
        ops_ref[0, :, sl] = -kk
        ops_ref[1, :, sl] = kk * lr[:, sl]
    ops_ref[2] = w
    ops_ref[3] = k2
    ops_ref[4] = v
    ops_ref[5] = r
    ri = lax.broadcasted_iota(jnp.int32, (HS, HS), 0)
    ci = lax.broadcasted_iota(jnp.int32, (HS, HS), 1)
    eye = jnp.where(ri == ci, 1.0, 0.0).astype(F32)

    for b in range(bt):
        for h in range(H_A):
            sl = slice(h * HS, (h + 1) * HS)
            a_row = ops_ref[0, b:b + 1, sl]
            b_row = ops_ref[1, b:b + 1, sl]
            w_row = ops_ref[2, b:b + 1, sl]
            k_row = ops_ref[3, b:b + 1, sl]
            v_row = ops_ref[4, b:b + 1, sl]
            r_row = ops_ref[5, b:b + 1, sl]
            s = s_in_ref[b, h]
            sa = jnp.sum(s * a_row, axis=-1, keepdims=True)
            v_col = jnp.sum(eye * v_row, axis=-1, keepdims=True)
            s_new = s * w_row + sa * b_row + v_col * k_row
            s_out_ref[b, h] = s_new
            y_col = jnp.sum(s_new * r_row, axis=-1, keepdims=True)
            y_ref[b:b + 1, sl] = jnp.sum(eye * y_col, axis=0, keepdims=True)
    y = y_ref[...]
    rk = rk_ref[...]
    gg = gg_ref[...]
    gb = gb_ref[...]
    for h in range(H_A):
        sl = slice(h * HS, (h + 1) * HS)
        o_ref[:, sl] = _head_post(y[:, sl], r[:, sl], k2[:, sl], v[:, sl], z[:, sl], rk[:, sl], gg[:, sl],
                                  gb[:, sl]).astype(o_ref.dtype)


def _rwkv_sample(ha, prev, s_in, vfirst, p, bt=8):
    bsz = ha.shape[0]
    has_vgate = vfirst is not None
    row_spec = lambda w: pl.BlockSpec((bt, w), lambda i: (i, 0))
    full = lambda a: pl.BlockSpec(a.shape, lambda i: (0,) * a.ndim)
    st_spec = pl.BlockSpec((bt, H_A, HS, HS), lambda i: (i, 0, 0, 0))
    ins = [ha, prev, s_in]
    in_specs = [row_spec(A_COLS), row_spec(A_COLS), st_spec]
    if has_vgate:
        ins.append(vfirst)
        in_specs.append(row_spec(D_A))
    names = ['mu', 'w0', 'w2', 'a0', 'a2', 'kk', 'ka', 'rk', 'gn_g', 'gn_b'] + (['v0', 'v1', 'v2'] if has_vgate else [])
    for nme in names:
        ins.append(p[nme])
        in_specs.append(full(p[nme]))
    out_shape = [jax.ShapeDtypeStruct((bsz, D_A), BF16)]
    out_specs = [row_spec(D_A)]
    if not has_vgate:
        out_shape.append(jax.ShapeDtypeStruct((bsz, D_A), F32))
        out_specs.append(row_spec(D_A))
    out_shape.append(jax.ShapeDtypeStruct((bsz, H_A, HS, HS), F32))
    out_specs.append(st_spec)
    outs = pl.pallas_call(
        functools.partial(_rwkv_sample_kernel, has_vgate=has_vgate, bt=bt),
        grid=(bsz // bt,),
        in_specs=in_specs,
        out_specs=out_specs,
        out_shape=out_shape,
        scratch_shapes=[pltpu.VMEM((6, bt, D_A), F32), pltpu.VMEM((bt, D_A), F32)],
        compiler_params=_params(("parallel",)),
    )(*ins)
    if has_vgate:
        return outs[0], vfirst, outs[1]
    return outs[0], outs[1], outs[2]


def _compress_rows(load_rows, w1_ref, b1, w2, b2, kv, nch):
    accs = [jnp.zeros((nch, 2 * CMP_HID), F32) for _ in range(G_KV)]
    for tau in range(CMP_STRIDE):
        for pair in range(G_KV // 2):
            rows = load_rows(tau, pair).astype(BF16)
            for parity in range(2):
                g = 2 * pair + parity
                accs[g] = accs[g] + jnp.dot(rows, w1_ref[kv, tau, parity], preferred_element_type=F32)
    outs = []
    for acc in accs:
        h = acc[:, :CMP_HID] + pltpu.roll(acc[:, CMP_HID:], nch - 1, axis=0) + b1
        outs.append(_bdot(_gelu_tanh(h), w2) + b2)
    return outs


def _kv_prep_kernel(kc_ref, ks_ref, kw_ref, w1_ref, b1_ref, w2_ref, b2_ref,
                    ks_o, vs_o, kw_o, vw_o, kc_o, vc_o, *, nch):
    npair = KV_W // LANES

    def put_transposed(out, pair, x):
        xt = x.T
        out[0, 2 * pair] = xt[0:HD].astype(BF16)
        out[0, 2 * pair + 1] = xt[HD:2 * HD].astype(BF16)

    for g in range(G_KV):
        sl = slice(g * HD, (g + 1) * HD)
        ks_o[0, g] = ks_ref[:, sl].astype(BF16)
        kw_o[0, g] = kw_ref[:, sl].astype(BF16)
    for pair in range(npair):
        sl2 = slice(KV_W + pair * LANES, KV_W + (pair + 1) * LANES)
        put_transposed(vs_o, pair, ks_ref[:, sl2])
        put_transposed(vw_o, pair, kw_ref[:, sl2])
    for kv in (0, 1):
        load = lambda tau, pair, kv=kv: kc_ref[pl.ds(2 * npair * tau + kv * npair + pair, nch,
                                                     stride=2 * npair * CMP_STRIDE), :]
        res = _compress_rows(load, w1_ref, b1_ref[kv], w2_ref[kv], b2_ref[kv], kv, nch)
        if kv == 0:
            for g in range(G_KV):
                kc_o[0, g] = res[g].astype(BF16)
        else:
            for pair in range(npair):
                put_transposed(vc_o, pair, jnp.concatenate([res[2 * pair], res[2 * pair + 1]], axis=1))


def _kv_prep(hb, hc, cp, bsz, t):
    nch = t // CMP_STRIDE
    blk = lambda j: pl.BlockSpec((t, 2 * KV_W), lambda b: (b, j))
    full = lambda a: pl.BlockSpec(a.shape, lambda b: (0,) * a.ndim)
    c0 = D_B // (2 * KV_W)
    lane_rows = 2 * KV_W // LANES
    hc = hc.reshape(bsz * t * lane_rows, LANES)
    def arr(n, transposed):
        shp = (G_KV, HD, n) if transposed else (G_KV, n, HD)
        return jax.ShapeDtypeStruct((bsz,) + shp, BF16), pl.BlockSpec((1,) + shp, lambda b: (b, 0, 0, 0))

    outs = [arr(t, False), arr(t, True), arr(t, False), arr(t, True), arr(nch, False), arr(nch, True)]
    return pl.pallas_call(
        functools.partial(_kv_prep_kernel, nch=nch),
        grid=(bsz,),
        in_specs=[pl.BlockSpec((t * lane_rows, LANES), lambda b: (b, 0)), blk(c0 + 1), blk(c0 + 2),
                  full(cp['w1']), full(cp['b1']), full(cp['w2']), full(cp['b2'])],
        out_specs=[o[1] for o in outs],
        out_shape=[o[0] for o in outs],
        compiler_params=_params(("parallel",)),
    )(hc, hb, hb, cp['w1'], cp['b1'], cp['w2'], cp['b2'])


def _softmax_block(qst, k, vt, bias, mask, m_ref, l_ref, acc_ref):
    s = jnp.dot(k, qst, preferred_element_type=F32) + bias
    if mask is not None:
        s = jnp.where(mask, s, NEG)
    m_prev = m_ref[...]
    m_new = jnp.maximum(m_prev, jnp.max(s, axis=0, keepdims=True))
    p = jnp.exp(s - m_new)
    if mask is not None:
        p = jnp.where(mask, p, 0.0)
    alpha = jnp.exp(m_prev - m_new)
    l_ref[...] = alpha * l_ref[...] + jnp.sum(p, axis=0, keepdims=True)
    acc_ref[...] = alpha * acc_ref[...] + jnp.dot(vt, p.astype(BF16), preferred_element_type=F32)
    m_ref[...] = m_new


def _nsa_prompt_kernel(q_ref, gl_ref, zb_ref, kc_ref, vc_ref, ks_ref, vs_ref, kw_ref, vw_ref,
                       bc_ref, d0_ref, d1_ref, far_ref, ov_ref, ex_ref, o_ref,
                       m_ref, l_ref, acc_ref, sel_ref, *, t, nch):
    g = pl.program_id(1)
    qt = pl.program_id(2)
    nsel = t // SEL_LEN
    qt_t = (q_ref[...] * SCALE).T
    qst = jnp.concatenate([qt_t[r * HD:(r + 1) * HD, :] for r in range(R_Q)], axis=1).astype(BF16)
    qpos = qt * TQ + lax.broadcasted_iota(jnp.int32, (1, TQ), 1)
    rep = lambda x: jnp.concatenate([x] * R_Q, axis=1)

    def reset():
        m_ref[...] = jnp.full_like(m_ref, NEG)
        l_ref[...] = jnp.zeros_like(l_ref)
        acc_ref[...] = jnp.zeros_like(acc_ref)

    def result():
        l = l_ref[...]
        return acc_ref[...] / jnp.where(l > 0.0, l, 1.0)

    nrow = lax.broadcasted_iota(jnp.int32, (nch, TQ), 0)
    mask_c = (qpos >= nrow * CMP_STRIDE + (CMP_LEN - 1)) & (nrow < nch - 1)
    mask_c4 = rep(mask_c)
    bias_c = jnp.concatenate([bc_ref[r] for r in range(R_Q)], axis=1)
    s = jnp.dot(kc_ref[0, 0], qst, preferred_element_type=F32) + bias_c
    s = jnp.where(mask_c4, s, NEG)
    p = jnp.where(mask_c4, jnp.exp(s - jnp.max(s, axis=0, keepdims=True)), 0.0)
    l = jnp.sum(p, axis=0, keepdims=True)
    p = p / jnp.where(l > 0.0, l, 1.0)
    o_c = jnp.dot(vc_ref[0, 0], p.astype(BF16), preferred_element_type=F32)
    psum = p[:, 0:TQ]
    for r in range(1, R_Q):
        psum = psum + p[:, r * TQ:(r + 1) * TQ]
    p_hi = psum.astype(BF16)
    p_lo = (psum - p_hi.astype(F32)).astype(BF16)
    ov = ov_ref[...]
    imp = jnp.dot(ov, p_hi, preferred_element_type=F32) + jnp.dot(ov, p_lo, preferred_element_type=F32)
    blk = lax.broadcasted_iota(jnp.int32, (nsel, TQ), 0)
    cur = qpos // SEL_LEN
    imp = jnp.where(blk * SEL_LEN <= qpos, imp, NEG)
    imp = jnp.where((blk == 0) | (blk == cur) | (blk == cur - 1), -NEG, imp)
    rank = jnp.zeros((nsel, TQ), F32)
    for s2 in range(nsel):
        other = imp[s2:s2 + 1, :]
        rank = rank + jnp.where((other > imp) | ((other == imp) & (blk > s2)), 1.0, 0.0)
    sel = jnp.where(rank < float(min(N_TOP, nsel)), 1.0, 0.0).astype(BF16)
    sel_ref[...] = jnp.dot(ex_ref[...], sel, preferred_element_type=F32)

    ik = lax.broadcasted_iota(jnp.int32, (KB, TQ), 0)
    iq = lax.broadcasted_iota(jnp.int32, (KB, TQ), 1)
    causal4 = rep(ik <= iq)
    upper4 = rep(ik > iq)
    bias_d0 = jnp.concatenate([d0_ref[r] for r in range(R_Q)], axis=1)
    bias_d1 = jnp.concatenate([d1_ref[r] for r in range(R_Q)], axis=1)
    bias_far = jnp.concatenate([jnp.full((KB, TQ), far_ref[g * R_Q + r], F32) for r in range(R_Q)], axis=1)

    def sel_mask(kb):
        return rep(sel_ref[pl.ds(pl.multiple_of(kb * KB, KB), KB), :] > 0.5)

    def kv(kref, vref, kb):
        off = pl.multiple_of(kb * KB, KB)
        return kref[0, 0, pl.ds(off, KB), :], vref[0, 0, :, pl.ds(off, KB)]

    reset()

    def far_sel(kb, carry):
        k, v = kv(ks_ref, vs_ref, kb)
        _softmax_block(qst, k, v,bias_far, sel_mask(kb), m_ref, l_ref, acc_ref)
        return carry

    lax.fori_loop(0, jnp.maximum(qt - 1, 0), far_sel, 0)

    @pl.when(qt >= 1)
    def _():
        k, v = kv(ks_ref, vs_ref, qt - 1)
        _softmax_block(qst, k, v,bias_d1, sel_mask(qt - 1), m_ref, l_ref, acc_ref)

    k, v = kv(ks_ref, vs_ref, qt)
    _softmax_block(qst, k, v,bias_d0, sel_mask(qt) & causal4, m_ref, l_ref, acc_ref)
    o_s = result()

    reset()
    nwin = WIN // KB

    @pl.when(qt >= nwin)
    def _():
        k, v = kv(kw_ref, vw_ref, qt - nwin)
        _softmax_block(qst, k, v,bias_far, upper4, m_ref, l_ref, acc_ref)

    def far_win(kb, carry):
        k, v = kv(kw_ref, vw_ref, kb)
        _softmax_block(qst, k, v,bias_far, None, m_ref, l_ref, acc_ref)
        return carry

    lax.fori_loop(jnp.maximum(qt - nwin + 1, 0), jnp.maximum(qt - 1, 0), far_win, 0)

    @pl.when(qt >= 1)
    def _():
        k, v = kv(kw_ref, vw_ref, qt - 1)
        _softmax_block(qst, k, v,bias_d1, None, m_ref, l_ref, acc_ref)

    k, v = kv(kw_ref, vw_ref, qt)
    _softmax_block(qst, k, v,bias_d0, causal4, m_ref, l_ref, acc_ref)
    o_w = result()

    gate = _sigmoid(gl_ref[...]).T
    outs = []
    for r in range(R_Q):
        cs = slice(r * TQ, (r + 1) * TQ)
        outs.append(gate[3 * r:3 * r + 1, :] * o_c[:, cs] + gate[3 * r + 1:3 * r + 2, :] * o_s[:, cs]
                    + gate[3 * r + 2:3 * r + 3, :] * o_w[:, cs])
    o = jnp.concatenate(outs, axis=0).T
    o_ref[...] = (o * _silu(zb_ref[...])).astype(o_ref.dtype)


def _nsa_prompt(hb, kvp, tabs, bsz, t):
    ks_t, vs_t, kw_t, vw_t, kc_t, vc_t = kvp
    nch = t // CMP_STRIDE
    nqt = t // TQ
    gw = R_Q * HD
    k_spec = lambda n: pl.BlockSpec((1, 1, n, HD), lambda b, g, i: (b, g, 0, 0))
    vt_spec = lambda n: pl.BlockSpec((1, 1, HD, n), lambda b, g, i: (b, g, 0, 0))
    gl0 = (D_B + 6 * KV_W) // GL_PAD
    zb0 = (D_B + 6 * KV_W + G_KV * GL_PAD) // gw
    return pl.pallas_call(
        functools.partial(_nsa_prompt_kernel, t=t, nch=nch),
        grid=(bsz, G_KV, nqt),
        in_specs=[pl.BlockSpec((TQ, gw), lambda b, g, i: (b * nqt + i, g)),
                  pl.BlockSpec((TQ, GL_PAD), lambda b, g, i: (b * nqt + i, gl0 + g)),
                  pl.BlockSpec((TQ, gw), lambda b, g, i: (b * nqt + i, zb0 + g)),
                  k_spec(nch), vt_spec(nch), k_spec(t), vt_spec(t), k_spec(t), vt_spec(t),
                  pl.BlockSpec((R_Q, nch, TQ), lambda b, g, i: (g, 0, i)),
                  pl.BlockSpec((R_Q, KB, TQ), lambda b, g, i: (g, 0, 0)),
                  pl.BlockSpec((R_Q, KB, TQ), lambda b, g, i: (g, 0, 0)),
                  pl.BlockSpec(memory_space=pltpu.SMEM),
                  pl.BlockSpec(tabs['overlap'].shape, lambda b, g, i: (0, 0)),
                  pl.BlockSpec(tabs['expand'].shape, lambda b, g, i: (0, 0))],
        out_specs=pl.BlockSpec((TQ, gw), lambda b, g, i: (b * nqt + i, g)),
        out_shape=jax.ShapeDtypeStruct((bsz * t, D_B), BF16),
        scratch_shapes=[pltpu.VMEM((1, R_Q * TQ), F32), pltpu.VMEM((1, R_Q * TQ), F32),
                        pltpu.VMEM((HD, R_Q * TQ), F32), pltpu.VMEM((t, TQ), F32)],
        compiler_params=_params(("parallel", "parallel", "arbitrary")),
    )(hb, hb, hb, kc_t, vc_t, ks_t, vs_t, kw_t, vw_t, tabs['bias_c'], tabs['d0'], tabs['d1'], tabs['far'],
      tabs['overlap'], tabs['expand'])


def _row_softmax(s, mask, s_new):
    sm = jnp.where(mask, s, NEG)
    m = jnp.maximum(jnp.max(sm, axis=-1, keepdims=True), s_new)
    p = jnp.where(mask, jnp.exp(sm - m), 0.0)
    p_new = jnp.exp(s_new - m)
    l = jnp.sum(p, axis=-1, keepdims=True) + p_new
    return p / l, p_new / l


def _nsa_sample_kernel(pt_ref, *refs, n_pages, past):
    page_refs = refs[:n_pages]
    (win_ref, qbd_ref, new_ref, gl_ref, zb_ref, w1_ref, b1_ref, w2_ref, b2_ref,
     bc_ref, bs_ref, bw_ref, b0_ref, ov_ref, gsum_ref, bdm_ref, ex_ref, o_ref, x_s, kc_s, vc_s) = refs[n_pages:]
    del pt_ref
    nch = past // CMP_STRIDE
    nsel = past // SEL_LEN + 1
    qbd = qbd_ref[0]
    qbd_f = qbd.astype(F32)
    npair = KV_W // LANES

    for pi, pr in enumerate(page_refs):
        for kv in range(2):
            for pair in range(npair):
                x_s[kv, pair, pi * PAGE_SIZE:(pi + 1) * PAGE_SIZE, :] = pr[0, 0, kv, pair * LANES:(pair + 1) * LANES, :].T
    for kv, dst in ((0, kc_s), (1, vc_s)):
        load = lambda tau, pair, kv=kv: x_s[kv, pair, pl.ds(tau, nch, stride=CMP_STRIDE), :]
        res = _compress_rows(load, w1_ref, b1_ref[kv], w2_ref[kv], b2_ref[kv], kv, nch)
        for g in range(G_KV):
            dst[:, g * HD:(g + 1) * HD] = res[g]
    ncol = lax.broadcasted_iota(jnp.int32, (H_B, nch), 1)
    s_c = _bdot(qbd, kc_s[...], NT) + bc_ref[...]
    mask_c = ncol < nch - 1
    sm = jnp.where(mask_c, s_c, NEG)
    p_c = jnp.where(mask_c, jnp.exp(sm - jnp.max(sm, axis=-1, keepdims=True)), 0.0)
    p_c = p_c / jnp.sum(p_c, axis=-1, keepdims=True)
    o_c = _bdot(p_c, vc_s[...])
    p_hi = p_c.astype(BF16)
    p_lo = (p_c - p_hi.astype(F32)).astype(BF16)
    ov = ov_ref[...]
    imp = jnp.dot(p_hi, ov, preferred_element_type=F32) + jnp.dot(p_lo, ov, preferred_element_type=F32)
    i_hi = imp.astype(BF16)
    i_lo = (imp - i_hi.astype(F32)).astype(BF16)
    gs = gsum_ref[...]
    imp = jnp.dot(gs, i_hi, preferred_element_type=F32) + jnp.dot(gs, i_lo, preferred_element_type=F32)
    nsp = imp.shape[1]
    blk = lax.broadcasted_iota(jnp.int32, (H_B, nsp), 1)
    cur = past // SEL_LEN
    imp = jnp.where((blk == 0) | (blk == cur) | (blk == cur - 1), -NEG, imp)
    imp = jnp.where(blk < nsel, imp, 2.0 * NEG)
    rank = jnp.zeros((H_B, nsp), F32)
    for s2 in range(nsel):
        other = imp[:, s2:s2 + 1]
        rank = rank + jnp.where((other > imp) | ((other == imp) & (blk > s2)), 1.0, 0.0)
    sel = jnp.where(rank < float(N_TOP), 1.0, 0.0).astype(BF16)
    mask_s = jnp.dot(sel, ex_ref[...], preferred_element_type=F32) > 0.5

    new = new_ref[0]
    ks_new = new[:, 2 * KV_W:3 * KV_W]
    vs_new = new[:, 3 * KV_W:4 * KV_W]
    kw_new = new[:, 4 * KV_W:5 * KV_W]
    vw_new = new[:, 5 * KV_W:6 * KV_W]
    b0 = b0_ref[...]
    s_s = jnp.concatenate([_bdot(qbd, pr[0, 0, 2]) for pr in page_refs], axis=1) + bs_ref[...]
    s_new = jnp.sum(qbd_f * ks_new, axis=-1, keepdims=True) + b0
    p_s, p_new = _row_softmax(s_s, mask_s, s_new)
    o_s = p_new * vs_new
    for pi, pr in enumerate(page_refs):
        o_s = o_s + _bdot(p_s[:, pi * PAGE_SIZE:(pi + 1) * PAGE_SIZE], pr[0, 0, 3], NT)
    nbuf = win_ref.shape[-1]
    wcol = lax.broadcasted_iota(jnp.int32, (H_B, nbuf), 1)
    s_w = _bdot(qbd, win_ref[0, 0, 0]) + bw_ref[...]
    s_wn = jnp.sum(qbd_f * kw_new, axis=-1, keepdims=True) + b0
    p_w, p_wn = _row_softmax(s_w, wcol >= nbuf + 1 - WIN, s_wn)
    o_w = _bdot(p_w, win_ref[0, 0, 1], NT) + p_wn * vw_new
    gate = _sigmoid(gl_ref[0])
    o = gate[:, 0:1] * o_c + gate[:, 1:2] * o_s + gate[:, 2:3] * o_w
    o = o * bdm_ref[...]
    o16 = o[:, 0:HD]
    for g in range(1, G_KV):
        o16 = o16 + o[:, g * HD:(g + 1) * HD]
    o_ref[0] = (o16 * _silu(zb_ref[0])).astype(o_ref.dtype)


def _nsa_sample(cache_l, l, page_table, win, qbd, new_rows, gl, zb, cp, tabs):
    bsz, n_pages = page_table.shape
    past = n_pages * PAGE_SIZE
    nch = past // CMP_STRIDE
    nbuf = win.shape[-1]
    full = lambda a: pl.BlockSpec(a.shape, lambda b, pt: (0,) * a.ndim)
    page_specs = [pl.BlockSpec((1, 1) + cache_l.shape[2:],
                               functools.partial(lambda b, pt, j: (l, pt[b, j], 0, 0, 0), j=j))
                  for j in range(n_pages)]
    consts = [cp['w1'], cp['b1'], cp['w2'], cp['b2'], tabs['bias_c'], tabs['bias_s'], tabs['bias_w'], tabs['bias_0'],
              tabs['overlap'], tabs['gsum'], tabs['bdmask'], tabs['expand']]
    grid_spec = pltpu.PrefetchScalarGridSpec(
        num_scalar_prefetch=1,
        grid=(bsz,),
        in_specs=page_specs + [
            pl.BlockSpec((1, 1, 2, KV_W, nbuf), lambda b, pt: (l, b, 0, 0, 0)),
            pl.BlockSpec((1, H_B, KV_W), lambda b, pt: (b, 0, 0)),
            pl.BlockSpec((1, 1, 6 * KV_W), lambda b, pt: (b, 0, 0)),
            pl.BlockSpec((1, H_B, 3), lambda b, pt: (b, 0, 0)),
            pl.BlockSpec((1, H_B, HD), lambda b, pt: (b, 0, 0)),
        ] + [full(a) for a in consts],
        out_specs=pl.BlockSpec((1, H_B, HD), lambda b, pt: (b, 0, 0)),
        scratch_shapes=[pltpu.VMEM((2, KV_W // LANES, past, LANES), F32), pltpu.VMEM((nch, KV_W), F32),
                        pltpu.VMEM((nch, KV_W), F32)],
    )
    return pl.pallas_call(
        functools.partial(_nsa_sample_kernel, n_pages=n_pages, past=past),
        grid_spec=grid_spec,
        out_shape=jax.ShapeDtypeStruct((bsz, H_B, HD), BF16),
        compiler_params=_params(("arbitrary",)),
    )(page_table, *([cache_l] * n_pages), win, qbd, new_rows, gl, zb, *consts)


def _t5_bucket(dist):
    n = jnp.maximum(dist, 0)
    max_exact = NUM_BUCKETS // 2
    nf = jnp.maximum(n, 1).astype(F32)
    large = max_exact + (jnp.log(nf / max_exact) / math.log(MAX_DIST / max_exact)
                         * (NUM_BUCKETS - max_exact)).astype(jnp.int32)
    large = jnp.minimum(large, NUM_BUCKETS - 1)
    return jnp.where(n < max_exact, n, large)


def _bias_of(rel_bias, dist):
    return jnp.moveaxis(rel_bias[_t5_bucket(dist)], -1, 0).astype(F32)


def _overlap(nch, nsel):
    ci = np.arange(nch)[:, None] * CMP_STRIDE
    sj = np.arange(nsel)[None, :] * SEL_LEN
    ov = ((ci < sj + SEL_LEN) & (ci + CMP_LEN > sj)).astype(np.float32)
    ov[nch - 1:, :] = 0.0
    return ov


def _toeplitz(f, starts, length):
    return jnp.stack([f[:, s:s + length] for s in starts], axis=1)


def _prompt_tables(rel_bias, t):
    nch = t // CMP_STRIDE
    nsel = t // SEL_LEN
    f = _bias_of(rel_bias, jnp.arange(max(t, 2 * KB), dtype=jnp.int32))
    lead = lambda n: jnp.concatenate([jnp.broadcast_to(f[:, :1], (H_B, n)), f], axis=1)
    pc = CMP_STRIDE * (nch - 1) + CMP_LEN - 1
    expand = (np.arange(t)[:, None] // SEL_LEN == np.arange(nsel)[None, :]).astype(np.float32)
    return {
        'bias_c': _toeplitz(lead(pc), [pc - CMP_STRIDE * n - (CMP_LEN - 1) for n in range(nch)], t),
        'd0': _toeplitz(lead(KB - 1), [KB - 1 - i for i in range(KB)], TQ),
        'd1': _toeplitz(f, [KB - i for i in range(KB)], TQ),
        'far': rel_bias[NUM_BUCKETS - 1].astype(F32),
        'overlap': jnp.asarray(_overlap(nch, nsel).T, BF16),
        'expand': jnp.asarray(expand, BF16),
    }


def _sample_tables(rel_bias, past, nbuf):
    nch = past // CMP_STRIDE
    nsel = past // SEL_LEN + 1
    nsp = -(-nsel // LANES) * LANES
    cmp_end = jnp.arange(nch, dtype=jnp.int32) * CMP_STRIDE + CMP_LEN - 1
    ov = np.zeros((nch, nsp), np.float32)
    ov[:, :nsel] = _overlap(nch, nsel)
    hh = np.arange(H_B)
    gsum = (hh[:, None] // R_Q == hh[None, :] // R_Q).astype(np.float32)
    bdm = (hh[:, None] // R_Q == np.arange(KV_W)[None, :] // HD).astype(np.float32)
    expand = (np.arange(nsp)[:, None] == np.arange(past)[None, :] // SEL_LEN).astype(np.float32)
    return {
        'bias_c': _bias_of(rel_bias, past - cmp_end),
        'bias_s': _bias_of(rel_bias, past - jnp.arange(past, dtype=jnp.int32)),
        'bias_w': _bias_of(rel_bias, nbuf - jnp.arange(nbuf, dtype=jnp.int32)),
        'bias_0': _bias_of(rel_bias, jnp.zeros((1,), jnp.int32)),
        'overlap': jnp.asarray(ov, BF16),
        'gsum': jnp.asarray(gsum, BF16),
        'bdmask': jnp.asarray(bdm, F32),
        'expand': jnp.asarray(expand, BF16),
    }


def _layer_params(l, w_in, mu_shift, rw_w0, rw_w2, rw_a0, rw_a2, rw_kk, rw_ka, rw_rk, rw_gn_g, rw_gn_b,
                  rw_v0, rw_v1, rw_v2, cmp_w1, cmp_b1, cmp_w2, cmp_b2, w_up_a, w_up_b, w_out, ln_g, ln_b):
    w = w_in[l]
    b0 = A_COLS
    q_kv = w[:, b0:b0 + D_B + 6 * KV_W]
    gl = w[:, b0 + D_B + 6 * KV_W:b0 + D_B + 6 * KV_W + 3 * H_B].reshape(D_MODEL, G_KV, 3 * R_Q)
    gl = jnp.pad(gl, ((0, 0), (0, 0), (0, GL_PAD - 3 * R_Q))).reshape(D_MODEL, G_KV * GL_PAD)
    zb = w[:, b0 + D_B + 6 * KV_W + 3 * H_B:b0 + D_B + 6 * KV_W + 3 * H_B + D_B]
    half = CMP_STRIDE * HD
    w1 = cmp_w1[l]
    w1r = jnp.concatenate([w1[:, :half].reshape(2, CMP_STRIDE, HD, CMP_HID),
                           w1[:, half:].reshape(2, CMP_STRIDE, HD, CMP_HID)], axis=-1)
    zero = jnp.zeros_like(w1r)
    w1r = jnp.stack([jnp.concatenate([w1r, zero], axis=2), jnp.concatenate([zero, w1r], axis=2)], axis=2)
    row = lambda a: a.reshape(1, -1).astype(F32)
    p = {
        'wa': w[:, :A_COLS].astype(BF16),
        'wb': jnp.concatenate([q_kv, gl, zb], axis=1).astype(BF16),
        'wg': w[:, b0 + D_B + 6 * KV_W + 3 * H_B + D_B:].astype(BF16),
        'wc': w[:, b0 + D_B:b0 + D_B + 2 * KV_W].astype(BF16),
        'mu': row(mu_shift[l]), 'w0': row(rw_w0[l]), 'w2': rw_w2[l].astype(BF16), 'a0': row(rw_a0[l]),
        'a2': rw_a2[l].astype(BF16), 'kk': row(rw_kk[l]), 'ka': row(rw_ka[l]), 'rk': row(rw_rk[l]),
        'gn_g': row(rw_gn_g[l]), 'gn_b': row(rw_gn_b[l]),
        'cmp': {'w1': w1r.astype(BF16), 'b1': cmp_b1[l].reshape(2, 1, CMP_HID).astype(F32),
                'w2': cmp_w2[l].astype(BF16), 'b2': cmp_b2[l].reshape(2, 1, HD).astype(F32)},
        'w_up_a': w_up_a[l].astype(BF16), 'w_up_b': w_up_b[l].astype(BF16), 'w_out': w_out[l].astype(BF16),
        'ln_g': row(ln_g[l]), 'ln_b': row(ln_b[l]),
    }
    if l > 0:
        p['v0'] = row(rw_v0[l - 1])
        p['v1'] = rw_v1[l - 1].astype(BF16)
        p['v2'] = rw_v2[l - 1].astype(BF16)
    return p


def _project(x2, p):
    m = x2.shape[0]
    tm = 512
    ha = _matmul(x2, p['wa'], tm, A_COLS // 3)
    hb = _matmul(x2, p['wb'], tm, 1024)
    hg = _matmul(x2, p['wg'], tm, 1024)
    return ha, hb, hg


def _finish(x2, o_a, o_b, hg, p):
    merged = _up_gate(o_a, o_b, p['w_up_a'], p['w_up_b'], hg)
    return _out_ln(merged, p['w_out'], x2, p['ln_g'], p['ln_b'])


def _prompt_layer(x2, vfirst, p, tabs, bsz, t):
    ha, hb, hg = _project(x2, p)
    o_a, vfirst, wkv = _rwkv_prompt(ha, vfirst, p, bsz, t)
    hc = _matmul(x2, p['wc'], 512, 2 * KV_W)
    kvp = _kv_prep(hb, hc, p['cmp'], bsz, t)
    o_b = _nsa_prompt(hb, kvp, tabs, bsz, t)
    y = _finish(x2, o_a, o_b, hg, p)
    kvc = D_B
    new_rows = hb[:, kvc:kvc + 4 * KV_W].reshape(bsz, t, 4, G_KV, HD)
    nwin = min(WIN, t)
    win_state = hb[:, kvc + 4 * KV_W:kvc + 6 * KV_W].reshape(bsz, t, 2, G_KV, HD)[:, t - nwin:]
    shift = ha.reshape(bsz, t, A_COLS)[:, t - 1]
    return y, vfirst, (new_rows, win_state, wkv, shift)


def _sample_layer(x2, vfirst, l, p, tabs, cache_l, win_l, cache_win_kv, state_wkv, state_shift, page_table):
    bsz = x2.shape[0]
    ha, hb, hg = _project(x2, p)
    o_a, vfirst, wkv = _rwkv_sample(ha, state_shift[l], state_wkv[l], vfirst, p)
    kvc = D_B
    q = hb[:, :D_B].reshape(bsz, G_KV, R_Q, HD) * SCALE
    eye = jnp.eye(G_KV, dtype=F32)
    qbd = (q[:, :, :, None, :] * eye[None, :, None, :, None]).reshape(bsz, H_B, KV_W).astype(BF16)
    new6 = hb[:, kvc:kvc + 6 * KV_W]
    gl0 = kvc + 6 * KV_W
    gl = hb[:, gl0:gl0 + G_KV * GL_PAD].reshape(bsz, G_KV, GL_PAD)[:, :, :3 * R_Q].reshape(bsz, H_B, 3)
    zb = hb[:, gl0 + G_KV * GL_PAD:].reshape(bsz, H_B, HD)
    nbuf = cache_win_kv.shape[2]
    o_b = _nsa_sample(cache_l, l, page_table, win_l, qbd, new6[:, None, :], gl, zb, p['cmp'], tabs)
    y = _finish(x2, o_a, o_b.reshape(bsz, D_B), hg, p)
    new_rows = new6[:, :4 * KV_W].reshape(bsz, 1, 4, G_KV, HD)
    new_win = new6[:, 4 * KV_W:].reshape(bsz, 1, 2, G_KV, HD)
    win_state = jnp.concatenate([cache_win_kv[l], new_win], axis=1)[:, -nbuf:]
    return y, vfirst, (new_rows, win_state, wkv, ha)


def kernel(x_prompt, x_sample, cache_kv, cache_win_kv, state_wkv, state_shift, page_table, w_in, mu_shift, rw_w0, rw_w2, rw_a0, rw_a2, rw_kk, rw_ka, rw_rk, rw_gn_g, rw_gn_b, rw_v0, rw_v1, rw_v2, cmp_w1, cmp_b1, cmp_w2, cmp_b2, rel_bias, w_up_a, w_up_b, w_out, ln_g, ln_b):
    bsz, t, _ = x_prompt.shape
    dec_b = x_sample.shape[0]
    n_pages = page_table.shape[1]
    depth, n_phys = cache_kv.shape[:2]
    cache_l = jnp.transpose(cache_kv, (0, 1, 3, 4, 5, 2)).reshape(depth, n_phys, 4, KV_W, PAGE_SIZE)
    win_l = jnp.transpose(cache_win_kv, (0, 1, 3, 4, 5, 2)).reshape(depth, dec_b, 2, KV_W, cache_win_kv.shape[2])
    tabs_p = _prompt_tables(rel_bias, t)
    tabs_s = _sample_tables(rel_bias, n_pages * PAGE_SIZE, cache_win_kv.shape[2])
    y_p = x_prompt.reshape(bsz * t, D_MODEL)
    y_s = x_sample.reshape(dec_b, D_MODEL)
    vf_p, vf_s = None, None
    st_p, st_s = [], []
    for l in range(depth):
        p = _layer_params(l, w_in, mu_shift, rw_w0, rw_w2, rw_a0, rw_a2, rw_kk, rw_ka, rw_rk, rw_gn_g, rw_gn_b,
                          rw_v0, rw_v1, rw_v2, cmp_w1, cmp_b1, cmp_w2, cmp_b2, w_up_a, w_up_b, w_out, ln_g, ln_b)
        y_p, vf_p, sp = _prompt_layer(y_p, vf_p, p, tabs_p, bsz, t)
        y_s, vf_s, ss = _sample_layer(y_s, vf_s, l, p, tabs_s, cache_l, win_l, cache_win_kv, state_wkv, state_shift,
                                      page_table)
        st_p.append(sp)
        st_s.append(ss)
    stack = lambda st, i: jnp.stack([s[i] for s in st])
    return (y_p.reshape(bsz, t, D_MODEL), y_s.reshape(dec_b, 1, D_MODEL),
            stack(st_p, 0), stack(st_p, 1), stack(st_p, 2), stack(st_p, 3),
            stack(st_s, 0), stack(st_s, 1), stack(st_s, 2), stack(st_s, 3))
```

```python
import functools
import math

import numpy as np
import jax
import jax.numpy as jnp
from jax import lax
from jax.experimental import pallas as pl
from jax.experimental.pallas import tpu as pltpu

D_MODEL = 2048
DEPTH = 2
PAGE_SIZE = 128
HS = 64
D_A = D_MODEL // 2
H_A = D_A // HS
R_W = 64
R_A = 64
R_V = 32
GN_EPS = 64e-5
HD = 64
D_B = D_MODEL // 2
H_B = D_B // HD
G_KV = 4
R_Q = H_B // G_KV
KV_W = G_KV * HD
CMP_LEN = 32
CMP_STRIDE = 16
CMP_HID = 128
SEL_LEN = 64
N_TOP = 16
WIN = 512
NUM_BUCKETS = 32
MAX_DIST = 128
SCALE = HD ** -0.5
A_COLS = 4 * D_A + R_W + R_A
ALPHA = (2 * DEPTH) ** 0.25
LN_EPS = 1e-5
NEG = -1e30

F32 = jnp.float32
BF16 = jnp.bfloat16

LANES = 128
VMEM_LIMIT = 56 * 1024 * 1024
CHUNK = 64
TQ = 128
KB = 128
FAR_GROUP = 4
GL_PAD = LANES
HB_COLS = D_B + 6 * KV_W + G_KV * GL_PAD + D_B

NT = (((1,), (1,)), ((), ()))
TN = (((0,), (0,)), ((), ()))


def _params(sem):
    return pltpu.CompilerParams(dimension_semantics=sem, vmem_limit_bytes=VMEM_LIMIT)


def _bdot(a, b, dims=None):
    a = a.astype(BF16)
    b = b.astype(BF16)
    if dims is None:
        return jnp.dot(a, b, preferred_element_type=F32)
    return lax.dot_general(a, b, dims, preferred_element_type=F32)


def _sigmoid(x):
    return 1.0 / (1.0 + jnp.exp(-x))


def _silu(x):
    return x * _sigmoid(x)


def _gelu_tanh(x):
    return 0.5 * x * (1.0 + jnp.tanh(math.sqrt(2.0 / math.pi) * (x + 0.044715 * (x * x * x))))


def _mm_kernel(x_ref, w_ref, o_ref):
    o_ref[...] = _bdot(x_ref[...], w_ref[...])


def _matmul(x, w, tm, tn):
    m, k = x.shape
    n = w.shape[1]
    tm = min(tm, m)
    return pl.pallas_call(
        _mm_kernel,
        grid=(m // tm, n // tn),
        in_specs=[pl.BlockSpec((tm, k), lambda i, j: (i, 0)),
                  pl.BlockSpec((k, tn), lambda i, j: (0, j))],
        out_specs=pl.BlockSpec((tm, tn), lambda i, j: (i, j)),
        out_shape=jax.ShapeDtypeStruct((m, n), F32),
        compiler_params=_params(("parallel", "parallel")),
    )(x, w)


def _up_kernel(oa_ref, ob_ref, wa_ref, wb_ref, ga_ref, gb_ref, o_ref):
    ua = _bdot(oa_ref[...], wa_ref[...])
    ub = _bdot(ob_ref[...], wb_ref[...])
    o_ref[...] = (_sigmoid(ga_ref[...]) * ua + _sigmoid(gb_ref[...]) * ub).astype(o_ref.dtype)


def _up_gate(o_a, o_b, w_up_a, w_up_b, hg, tm=512, tn=1024):
    m = o_a.shape[0]
    tm = min(tm, m)
    nb = D_MODEL // tn
    return pl.pallas_call(
        _up_kernel,
        grid=(m // tm, nb),
        in_specs=[pl.BlockSpec((tm, D_A), lambda i, j: (i, 0)),
                  pl.BlockSpec((tm, D_B), lambda i, j: (i, 0)),
                  pl.BlockSpec((D_A, tn), lambda i, j: (0, j)),
                  pl.BlockSpec((D_B, tn), lambda i, j: (0, j)),
                  pl.BlockSpec((tm, tn), lambda i, j: (i, j)),
                  pl.BlockSpec((tm, tn), lambda i, j: (i, j + nb))],
        out_specs=pl.BlockSpec((tm, tn), lambda i, j: (i, j)),
        out_shape=jax.ShapeDtypeStruct((m, D_MODEL), BF16),
        compiler_params=_params(("parallel", "parallel")),
    )(o_a, o_b, w_up_a, w_up_b, hg, hg)


def _out_ln_kernel(m_ref, w_ref, x_ref, g_ref, b_ref, o_ref):
    u = ALPHA * x_ref[...] + _bdot(m_ref[...], w_ref[...])
    mu = jnp.mean(u, axis=-1, keepdims=True)
    d = u - mu
    var = jnp.mean(d * d, axis=-1, keepdims=True)
    o_ref[...] = d * lax.rsqrt(var + LN_EPS) * g_ref[...] + b_ref[...]


def _out_ln(merged, w_out, x, ln_g, ln_b, tm=256):
    m = x.shape[0]
    tm = min(tm, m)
    return pl.pallas_call(
        _out_ln_kernel,
        grid=(m // tm,),
        in_specs=[pl.BlockSpec((tm, D_MODEL), lambda i: (i, 0)),
                  pl.BlockSpec((D_MODEL, D_MODEL), lambda i: (0, 0)),
                  pl.BlockSpec((tm, D_MODEL), lambda i: (i, 0)),
                  pl.BlockSpec((1, D_MODEL), lambda i: (0, 0)),
                  pl.BlockSpec((1, D_MODEL), lambda i: (0, 0))],
        out_specs=pl.BlockSpec((tm, D_MODEL), lambda i: (i, 0)),
        out_shape=jax.ShapeDtypeStruct((m, D_MODEL), F32),
        compiler_params=_params(("parallel",)),
    )(merged, w_out, x, ln_g, ln_b)


def _rwkv_premix(xm, vfirst, w0, w2, a0, a2, kkp, ka, vgate):
    r = xm[:, 0:D_A]
    k = xm[:, D_A:2 * D_A]
    v = xm[:, 2 * D_A:3 * D_A]
    w_lo = xm[:, 3 * D_A:3 * D_A + R_W]
    a_lo = xm[:, 3 * D_A + R_W:3 * D_A + R_W + R_A]
    z = xm[:, 3 * D_A + R_W + R_A:A_COLS]
    t = w0 + _bdot(jnp.tanh(w_lo), w2)
    lw = -math.exp(-0.5) * _sigmoid(t)
    if vgate is not None:
        v0, v1, v2 = vgate
        vg = _sigmoid(v0 + _bdot(_bdot(v, v1), v2))
        v = v + (vfirst - v) * vg
    lr = _sigmoid(a0 + _bdot(a_lo, a2))
    kkr = k * kkp
    k2 = k * (1.0 + (lr - 1.0) * ka)
    return r, lw, k2, v, lr, kkr, z


def _head_post(y, r, k2, v, z, rk, gn_g, gn_b):
    mu = jnp.mean(y, axis=-1, keepdims=True)
    d = y - mu
    var = jnp.mean(d * d, axis=-1, keepdims=True)
    yn = d * lax.rsqrt(var + GN_EPS) * gn_g + gn_b
    bonus = jnp.sum(r * k2 * rk, axis=-1, keepdims=True) * v
    return (yn + bonus) * _silu(z)


def _rwkv_prompt_kernel(*refs, has_vgate):
    if has_vgate:
        (fa_ref, vf_ref, mu_ref, w0_ref, w2_ref, a0_ref, a2_ref, kkp_ref, ka_ref, rk_ref, gg_ref, gb_ref,
         v0_ref, v1_ref, v2_ref, o_ref, s_out_ref, s_ref, last_ref) = refs
    else:
        (fa_ref, mu_ref, w0_ref, w2_ref, a0_ref, a2_ref, kkp_ref, ka_ref, rk_ref, gg_ref, gb_ref,
         o_ref, vf_out_ref, s_out_ref, s_ref, last_ref) = refs
    c = pl.program_id(1)
    nc = pl.num_programs(1)
    C = CHUNK

    @pl.when(c == 0)
    def _():
        s_ref[...] = jnp.zeros_like(s_ref)
        last_ref[...] = jnp.zeros_like(last_ref)

    x = fa_ref[...]
    row = lax.broadcasted_iota(jnp.int32, (C, 1), 0)
    prev = jnp.where(row == 0, last_ref[0:1, :], pltpu.roll(x, 1, axis=0))
    last_ref[0:1, :] = x[C - 1:C, :]
    xm = x + (prev - x) * mu_ref[...]
    if has_vgate:
        vgate = (v0_ref[...], v1_ref[...], v2_ref[...])
        vfirst = vf_ref[...]
    else:
        vgate, vfirst = None, None
    r, lw, k2, v, lr, kkr, z = _rwkv_premix(xm, vfirst, w0_ref[...], w2_ref[...], a0_ref[...], a2_ref[...],
                                            kkp_ref[...], ka_ref[...], vgate)
    if not has_vgate:
        vf_out_ref[...] = v

    ri = lax.broadcasted_iota(jnp.int32, (C, C), 0)
    ci = lax.broadcasted_iota(jnp.int32, (C, C), 1)
    tri_i = ri >= ci
    tri_s = ri > ci
    tri_b = jnp.where(tri_i, 1.0, 0.0).astype(BF16)
    eye = jnp.where(ri == ci, 1.0, 0.0).astype(F32)
    lw_hi = lw.astype(BF16)
    lw_lo = (lw - lw_hi.astype(F32)).astype(BF16)
    L = jnp.dot(tri_b, lw_hi, preferred_element_type=F32) + jnp.dot(tri_b, lw_lo, preferred_element_type=F32)
    LC = L[C - 1:C, :]
    e_in = jnp.exp(L)
    e_ex = jnp.exp(L - lw)
    e_neg = jnp.exp(-L)
    e_rem = jnp.exp(LC - L)
    pc = jnp.exp(LC)
    rk = rk_ref[...]
    gg = gg_ref[...]
    gb = gb_ref[...]

    heads = range(H_A)
    sls = [slice(h * HS, (h + 1) * HS) for h in heads]
    bf = lambda xs: [x.astype(BF16) for x in xs]
    kk = []
    for sl in sls:
        u = kkr[:, sl]
        kk.append(u * lax.rsqrt(jnp.maximum(jnp.sum(u * u, axis=-1, keepdims=True), 1e-24)))
    r_h = [r[:, sl] for sl in sls]
    k_h = [k2[:, sl] for sl in sls]
    v_h = [v[:, sl] for sl in sls]
    v_b = bf(v_h)
    b_h = [kk[h] * lr[:, sls[h]] for h in heads]
    at = [(-kk[h]) * e_ex[:, sls[h]] for h in heads]
    rt = [r_h[h] * e_in[:, sls[h]] for h in heads]
    bt = [b_h[h] * e_neg[:, sls[h]] for h in heads]
    kt = [k_h[h] * e_neg[:, sls[h]] for h in heads]
    bh = bf([b_h[h] * e_rem[:, sls[h]] for h in heads])
    kh = bf([k_h[h] * e_rem[:, sls[h]] for h in heads])
    g = [_bdot(jnp.concatenate([at[h], rt[h]], axis=0), jnp.concatenate([bt[h], kt[h]], axis=0), NT)
         for h in heads]
    a_ab = bf([jnp.where(tri_s, x[:C, :C], 0.0) for x in g])
    a_ak = bf([jnp.where(tri_s, x[:C, C:], 0.0) for x in g])
    a_rb = bf([jnp.where(tri_i, x[C:, :C], 0.0) for x in g])
    a_rk = bf([jnp.where(tri_i, x[C:, C:], 0.0) for x in g])
    tm = [eye + x.astype(F32) for x in a_ab]
    ap = bf([_bdot(x, x) for x in a_ab])
    n = 2
    while n < C:
        tm_next = [tm[h] + _bdot(tm[h], ap[h]) for h in heads]
        if 2 * n < C:
            ap = bf([_bdot(x, x) for x in ap])
        tm = tm_next
        n *= 2
    akv = [_bdot(a_ak[h], v_b[h]) for h in heads]
    wu = bf([_bdot(tm[h], jnp.concatenate([at[h], akv[h]], axis=1)) for h in heads])
    arw = [_bdot(a_rb[h], wu[h]) for h in heads]
    yh = [arw[h][:, HS:] + _bdot(a_rk[h], v_b[h]) for h in heads]
    s_old = [s_ref[h] for h in heads]
    s_b = bf(s_old)
    y = [_bdot(rt[h] + arw[h][:, :HS], s_b[h], NT) + yh[h] for h in heads]
    bw = [_bdot(bh[h], wu[h][:, :HS], TN) for h in heads]
    nt = [_bdot(jnp.concatenate([wu[h][:, HS:], v_b[h]], axis=0), jnp.concatenate([bh[h], kh[h]], axis=0), TN)
          for h in heads]
    for h in heads:
        s_ref[h] = s_old[h] * pc[:, sls[h]] + _bdot(s_b[h], bw[h], NT) + nt[h]
    for h in heads:
        sl = sls[h]
        o_ref[:, sl] = _head_post(y[h], r_h[h], k_h[h], v_h[h], z[:, sl], rk[:, sl], gg[:, sl],
                                  gb[:, sl]).astype(o_ref.dtype)

    @pl.when(c == nc - 1)
    def _():
        s_out_ref[0] = s_ref[...]


def _rwkv_prompt(ha, vfirst, p, bsz, t):
    has_vgate = vfirst is not None
    nc = t // CHUNK
    row_spec = lambda w: pl.BlockSpec((CHUNK, w), lambda b, c: (b * nc + c, 0))
    full = lambda a: pl.BlockSpec(a.shape, lambda b, c: (0,) * a.ndim)
    ins = [ha]
    in_specs = [row_spec(A_COLS)]
    if has_vgate:
        ins.append(vfirst)
        in_specs.append(row_spec(D_A))
    names = ['mu', 'w0', 'w2', 'a0', 'a2', 'kk', 'ka', 'rk', 'gn_g', 'gn_b'] + (['v0', 'v1', 'v2'] if has_vgate else [])
    for nme in names:
        ins.append(p[nme])
        in_specs.append(full(p[nme]))
    out_shape = [jax.ShapeDtypeStruct((bsz * t, D_A), BF16)]
    out_specs = [row_spec(D_A)]
    if not has_vgate:
        out_shape.append(jax.ShapeDtypeStruct((bsz * t, D_A), F32))
        out_specs.append(row_spec(D_A))
    out_shape.append(jax.ShapeDtypeStruct((bsz, H_A, HS, HS), F32))
    out_specs.append(pl.BlockSpec((1, H_A, HS, HS), lambda b, c: (b, 0, 0, 0)))
    outs = pl.pallas_call(
        functools.partial(_rwkv_prompt_kernel, has_vgate=has_vgate),
        grid=(bsz, nc),
        in_specs=in_specs,
        out_specs=out_specs,
        out_shape=out_shape,
        scratch_shapes=[pltpu.VMEM((H_A, HS, HS), F32), pltpu.VMEM((8, A_COLS), F32)],
        compiler_params=_params(("parallel", "arbitrary")),
    )(*ins)
    if has_vgate:
        return outs[0], vfirst, outs[1]
    return outs[0], outs[1], outs[2]


def _rwkv_sample_kernel(*refs, has_vgate, bt):
    if has_vgate:
        (fa_ref, prev_ref, s_in_ref, vf_ref, mu_ref, w0_ref, w2_ref, a0_ref, a2_ref, kkp_ref, ka_ref, rk_ref,
         gg_ref, gb_ref, v0_ref, v1_ref, v2_ref, o_ref, s_out_ref, ops_ref, y_ref) = refs
    else:
        (fa_ref, prev_ref, s_in_ref, mu_ref, w0_ref, w2_ref, a0_ref, a2_ref, kkp_ref, ka_ref, rk_ref,
         gg_ref, gb_ref, o_ref, vf_out_ref, s_out_ref, ops_ref, y_ref) = refs
    x = fa_ref[...]
    xm = x + (prev_ref[...] - x) * mu_ref[...]
    if has_vgate:
        vgate = (v0_ref[...], v1_ref[...], v2_ref[...])
        vfirst = vf_ref[...]
    else:
        vgate, vfirst = None, None
    r, lw, k2, v, lr, kkr, z = _rwkv_premix(xm, vfirst, w0_ref[...], w2_ref[...], a0_ref[...], a2_ref[...],
                                            kkp_ref[...], ka_ref[...], vgate)
    if not has_vgate:
        vf_out_ref[...] = v
    w = jnp.exp(lw)
    for h in range(H_A):
        sl = slice(h * HS, (h + 1) * HS)
        kk = kkr[:, sl]
        kk = kk * lax.rsqrt(jnp.maximum(jnp.sum(kk * kk, axis=-1, keepdims=True), 1e-24))
        ops_ref[0, :, sl] = -kk
        ops_ref[1, :, sl] = kk * lr[:, sl]
    ops_ref[2] = w
    ops_ref[3] = k2
    ops_ref[4] = v
    ops_ref[5] = r
    ri = lax.broadcasted_iota(jnp.int32, (HS, HS), 0)
    ci = lax.broadcasted_iota(jnp.int32, (HS, HS), 1)
    eye = jnp.where(ri == ci, 1.0, 0.0).astype(F32)

    heads = range(H_A)
    sls = [slice(h * HS, (h + 1) * HS) for h in heads]
    eye_b = eye.astype(BF16)
    rows_of = lambda x: jnp.broadcast_to(x, (HS, HS)).astype(BF16)
    for b in range(bt):
        row = lambda i: [ops_ref[i, b:b + 1, sl] for sl in sls]
        a_row, b_row, w_row, k_row, v_row, r_row = (row(i) for i in range(6))
        s = [s_in_ref[b, h] for h in heads]
        sa = [_bdot(s[h], rows_of(a_row[h]), NT) for h in heads]
        v_hi = [v_row[h].astype(BF16) for h in heads]
        v_lo = [(v_row[h] - v_hi[h].astype(F32)).astype(BF16) for h in heads]
        v_bc = [_bdot(eye_b, rows_of(v_hi[h]), NT) + _bdot(eye_b, rows_of(v_lo[h]), NT) for h in heads]
        s_new = [s[h] * w_row[h] + sa[h] * b_row[h] + v_bc[h] * k_row[h] for h in heads]
        for h in heads:
            s_out_ref[b, h] = s_new[h]
        y_bc = [_bdot(s_new[h], rows_of(r_row[h]), NT) for h in heads]
        for h in heads:
            y_ref[b:b + 1, sls[h]] = jnp.sum(eye * y_bc[h], axis=0, keepdims=True)
    y = y_ref[...]
    rk = rk_ref[...]
    gg = gg_ref[...]
    gb = gb_ref[...]
    for h in range(H_A):
        sl = slice(h * HS, (h + 1) * HS)
        o_ref[:, sl] = _head_post(y[:, sl], r[:, sl], k2[:, sl], v[:, sl], z[:, sl], rk[:, sl], gg[:, sl],
                                  gb[:, sl]).astype(o_ref.dtype)


def _rwkv_sample(ha, prev, s_in, vfirst, p, bt=8):
    bsz = ha.shape[0]
    has_vgate = vfirst is not None
    row_spec = lambda w: pl.BlockSpec((bt, w), lambda i: (i, 0))
    full = lambda a: pl.BlockSpec(a.shape, lambda i: (0,) * a.ndim)
    st_spec = pl.BlockSpec((bt, H_A, HS, HS), lambda i: (i, 0, 0, 0))
    ins = [ha, prev, s_in]
    in_specs = [row_spec(A_COLS), row_spec(A_COLS), st_spec]
    if has_vgate:
        ins.append(vfirst)
        in_specs.append(row_spec(D_A))
    names = ['mu', 'w0', 'w2', 'a0', 'a2', 'kk', 'ka', 'rk', 'gn_g', 'gn_b'] + (['v0', 'v1', 'v2'] if has_vgate else [])
    for nme in names:
        ins.append(p[nme])
        in_specs.append(full(p[nme]))
    out_shape = [jax.ShapeDtypeStruct((bsz, D_A), BF16)]
    out_specs = [row_spec(D_A)]
    if not has_vgate:
        out_shape.append(jax.ShapeDtypeStruct((bsz, D_A), F32))
        out_specs.append(row_spec(D_A))
    out_shape.append(jax.ShapeDtypeStruct((bsz, H_A, HS, HS), F32))
    out_specs.append(st_spec)
    outs = pl.pallas_call(
        functools.partial(_rwkv_sample_kernel, has_vgate=has_vgate, bt=bt),
        grid=(bsz // bt,),
        in_specs=in_specs,
        out_specs=out_specs,
        out_shape=out_shape,
        scratch_shapes=[pltpu.VMEM((6, bt, D_A), F32), pltpu.VMEM((bt, D_A), F32)],
        compiler_params=_params(("parallel",)),
    )(*ins)
    if has_vgate:
        return outs[0], vfirst, outs[1]
    return outs[0], outs[1], outs[2]


def _compress_rows(load_rows, w1_ref, b1, w2, b2, kv, nch):
    accs = [jnp.zeros((nch, 2 * CMP_HID), F32) for _ in range(G_KV)]
    for tau in range(CMP_STRIDE):
        for pair in range(G_KV // 2):
            rows = load_rows(tau, pair).astype(BF16)
            for parity in range(2):
                g = 2 * pair + parity
                accs[g] = accs[g] + jnp.dot(rows, w1_ref[kv, tau, parity], preferred_element_type=F32)
    outs = []
    for acc in accs:
        h = acc[:, :CMP_HID] + pltpu.roll(acc[:, CMP_HID:], nch - 1, axis=0) + b1
        outs.append(_bdot(_gelu_tanh(h), w2) + b2)
    return outs


def _kv_prep_kernel(kc_ref, ks_ref, kw_ref, w1_ref, b1_ref, w2_ref, b2_ref,
                    ks_o, vs_o, kw_o, vw_o, kc_o, vc_o, *, nch):
    npair = KV_W // LANES

    def put_transposed(out, pair, x):
        xt = x.T
        out[0, 2 * pair] = xt[0:HD].astype(BF16)
        out[0, 2 * pair + 1] = xt[HD:2 * HD].astype(BF16)

    for g in range(G_KV):
        sl = slice(g * HD, (g + 1) * HD)
        ks_o[0, g] = ks_ref[:, sl].astype(BF16)
        kw_o[0, g] = kw_ref[:, sl].astype(BF16)
    for pair in range(npair):
        sl2 = slice(KV_W + pair * LANES, KV_W + (pair + 1) * LANES)
        put_transposed(vs_o, pair, ks_ref[:, sl2])
        put_transposed(vw_o, pair, kw_ref[:, sl2])
    for kv in (0, 1):
        load = lambda tau, pair, kv=kv: kc_ref[pl.ds(2 * npair * tau + kv * npair + pair, nch,
                                                     stride=2 * npair * CMP_STRIDE), :]
        res = _compress_rows(load, w1_ref, b1_ref[kv], w2_ref[kv], b2_ref[kv], kv, nch)
        if kv == 0:
            for g in range(G_KV):
                kc_o[0, g] = res[g].astype(BF16)
        else:
            for pair in range(npair):
                put_transposed(vc_o, pair, jnp.concatenate([res[2 * pair], res[2 * pair + 1]], axis=1))


def _kv_prep(hb, hc, cp, bsz, t):
    nch = t // CMP_STRIDE
    blk = lambda j: pl.BlockSpec((t, 2 * KV_W), lambda b: (b, j))
    full = lambda a: pl.BlockSpec(a.shape, lambda b: (0,) * a.ndim)
    c0 = D_B // (2 * KV_W)
    lane_rows = 2 * KV_W // LANES
    hc = hc.reshape(bsz * t * lane_rows, LANES)
    def arr(n, transposed):
        shp = (G_KV, HD, n) if transposed else (G_KV, n, HD)
        return jax.ShapeDtypeStruct((bsz,) + shp, BF16), pl.BlockSpec((1,) + shp, lambda b: (b, 0, 0, 0))

    outs = [arr(t, False), arr(t, True), arr(t, False), arr(t, True), arr(nch, False), arr(nch, True)]
    return pl.pallas_call(
        functools.partial(_kv_prep_kernel, nch=nch),
        grid=(bsz,),
        in_specs=[pl.BlockSpec((t * lane_rows, LANES), lambda b: (b, 0)), blk(c0 + 1), blk(c0 + 2),
                  full(cp['w1']), full(cp['b1']), full(cp['w2']), full(cp['b2'])],
        out_specs=[o[1] for o in outs],
        out_shape=[o[0] for o in outs],
        compiler_params=_params(("parallel",)),
    )(hc, hb, hb, cp['w1'], cp['b1'], cp['w2'], cp['b2'])


def _softmax_pieces(qst, pieces, m_ref, l_ref, acc_ref):
    ss = []
    for k, vt, bias, mask in pieces:
        s = jnp.dot(k, qst, preferred_element_type=F32) + bias
        ss.append(s if mask is None else jnp.where(mask, s, NEG))
    m_prev = m_ref[...]
    m_new = m_prev
    for s in ss:
        m_new = jnp.maximum(m_new, jnp.max(s, axis=0, keepdims=True))
    ps = []
    for (k, vt, bias, mask), s in zip(pieces, ss):
        p = jnp.exp(s - m_new)
        ps.append(p if mask is None else jnp.where(mask, p, 0.0))
    alpha = jnp.exp(m_prev - m_new)
    l_new = alpha * l_ref[...]
    for p in ps:
        l_new = l_new + jnp.sum(p, axis=0, keepdims=True)
    l_ref[...] = l_new
    p_all = jnp.concatenate([p.astype(BF16) for p in ps], axis=0)
    vt_all = jnp.concatenate([vt for _, vt, _, _ in pieces], axis=1)
    acc_ref[...] = alpha * acc_ref[...] + jnp.dot(vt_all, p_all, preferred_element_type=F32)
    m_ref[...] = m_new


def _nsa_prompt_kernel(q_ref, gl_ref, zb_ref, kc_ref, vc_ref, ks_ref, vs_ref, kw_ref, vw_ref,
                       bc_ref, d0_ref, d1_ref, far_ref, ov_ref, ex_ref, o_ref,
                       m_ref, l_ref, acc_ref, sel_ref, *, t, nch):
    g = pl.program_id(1)
    qt = pl.program_id(2)
    nsel = t // SEL_LEN
    qt_t = (q_ref[...] * SCALE).T
    qst = jnp.concatenate([qt_t[r * HD:(r + 1) * HD, :] for r in range(R_Q)], axis=1).astype(BF16)
    qpos = qt * TQ + lax.broadcasted_iota(jnp.int32, (1, TQ), 1)
    rep = lambda x: jnp.concatenate([x] * R_Q, axis=1)

    def reset():
        m_ref[...] = jnp.full_like(m_ref, NEG)
        l_ref[...] = jnp.zeros_like(l_ref)
        acc_ref[...] = jnp.zeros_like(acc_ref)

    def result():
        l = l_ref[...]
        return acc_ref[...] / jnp.where(l > 0.0, l, 1.0)

    nrow = lax.broadcasted_iota(jnp.int32, (nch, TQ), 0)
    mask_c = (qpos >= nrow * CMP_STRIDE + (CMP_LEN - 1)) & (nrow < nch - 1)
    mask_c4 = rep(mask_c)
    bias_c = jnp.concatenate([bc_ref[r] for r in range(R_Q)], axis=1)
    s = jnp.dot(kc_ref[0, 0], qst, preferred_element_type=F32) + bias_c
    s = jnp.where(mask_c4, s, NEG)
    p = jnp.where(mask_c4, jnp.exp(s - jnp.max(s, axis=0, keepdims=True)), 0.0)
    l = jnp.sum(p, axis=0, keepdims=True)
    p = p / jnp.where(l > 0.0, l, 1.0)
    o_c = jnp.dot(vc_ref[0, 0], p.astype(BF16), preferred_element_type=F32)
    psum = p[:, 0:TQ]
    for r in range(1, R_Q):
        psum = psum + p[:, r * TQ:(r + 1) * TQ]
    p_hi = psum.astype(BF16)
    p_lo = (psum - p_hi.astype(F32)).astype(BF16)
    ov = ov_ref[...]
    imp = jnp.dot(ov, p_hi, preferred_element_type=F32) + jnp.dot(ov, p_lo, preferred_element_type=F32)
    blk = lax.broadcasted_iota(jnp.int32, (nsel, TQ), 0)
    cur = qpos // SEL_LEN
    imp = jnp.where(blk * SEL_LEN <= qpos, imp, NEG)
    imp = jnp.where((blk == 0) | (blk == cur) | (blk == cur - 1), -NEG, imp)
    rank = jnp.zeros((nsel, TQ), F32)
    for s2 in range(nsel):
        other = imp[s2:s2 + 1, :]
        rank = rank + jnp.where((other > imp) | ((other == imp) & (blk > s2)), 1.0, 0.0)
    sel = jnp.where(rank < float(min(N_TOP, nsel)), 1.0, 0.0).astype(BF16)
    sel_ref[...] = jnp.dot(ex_ref[...], sel, preferred_element_type=F32)

    ik = lax.broadcasted_iota(jnp.int32, (KB, TQ), 0)
    iq = lax.broadcasted_iota(jnp.int32, (KB, TQ), 1)
    causal4 = rep(ik <= iq)
    upper4 = rep(ik > iq)
    bias_d0 = jnp.concatenate([d0_ref[r] for r in range(R_Q)], axis=1)
    bias_d1 = jnp.concatenate([d1_ref[r] for r in range(R_Q)], axis=1)
    bias_far = jnp.concatenate([jnp.full((KB, TQ), far_ref[g * R_Q + r], F32) for r in range(R_Q)], axis=1)

    nkb = t // KB

    def piece(kref, vref, kb, bias, mask, valid):
        kbc = jnp.clip(kb, 0, nkb - 1)
        off = pl.multiple_of(kbc * KB, KB)
        if valid is not None:
            bias = bias + jnp.where(valid, 0.0, NEG)
        return kref[0, 0, pl.ds(off, KB), :], vref[0, 0, :, pl.ds(off, KB)], bias, mask

    def sel_mask(kb):
        kbc = jnp.clip(kb, 0, nkb - 1)
        return rep(sel_ref[pl.ds(pl.multiple_of(kbc * KB, KB), KB), :] > 0.5)

    reset()
    _softmax_pieces(qst, [piece(ks_ref, vs_ref, qt - 1, bias_d1, sel_mask(qt - 1), qt >= 1),
                          piece(ks_ref, vs_ref, qt, bias_d0, sel_mask(qt) & causal4, None)], m_ref, l_ref, acc_ref)
    nfar = jnp.maximum(qt - 1, 0)

    def far_sel(gi, carry):
        pieces = []
        for j in range(FAR_GROUP):
            kb = gi * FAR_GROUP + j
            pieces.append(piece(ks_ref, vs_ref, kb, bias_far, sel_mask(kb), kb < nfar))
        _softmax_pieces(qst, pieces, m_ref, l_ref, acc_ref)
        return carry

    lax.fori_loop(0, (nfar + FAR_GROUP - 1) // FAR_GROUP, far_sel, 0)
    o_s = result()

    reset()
    nwin = WIN // KB
    pieces = [piece(kw_ref, vw_ref, qt - nwin, bias_far, upper4, qt >= nwin)]
    for j in range(nwin - 1, 1, -1):
        pieces.append(piece(kw_ref, vw_ref, qt - j, bias_far, None, qt >= j))
    pieces.append(piece(kw_ref, vw_ref, qt - 1, bias_d1, None, qt >= 1))
    pieces.append(piece(kw_ref, vw_ref, qt, bias_d0, causal4, None))
    _softmax_pieces(qst, pieces, m_ref, l_ref, acc_ref)
    o_w = result()

    gate = _sigmoid(gl_ref[...]).T
    outs = []
    for r in range(R_Q):
        cs = slice(r * TQ, (r + 1) * TQ)
        outs.append(gate[3 * r:3 * r + 1, :] * o_c[:, cs] + gate[3 * r + 1:3 * r + 2, :] * o_s[:, cs]
                    + gate[3 * r + 2:3 * r + 3, :] * o_w[:, cs])
    o = jnp.concatenate(outs, axis=0).T
    o_ref[...] = (o * _silu(zb_ref[...])).astype(o_ref.dtype)


def _nsa_prompt(hb, kvp, tabs, bsz, t):
    ks_t, vs_t, kw_t, vw_t, kc_t, vc_t = kvp
    nch = t // CMP_STRIDE
    nqt = t // TQ
    gw = R_Q * HD
    k_spec = lambda n: pl.BlockSpec((1, 1, n, HD), lambda b, g, i: (b, g, 0, 0))
    vt_spec = lambda n: pl.BlockSpec((1, 1, HD, n), lambda b, g, i: (b, g, 0, 0))
    gl0 = (D_B + 6 * KV_W) // GL_PAD
    zb0 = (D_B + 6 * KV_W + G_KV * GL_PAD) // gw
    return pl.pallas_call(
        functools.partial(_nsa_prompt_kernel, t=t, nch=nch),
        grid=(bsz, G_KV, nqt),
        in_specs=[pl.BlockSpec((TQ, gw), lambda b, g, i: (b * nqt + i, g)),
                  pl.BlockSpec((TQ, GL_PAD), lambda b, g, i: (b * nqt + i, gl0 + g)),
                  pl.BlockSpec((TQ, gw), lambda b, g, i: (b * nqt + i, zb0 + g)),
                  k_spec(nch), vt_spec(nch), k_spec(t), vt_spec(t), k_spec(t), vt_spec(t),
                  pl.BlockSpec((R_Q, nch, TQ), lambda b, g, i: (g, 0, i)),
                  pl.BlockSpec((R_Q, KB, TQ), lambda b, g, i: (g, 0, 0)),
                  pl.BlockSpec((R_Q, KB, TQ), lambda b, g, i: (g, 0, 0)),
                  pl.BlockSpec(memory_space=pltpu.SMEM),
                  pl.BlockSpec(tabs['overlap'].shape, lambda b, g, i: (0, 0)),
                  pl.BlockSpec(tabs['expand'].shape, lambda b, g, i: (0, 0))],
        out_specs=pl.BlockSpec((TQ, gw), lambda b, g, i: (b * nqt + i, g)),
        out_shape=jax.ShapeDtypeStruct((bsz * t, D_B), BF16),
        scratch_shapes=[pltpu.VMEM((1, R_Q * TQ), F32), pltpu.VMEM((1, R_Q * TQ), F32),
                        pltpu.VMEM((HD, R_Q * TQ), F32), pltpu.VMEM((t, TQ), F32)],
        compiler_params=_params(("parallel", "parallel", "arbitrary")),
    )(hb, hb, hb, kc_t, vc_t, ks_t, vs_t, kw_t, vw_t, tabs['bias_c'], tabs['d0'], tabs['d1'], tabs['far'],
      tabs['overlap'], tabs['expand'])


def _row_softmax(s, mask, s_new):
    sm = jnp.where(mask, s, NEG)
    m = jnp.maximum(jnp.max(sm, axis=-1, keepdims=True), s_new)
    p = jnp.where(mask, jnp.exp(sm - m), 0.0)
    p_new = jnp.exp(s_new - m)
    l = jnp.sum(p, axis=-1, keepdims=True) + p_new
    return p / l, p_new / l


def _nsa_sample_kernel(pt_ref, *refs, n_pages, past):
    page_refs = refs[:n_pages]
    (win_ref, qbd_ref, new_ref, gl_ref, zb_ref, w1_ref, b1_ref, w2_ref, b2_ref,
     bc_ref, bs_ref, bw_ref, b0_ref, ov_ref, gsum_ref, bdm_ref, ex_ref, o_ref, x_s, kc_s, vc_s) = refs[n_pages:]
    del pt_ref
    nch = past // CMP_STRIDE
    nsel = past // SEL_LEN + 1
    qbd = qbd_ref[0]
    qbd_f = qbd.astype(F32)
    npair = KV_W // LANES

    for pi, pr in enumerate(page_refs):
        for kv in range(2):
            for pair in range(npair):
                x_s[kv, pair, pi * PAGE_SIZE:(pi + 1) * PAGE_SIZE, :] = pr[0, 0, kv, pair * LANES:(pair + 1) * LANES, :].T
    for kv, dst in ((0, kc_s), (1, vc_s)):
        load = lambda tau, pair, kv=kv: x_s[kv, pair, pl.ds(tau, nch, stride=CMP_STRIDE), :]
        res = _compress_rows(load, w1_ref, b1_ref[kv], w2_ref[kv], b2_ref[kv], kv, nch)
        for g in range(G_KV):
            dst[:, g * HD:(g + 1) * HD] = res[g]
    ncol = lax.broadcasted_iota(jnp.int32, (H_B, nch), 1)
    s_c = _bdot(qbd, kc_s[...], NT) + bc_ref[...]
    mask_c = ncol < nch - 1
    sm = jnp.where(mask_c, s_c, NEG)
    p_c = jnp.where(mask_c, jnp.exp(sm - jnp.max(sm, axis=-1, keepdims=True)), 0.0)
    p_c = p_c / jnp.sum(p_c, axis=-1, keepdims=True)
    o_c = _bdot(p_c, vc_s[...])
    p_hi = p_c.astype(BF16)
    p_lo = (p_c - p_hi.astype(F32)).astype(BF16)
    ov = ov_ref[...]
    imp = jnp.dot(p_hi, ov, preferred_element_type=F32) + jnp.dot(p_lo, ov, preferred_element_type=F32)
    i_hi = imp.astype(BF16)
    i_lo = (imp - i_hi.astype(F32)).astype(BF16)
    gs = gsum_ref[...]
    imp = jnp.dot(gs, i_hi, preferred_element_type=F32) + jnp.dot(gs, i_lo, preferred_element_type=F32)
    nsp = imp.shape[1]
    blk = lax.broadcasted_iota(jnp.int32, (H_B, nsp), 1)
    cur = past // SEL_LEN
    imp = jnp.where((blk == 0) | (blk == cur) | (blk == cur - 1), -NEG, imp)
    imp = jnp.where(blk < nsel, imp, 2.0 * NEG)
    rank = jnp.zeros((H_B, nsp), F32)
    for s2 in range(nsel):
        other = imp[:, s2:s2 + 1]
        rank = rank + jnp.where((other > imp) | ((other == imp) & (blk > s2)), 1.0, 0.0)
    sel = jnp.where(rank < float(N_TOP), 1.0, 0.0).astype(BF16)
    mask_s = jnp.dot(sel, ex_ref[...], preferred_element_type=F32) > 0.5

    new = new_ref[0]
    ks_new = new[:, 2 * KV_W:3 * KV_W]
    vs_new = new[:, 3 * KV_W:4 * KV_W]
    kw_new = new[:, 4 * KV_W:5 * KV_W]
    vw_new = new[:, 5 * KV_W:6 * KV_W]
    b0 = b0_ref[...]
    s_s = jnp.concatenate([_bdot(qbd, pr[0, 0, 2]) for pr in page_refs], axis=1) + bs_ref[...]
    s_new = jnp.sum(qbd_f * ks_new, axis=-1, keepdims=True) + b0
    p_s, p_new = _row_softmax(s_s, mask_s, s_new)
    o_s = p_new * vs_new
    for pi, pr in enumerate(page_refs):
        o_s = o_s + _bdot(p_s[:, pi * PAGE_SIZE:(pi + 1) * PAGE_SIZE], pr[0, 0, 3], NT)
    nbuf = win_ref.shape[-1]
    wcol = lax.broadcasted_iota(jnp.int32, (H_B, nbuf), 1)
    s_w = _bdot(qbd, win_ref[0, 0, 0]) + bw_ref[...]
    s_wn = jnp.sum(qbd_f * kw_new, axis=-1, keepdims=True) + b0
    p_w, p_wn = _row_softmax(s_w, wcol >= nbuf + 1 - WIN, s_wn)
    o_w = _bdot(p_w, win_ref[0, 0, 1], NT) + p_wn * vw_new
    gate = _sigmoid(gl_ref[0])
    o = gate[:, 0:1] * o_c + gate[:, 1:2] * o_s + gate[:, 2:3] * o_w
    o = o * bdm_ref[...]
    o16 = o[:, 0:HD]
    for g in range(1, G_KV):
        o16 = o16 + o[:, g * HD:(g + 1) * HD]
    o_ref[0] = (o16 * _silu(zb_ref[0])).astype(o_ref.dtype)


def _nsa_sample(cache_l, l, page_table, win, qbd, new_rows, gl, zb, cp, tabs):
    bsz, n_pages = page_table.shape
    past = n_pages * PAGE_SIZE
    nch = past // CMP_STRIDE
    nbuf = win.shape[-1]
    full = lambda a: pl.BlockSpec(a.shape, lambda b, pt: (0,) * a.ndim)
    page_specs = [pl.BlockSpec((1, 1) + cache_l.shape[2:],
                               functools.partial(lambda b, pt, j: (l, pt[b, j], 0, 0, 0), j=j))
                  for j in range(n_pages)]
    consts = [cp['w1'], cp['b1'], cp['w2'], cp['b2'], tabs['bias_c'], tabs['bias_s'], tabs['bias_w'], tabs['bias_0'],
              tabs['overlap'], tabs['gsum'], tabs['bdmask'], tabs['expand']]
    grid_spec = pltpu.PrefetchScalarGridSpec(
        num_scalar_prefetch=1,
        grid=(bsz,),
        in_specs=page_specs + [
            pl.BlockSpec((1, 1, 2, KV_W, nbuf), lambda b, pt: (l, b, 0, 0, 0)),
            pl.BlockSpec((1, H_B, KV_W), lambda b, pt: (b, 0, 0)),
            pl.BlockSpec((1, 1, 6 * KV_W), lambda b, pt: (b, 0, 0)),
            pl.BlockSpec((1, H_B, 3), lambda b, pt: (b, 0, 0)),
            pl.BlockSpec((1, H_B, HD), lambda b, pt: (b, 0, 0)),
        ] + [full(a) for a in consts],
        out_specs=pl.BlockSpec((1, H_B, HD), lambda b, pt: (b, 0, 0)),
        scratch_shapes=[pltpu.VMEM((2, KV_W // LANES, past, LANES), F32), pltpu.VMEM((nch, KV_W), F32),
                        pltpu.VMEM((nch, KV_W), F32)],
    )
    return pl.pallas_call(
        functools.partial(_nsa_sample_kernel, n_pages=n_pages, past=past),
        grid_spec=grid_spec,
        out_shape=jax.ShapeDtypeStruct((bsz, H_B, HD), BF16),
        compiler_params=_params(("arbitrary",)),
    )(page_table, *([cache_l] * n_pages), win, qbd, new_rows, gl, zb, *consts)


def _t5_bucket(dist):
    n = jnp.maximum(dist, 0)
    max_exact = NUM_BUCKETS // 2
    nf = jnp.maximum(n, 1).astype(F32)
    large = max_exact + (jnp.log(nf / max_exact) / math.log(MAX_DIST / max_exact)
                         * (NUM_BUCKETS - max_exact)).astype(jnp.int32)
    large = jnp.minimum(large, NUM_BUCKETS - 1)
    return jnp.where(n < max_exact, n, large)


def _bias_of(rel_bias, dist):
    return jnp.moveaxis(rel_bias[_t5_bucket(dist)], -1, 0).astype(F32)


def _overlap(nch, nsel):
    ci = np.arange(nch)[:, None] * CMP_STRIDE
    sj = np.arange(nsel)[None, :] * SEL_LEN
    ov = ((ci < sj + SEL_LEN) & (ci + CMP_LEN > sj)).astype(np.float32)
    ov[nch - 1:, :] = 0.0
    return ov


def _skew(g, n, step, length):
    h, L = g.shape
    flat = jnp.tile(g, (1, n))[:, :n * (L - step)]
    return flat.reshape(h, n, L - step)[:, :, :length]


def _prompt_tables(rel_bias, t):
    nch = t // CMP_STRIDE
    nsel = t // SEL_LEN
    f = _bias_of(rel_bias, jnp.arange(max(t, 2 * KB), dtype=jnp.int32))
    f0 = lambda n: jnp.broadcast_to(f[:, :1], (H_B, n))
    lead = CMP_LEN - 1
    g_c = jnp.concatenate([f0(lead), f[:, :t - lead], f0(CMP_STRIDE * nch)], axis=1)
    g_0 = jnp.concatenate([f[:, :KB], f0(KB)], axis=1)
    g_1 = jnp.concatenate([f[:, KB:2 * KB], f[:, :KB]], axis=1)
    expand = (np.arange(t)[:, None] // SEL_LEN == np.arange(nsel)[None, :]).astype(np.float32)
    return {
        'bias_c': _skew(g_c, nch, CMP_STRIDE, t),
        'd0': _skew(g_0, KB, 1, TQ),
        'd1': _skew(g_1, KB, 1, TQ),
        'far': rel_bias[NUM_BUCKETS - 1].astype(F32),
        'overlap': jnp.asarray(_overlap(nch, nsel).T, BF16),
        'expand': jnp.asarray(expand, BF16),
    }


def _sample_tables(rel_bias, past, nbuf):
    nch = past // CMP_STRIDE
    nsel = past // SEL_LEN + 1
    nsp = -(-nsel // LANES) * LANES
    cmp_end = jnp.arange(nch, dtype=jnp.int32) * CMP_STRIDE + CMP_LEN - 1
    ov = np.zeros((nch, nsp), np.float32)
    ov[:, :nsel] = _overlap(nch, nsel)
    hh = np.arange(H_B)
    gsum = (hh[:, None] // R_Q == hh[None, :] // R_Q).astype(np.float32)
    bdm = (hh[:, None] // R_Q == np.arange(KV_W)[None, :] // HD).astype(np.float32)
    expand = (np.arange(nsp)[:, None] == np.arange(past)[None, :] // SEL_LEN).astype(np.float32)
    return {
        'bias_c': _bias_of(rel_bias, past - cmp_end),
        'bias_s': _bias_of(rel_bias, past - jnp.arange(past, dtype=jnp.int32)),
        'bias_w': _bias_of(rel_bias, nbuf - jnp.arange(nbuf, dtype=jnp.int32)),
        'bias_0': _bias_of(rel_bias, jnp.zeros((1,), jnp.int32)),
        'overlap': jnp.asarray(ov, BF16),
        'gsum': jnp.asarray(gsum, BF16),
        'bdmask': jnp.asarray(bdm, F32),
        'expand': jnp.asarray(expand, BF16),
    }


def _layer_params(l, w_in, mu_shift, rw_w0, rw_w2, rw_a0, rw_a2, rw_kk, rw_ka, rw_rk, rw_gn_g, rw_gn_b,
                  rw_v0, rw_v1, rw_v2, cmp_w1, cmp_b1, cmp_w2, cmp_b2, w_up_a, w_up_b, w_out, ln_g, ln_b):
    w = w_in[l]
    b0 = A_COLS
    q_kv = w[:, b0:b0 + D_B + 6 * KV_W]
    gl = w[:, b0 + D_B + 6 * KV_W:b0 + D_B + 6 * KV_W + 3 * H_B].reshape(D_MODEL, G_KV, 3 * R_Q)
    gl = jnp.pad(gl, ((0, 0), (0, 0), (0, GL_PAD - 3 * R_Q))).reshape(D_MODEL, G_KV * GL_PAD)
    zb = w[:, b0 + D_B + 6 * KV_W + 3 * H_B:b0 + D_B + 6 * KV_W + 3 * H_B + D_B]
    half = CMP_STRIDE * HD
    w1 = cmp_w1[l]
    w1r = jnp.concatenate([w1[:, :half].reshape(2, CMP_STRIDE, HD, CMP_HID),
                           w1[:, half:].reshape(2, CMP_STRIDE, HD, CMP_HID)], axis=-1)
    zero = jnp.zeros_like(w1r)
    w1r = jnp.stack([jnp.concatenate([w1r, zero], axis=2), jnp.concatenate([zero, w1r], axis=2)], axis=2)
    row = lambda a: a.reshape(1, -1).astype(F32)
    p = {
        'wa': w[:, :A_COLS].astype(BF16),
        'wb': jnp.concatenate([q_kv, gl, zb], axis=1).astype(BF16),
        'wg': w[:, b0 + D_B + 6 * KV_W + 3 * H_B + D_B:].astype(BF16),
        'wc': w[:, b0 + D_B:b0 + D_B + 2 * KV_W].astype(BF16),
        'mu': row(mu_shift[l]), 'w0': row(rw_w0[l]), 'w2': rw_w2[l].astype(BF16), 'a0': row(rw_a0[l]),
        'a2': rw_a2[l].astype(BF16), 'kk': row(rw_kk[l]), 'ka': row(rw_ka[l]), 'rk': row(rw_rk[l]),
        'gn_g': row(rw_gn_g[l]), 'gn_b': row(rw_gn_b[l]),
        'cmp': {'w1': w1r.astype(BF16), 'b1': cmp_b1[l].reshape(2, 1, CMP_HID).astype(F32),
                'w2': cmp_w2[l].astype(BF16), 'b2': cmp_b2[l].reshape(2, 1, HD).astype(F32)},
        'w_up_a': w_up_a[l].astype(BF16), 'w_up_b': w_up_b[l].astype(BF16), 'w_out': w_out[l].astype(BF16),
        'ln_g': row(ln_g[l]), 'ln_b': row(ln_b[l]),
    }
    if l > 0:
        p['v0'] = row(rw_v0[l - 1])
        p['v1'] = rw_v1[l - 1].astype(BF16)
        p['v2'] = rw_v2[l - 1].astype(BF16)
    return p


def _project(x2, p):
    m = x2.shape[0]
    tm = 512
    ha = _matmul(x2, p['wa'], tm, A_COLS // 3)
    hb = _matmul(x2, p['wb'], tm, 1024)
    hg = _matmul(x2, p['wg'], tm, 1024)
    return ha, hb, hg


def _finish(x2, o_a, o_b, hg, p):
    merged = _up_gate(o_a, o_b, p['w_up_a'], p['w_up_b'], hg)
    return _out_ln(merged, p['w_out'], x2, p['ln_g'], p['ln_b'])


def _prompt_layer(x2, vfirst, p, tabs, bsz, t):
    ha, hb, hg = _project(x2, p)
    o_a, vfirst, wkv = _rwkv_prompt(ha, vfirst, p, bsz, t)
    hc = _matmul(x2, p['wc'], 512, 2 * KV_W)
    kvp = _kv_prep(hb, hc, p['cmp'], bsz, t)
    o_b = _nsa_prompt(hb, kvp, tabs, bsz, t)
    y = _finish(x2, o_a, o_b, hg, p)
    kvc = D_B
    new_rows = hb[:, kvc:kvc + 4 * KV_W].reshape(bsz, t, 4, G_KV, HD)
    nwin = min(WIN, t)
    win_state = hb[:, kvc + 4 * KV_W:kvc + 6 * KV_W].reshape(bsz, t, 2, G_KV, HD)[:, t - nwin:]
    shift = ha.reshape(bsz, t, A_COLS)[:, t - 1]
    return y, vfirst, (new_rows, win_state, wkv, shift)


def _sample_layer(x2, vfirst, l, p, tabs, cache_l, win_l, cache_win_kv, state_wkv, state_shift, page_table):
    bsz = x2.shape[0]
    ha, hb, hg = _project(x2, p)
    o_a, vfirst, wkv = _rwkv_sample(ha, state_shift[l], state_wkv[l], vfirst, p)
    kvc = D_B
    q = hb[:, :D_B].reshape(bsz, G_KV, R_Q, HD) * SCALE
    eye = jnp.eye(G_KV, dtype=F32)
    qbd = (q[:, :, :, None, :] * eye[None, :, None, :, None]).reshape(bsz, H_B, KV_W).astype(BF16)
    new6 = hb[:, kvc:kvc + 6 * KV_W]
    gl0 = kvc + 6 * KV_W
    gl = hb[:, gl0:gl0 + G_KV * GL_PAD].reshape(bsz, G_KV, GL_PAD)[:, :, :3 * R_Q].reshape(bsz, H_B, 3)
    zb = hb[:, gl0 + G_KV * GL_PAD:].reshape(bsz, H_B, HD)
    nbuf = cache_win_kv.shape[2]
    o_b = _nsa_sample(cache_l, l, page_table, win_l, qbd, new6[:, None, :], gl, zb, p['cmp'], tabs)
    y = _finish(x2, o_a, o_b.reshape(bsz, D_B), hg, p)
    new_rows = new6[:, :4 * KV_W].reshape(bsz, 1, 4, G_KV, HD)
    new_win = new6[:, 4 * KV_W:].reshape(bsz, 1, 2, G_KV, HD)
    win_state = jnp.concatenate([cache_win_kv[l], new_win], axis=1)[:, -nbuf:]
    return y, vfirst, (new_rows, win_state, wkv, ha)


def kernel(x_prompt, x_sample, cache_kv, cache_win_kv, state_wkv, state_shift, page_table, w_in, mu_shift, rw_w0, rw_w2, rw_a0, rw_a2, rw_kk, rw_ka, rw_rk, rw_gn_g, rw_gn_b, rw_v0, rw_v1, rw_v2, cmp_w1, cmp_b1, cmp_w2, cmp_b2, rel_bias, w_up_a, w_up_b, w_out, ln_g, ln_b):
    bsz, t, _ = x_prompt.shape
    dec_b = x_sample.shape[0]
    n_pages = page_table.shape[1]
    depth, n_phys = cache_kv.shape[:2]
    cache_l = jnp.transpose(cache_kv, (0, 1, 3, 4, 5, 2)).reshape(depth, n_phys, 4, KV_W, PAGE_SIZE)
    win_l = jnp.transpose(cache_win_kv, (0, 1, 3, 4, 5, 2)).reshape(depth, dec_b, 2, KV_W, cache_win_kv.shape[2])
    tabs_p = _prompt_tables(rel_bias, t)
    tabs_s = _sample_tables(rel_bias, n_pages * PAGE_SIZE, cache_win_kv.shape[2])
    y_p = x_prompt.reshape(bsz * t, D_MODEL)
    y_s = x_sample.reshape(dec_b, D_MODEL)
    vf_p, vf_s = None, None
    st_p, st_s = [], []
    for l in range(depth):
        p = _layer_params(l, w_in, mu_shift, rw_w0, rw_w2, rw_a0, rw_a2, rw_kk, rw_ka, rw_rk, rw_gn_g, rw_gn_b,
                          rw_v0, rw_v1, rw_v2, cmp_w1, cmp_b1, cmp_w2, cmp_b2, w_up_a, w_up_b, w_out, ln_g, ln_b)
        y_p, vf_p, sp = _prompt_layer(y_p, vf_p, p, tabs_p, bsz, t)
        y_s, vf_s, ss = _sample_layer(y_s, vf_s, l, p, tabs_s, cache_l, win_l, cache_win_kv, state_wkv, state_shift,
                                      page_table)
        st_p.append(sp)
        st_s.append(ss)
    stack = lambda st, i: jnp.stack([s[i] for s in st])
    return (y_p.reshape(bsz, t, D_MODEL), y_s.reshape(dec_b, 1, D_MODEL),
            stack(st_p, 0), stack(st_p, 1), stack(st_p, 2), stack(st_p, 3),
            stack(st_s, 0), stack(st_s, 1), stack(st_s, 2), stack(st_s, 3))
```

```python
import functools
import math

import numpy as np
import jax
import jax.numpy as jnp
from jax import lax
from jax.experimental import pallas as pl
from jax.experimental.pallas import tpu as pltpu

D_MODEL = 2048
DEPTH = 2
PAGE_SIZE = 128
HS = 64
D_A = D_MODEL // 2
H_A = D_A // HS
R_W = 64
R_A = 64
R_V = 32
GN_EPS = 64e-5
HD = 64
D_B = D_MODEL // 2
H_B = D_B // HD
G_KV = 4
R_Q = H_B // G_KV
KV_W = G_KV * HD
CMP_LEN = 32
CMP_STRIDE = 16
CMP_HID = 128
SEL_LEN = 64
N_TOP = 16
WIN = 512
NUM_BUCKETS = 32
MAX_DIST = 128
SCALE = HD ** -0.5
A_COLS = 4 * D_A + R_W + R_A
ALPHA = (2 * DEPTH) ** 0.25
LN_EPS = 1e-5
NEG = -1e30

F32 = jnp.float32
BF16 = jnp.bfloat16

LANES = 128
VMEM_LIMIT = 56 * 1024 * 1024
CHUNK = 64
RWKV_ROWS = 1
UNIT_GROUP = 16
TQ = 128
KB = 128
FAR_GROUP = 4
PROJ_TM = 1024
PROJ_TN = 1024
UP_TM = 512
LN_TM = 512
GL_PAD = LANES
HB_COLS = D_B + 6 * KV_W + G_KV * GL_PAD + D_B

NT = (((1,), (1,)), ((), ()))
TN = (((0,), (0,)), ((), ()))


def _params(sem):
    return pltpu.CompilerParams(dimension_semantics=sem, vmem_limit_bytes=VMEM_LIMIT)


def _bdot(a, b, dims=None):
    a = a.astype(BF16)
    b = b.astype(BF16)
    if dims is None:
        return jnp.dot(a, b, preferred_element_type=F32)
    return lax.dot_general(a, b, dims, preferred_element_type=F32)


def _sigmoid(x):
    return 1.0 / (1.0 + jnp.exp(-x))


def _silu(x):
    return x * _sigmoid(x)


def _gelu_tanh(x):
    return 0.5 * x * (1.0 + jnp.tanh(math.sqrt(2.0 / math.pi) * (x + 0.044715 * (x * x * x))))


def _mm_kernel(x_ref, w_ref, o_ref):
    o_ref[...] = _bdot(x_ref[...], w_ref[...])


def _matmul(x, w, tm, tn):
    m, k = x.shape
    n = w.shape[1]
    tm = min(tm, m)
    return pl.pallas_call(
        _mm_kernel,
        grid=(m // tm, n // tn),
        in_specs=[pl.BlockSpec((tm, k), lambda i, j: (i, 0)),
                  pl.BlockSpec((k, tn), lambda i, j: (0, j))],
        out_specs=pl.BlockSpec((tm, tn), lambda i, j: (i, j)),
        out_shape=jax.ShapeDtypeStruct((m, n), F32),
        compiler_params=_params(("parallel", "parallel")),
    )(x, w)


def _up_kernel(oa_ref, ob_ref, wa_ref, wb_ref, ga_ref, gb_ref, o_ref):
    ua = _bdot(oa_ref[...], wa_ref[...])
    ub = _bdot(ob_ref[...], wb_ref[...])
    o_ref[...] = (_sigmoid(ga_ref[...]) * ua + _sigmoid(gb_ref[...]) * ub).astype(o_ref.dtype)


def _up_gate(o_a, o_b, w_up_a, w_up_b, hg, tm=UP_TM, tn=D_MODEL):
    m = o_a.shape[0]
    tm = min(tm, m)
    nb = D_MODEL // tn
    return pl.pallas_call(
        _up_kernel,
        grid=(m // tm, nb),
        in_specs=[pl.BlockSpec((tm, D_A), lambda i, j: (i, 0)),
                  pl.BlockSpec((tm, D_B), lambda i, j: (i, 0)),
                  pl.BlockSpec((D_A, tn), lambda i, j: (0, j)),
                  pl.BlockSpec((D_B, tn), lambda i, j: (0, j)),
                  pl.BlockSpec((tm, tn), lambda i, j: (i, j)),
                  pl.BlockSpec((tm, tn), lambda i, j: (i, j + nb))],
        out_specs=pl.BlockSpec((tm, tn), lambda i, j: (i, j)),
        out_shape=jax.ShapeDtypeStruct((m, D_MODEL), BF16),
        compiler_params=_params(("parallel", "parallel")),
    )(o_a, o_b, w_up_a, w_up_b, hg, hg)


def _out_ln_kernel(m_ref, w_ref, x_ref, g_ref, b_ref, o_ref):
    u = ALPHA * x_ref[...] + _bdot(m_ref[...], w_ref[...])
    mu = jnp.mean(u, axis=-1, keepdims=True)
    d = u - mu
    var = jnp.mean(d * d, axis=-1, keepdims=True)
    o_ref[...] = d * lax.rsqrt(var + LN_EPS) * g_ref[...] + b_ref[...]


def _out_ln(merged, w_out, x, ln_g, ln_b, tm=LN_TM):
    m = x.shape[0]
    tm = min(tm, m)
    return pl.pallas_call(
        _out_ln_kernel,
        grid=(m // tm,),
        in_specs=[pl.BlockSpec((tm, D_MODEL), lambda i: (i, 0)),
                  pl.BlockSpec((D_MODEL, D_MODEL), lambda i: (0, 0)),
                  pl.BlockSpec((tm, D_MODEL), lambda i: (i, 0)),
                  pl.BlockSpec((1, D_MODEL), lambda i: (0, 0)),
                  pl.BlockSpec((1, D_MODEL), lambda i: (0, 0))],
        out_specs=pl.BlockSpec((tm, D_MODEL), lambda i: (i, 0)),
        out_shape=jax.ShapeDtypeStruct((m, D_MODEL), F32),
        compiler_params=_params(("parallel",)),
    )(merged, w_out, x, ln_g, ln_b)


def _rwkv_premix(xm, vfirst, w0, w2, a0, a2, kkp, ka, vgate):
    r = xm[:, 0:D_A]
    k = xm[:, D_A:2 * D_A]
    v = xm[:, 2 * D_A:3 * D_A]
    w_lo = xm[:, 3 * D_A:3 * D_A + R_W]
    a_lo = xm[:, 3 * D_A + R_W:3 * D_A + R_W + R_A]
    z = xm[:, 3 * D_A + R_W + R_A:A_COLS]
    t = w0 + _bdot(jnp.tanh(w_lo), w2)
    lw = -math.exp(-0.5) * _sigmoid(t)
    if vgate is not None:
        v0, v1, v2 = vgate
        vg = _sigmoid(v0 + _bdot(_bdot(v, v1), v2))
        v = v + (vfirst - v) * vg
    lr = _sigmoid(a0 + _bdot(a_lo, a2))
    kkr = k * kkp
    k2 = k * (1.0 + (lr - 1.0) * ka)
    return r, lw, k2, v, lr, kkr, z


def _head_post(y, r, k2, v, z, rk, gn_g, gn_b):
    mu = jnp.mean(y, axis=-1, keepdims=True)
    d = y - mu
    var = jnp.mean(d * d, axis=-1, keepdims=True)
    yn = d * lax.rsqrt(var + GN_EPS) * gn_g + gn_b
    bonus = jnp.sum(r * k2 * rk, axis=-1, keepdims=True) * v
    return (yn + bonus) * _silu(z)


def _rwkv_prompt_kernel(*refs, has_vgate):
    if has_vgate:
        (fa_ref, vf_ref, mu_ref, w0_ref, w2_ref, a0_ref, a2_ref, kkp_ref, ka_ref, rk_ref, gg_ref, gb_ref,
         v0_ref, v1_ref, v2_ref, o_ref, s_out_ref, s_ref, last_ref) = refs
    else:
        (fa_ref, mu_ref, w0_ref, w2_ref, a0_ref, a2_ref, kkp_ref, ka_ref, rk_ref, gg_ref, gb_ref,
         o_ref, vf_out_ref, s_out_ref, s_ref, last_ref) = refs
    c = pl.program_id(1)
    nc = pl.num_programs(1)
    C = CHUNK

    @pl.when(c == 0)
    def _():
        s_ref[...] = jnp.zeros_like(s_ref)
        last_ref[...] = jnp.zeros_like(last_ref)

    ri = lax.broadcasted_iota(jnp.int32, (C, C), 0)
    ci = lax.broadcasted_iota(jnp.int32, (C, C), 1)
    tri_i = ri >= ci
    tri_s = ri > ci
    tri_b = jnp.where(tri_i, 1.0, 0.0).astype(BF16)
    eye = jnp.where(ri == ci, 1.0, 0.0).astype(F32)
    row = lax.broadcasted_iota(jnp.int32, (C, 1), 0)
    rk = rk_ref[...]
    gg = gg_ref[...]
    gb = gb_ref[...]
    nb = fa_ref.shape[0]

    pre = []
    for bi in range(nb):
        x = fa_ref[bi]
        prev = jnp.where(row == 0, last_ref[bi, 0:1, :], pltpu.roll(x, 1, axis=0))
        last_ref[bi, 0:1, :] = x[C - 1:C, :]
        xm = x + (prev - x) * mu_ref[...]
        if has_vgate:
            vgate = (v0_ref[...], v1_ref[...], v2_ref[...])
            vfirst = vf_ref[bi]
        else:
            vgate, vfirst = None, None
        r, lw, k2, v, lr, kkr, z = _rwkv_premix(xm, vfirst, w0_ref[...], w2_ref[...], a0_ref[...], a2_ref[...],
                                                kkp_ref[...], ka_ref[...], vgate)
        if not has_vgate:
            vf_out_ref[bi] = v
        lw_hi = lw.astype(BF16)
        lw_lo = (lw - lw_hi.astype(F32)).astype(BF16)
        L = jnp.dot(tri_b, lw_hi, preferred_element_type=F32) + jnp.dot(tri_b, lw_lo, preferred_element_type=F32)
        LC = L[C - 1:C, :]
        pre.append(dict(r=r, k2=k2, v=v, lr=lr, kkr=kkr, z=z, e_in=jnp.exp(L), e_ex=jnp.exp(L - lw),
                        e_neg=jnp.exp(-L), e_rem=jnp.exp(LC - L), pc=jnp.exp(LC)))

    def run(units):
        heads = range(len(units))
        sls = [slice(h * HS, (h + 1) * HS) for _, h in units]
        col = lambda name: [pre[bi][name][:, sls[u]] for u, (bi, _) in enumerate(units)]
        bf = lambda xs: [x.astype(BF16) for x in xs]
        kk = []
        for u in col('kkr'):
            kk.append(u * lax.rsqrt(jnp.maximum(jnp.sum(u * u, axis=-1, keepdims=True), 1e-24)))
        r_h, k_h, v_h, lr_h, z_h = col('r'), col('k2'), col('v'), col('lr'), col('z')
        e_in, e_ex, e_neg, e_rem, pc = col('e_in'), col('e_ex'), col('e_neg'), col('e_rem'), col('pc')
        v_b = bf(v_h)
        b_h = [kk[h] * lr_h[h] for h in heads]
        at = [(-kk[h]) * e_ex[h] for h in heads]
        rt = [r_h[h] * e_in[h] for h in heads]
        bt = [b_h[h] * e_neg[h] for h in heads]
        kt = [k_h[h] * e_neg[h] for h in heads]
        bh = bf([b_h[h] * e_rem[h] for h in heads])
        kh = bf([k_h[h] * e_rem[h] for h in heads])
        g = [_bdot(jnp.concatenate([at[h], rt[h]], axis=0), jnp.concatenate([bt[h], kt[h]], axis=0), NT)
             for h in heads]
        a_ab = bf([jnp.where(tri_s, x[:C, :C], 0.0) for x in g])
        a_ak = bf([jnp.where(tri_s, x[:C, C:], 0.0) for x in g])
        a_rb = bf([jnp.where(tri_i, x[C:, :C], 0.0) for x in g])
        a_rk = bf([jnp.where(tri_i, x[C:, C:], 0.0) for x in g])
        tm = [eye + x.astype(F32) for x in a_ab]
        ap = bf([_bdot(x, x) for x in a_ab])
        n = 2
        while n < C:
            tm_next = [tm[h] + _bdot(tm[h], ap[h]) for h in heads]
            if 2 * n < C:
                ap = bf([_bdot(x, x) for x in ap])
            tm = tm_next
            n *= 2
        akv = [_bdot(a_ak[h], v_b[h]) for h in heads]
        wu = bf([_bdot(tm[h], jnp.concatenate([at[h], akv[h]], axis=1)) for h in heads])
        arw = [_bdot(a_rb[h], wu[h]) for h in heads]
        yh = [arw[h][:, HS:] + _bdot(a_rk[h], v_b[h]) for h in heads]
        s_old = [s_ref[bi, h] for bi, h in units]
        s_b = bf(s_old)
        y = [_bdot(rt[h] + arw[h][:, :HS], s_b[h], NT) + yh[h] for h in heads]
        bw = [_bdot(bh[h], wu[h][:, :HS], TN) for h in heads]
        nt = [_bdot(jnp.concatenate([wu[h][:, HS:], v_b[h]], axis=0), jnp.concatenate([bh[h], kh[h]], axis=0), TN)
              for h in heads]
        for u, (bi, h) in enumerate(units):
            s_ref[bi, h] = s_old[u] * pc[u] + _bdot(s_b[u], bw[u], NT) + nt[u]
        for u, (bi, h) in enumerate(units):
            sl = sls[u]
            o_ref[bi, :, sl] = _head_post(y[u], r_h[u], k_h[u], v_h[u], z_h[u], rk[:, sl], gg[:, sl],
                                          gb[:, sl]).astype(o_ref.dtype)

    all_units = [(bi, h) for bi in range(nb) for h in range(H_A)]
    for g0 in range(0, len(all_units), UNIT_GROUP):
        run(all_units[g0:g0 + UNIT_GROUP])

    @pl.when(c == nc - 1)
    def _():
        s_out_ref[...] = s_ref[...]


def _rwkv_prompt(ha, vfirst, p, bsz, t):
    has_vgate = vfirst is not None
    nc = t // CHUNK
    nb = RWKV_ROWS if bsz % RWKV_ROWS == 0 else 1
    row_spec = lambda w: pl.BlockSpec((nb, CHUNK, w), lambda b, c: (b, c, 0))
    full = lambda a: pl.BlockSpec(a.shape, lambda b, c: (0,) * a.ndim)
    ins = [ha.reshape(bsz, t, A_COLS)]
    in_specs = [row_spec(A_COLS)]
    if has_vgate:
        ins.append(vfirst.reshape(bsz, t, D_A))
        in_specs.append(row_spec(D_A))
    names = ['mu', 'w0', 'w2', 'a0', 'a2', 'kk', 'ka', 'rk', 'gn_g', 'gn_b'] + (['v0', 'v1', 'v2'] if has_vgate else [])
    for nme in names:
        ins.append(p[nme])
        in_specs.append(full(p[nme]))
    out_shape = [jax.ShapeDtypeStruct((bsz, t, D_A), BF16)]
    out_specs = [row_spec(D_A)]
    if not has_vgate:
        out_shape.append(jax.ShapeDtypeStruct((bsz, t, D_A), F32))
        out_specs.append(row_spec(D_A))
    out_shape.append(jax.ShapeDtypeStruct((bsz, H_A, HS, HS), F32))
    out_specs.append(pl.BlockSpec((nb, H_A, HS, HS), lambda b, c: (b, 0, 0, 0)))
    outs = pl.pallas_call(
        functools.partial(_rwkv_prompt_kernel, has_vgate=has_vgate),
        grid=(bsz // nb, nc),
        in_specs=in_specs,
        out_specs=out_specs,
        out_shape=out_shape,
        scratch_shapes=[pltpu.VMEM((nb, H_A, HS, HS), F32), pltpu.VMEM((nb, 8, A_COLS), F32)],
        compiler_params=_params(("parallel", "arbitrary")),
    )(*ins)
    o_a = outs[0].reshape(bsz * t, D_A)
    if has_vgate:
        return o_a, vfirst, outs[1]
    return o_a, outs[1].reshape(bsz * t, D_A), outs[2]


def _rwkv_sample_kernel(*refs, has_vgate, bt):
    if has_vgate:
        (fa_ref, prev_ref, s_in_ref, vf_ref, mu_ref, w0_ref, w2_ref, a0_ref, a2_ref, kkp_ref, ka_ref, rk_ref,
         gg_ref, gb_ref, v0_ref, v1_ref, v2_ref, o_ref, s_out_ref, ops_ref, y_ref) = refs
    else:
        (fa_ref, prev_ref, s_in_ref, mu_ref, w0_ref, w2_ref, a0_ref, a2_ref, kkp_ref, ka_ref, rk_ref,
         gg_ref, gb_ref, o_ref, vf_out_ref, s_out_ref, ops_ref, y_ref) = refs
    x = fa_ref[...]
    xm = x + (prev_ref[...] - x) * mu_ref[...]
    if has_vgate:
        vgate = (v0_ref[...], v1_ref[...], v2_ref[...])
        vfirst = vf_ref[...]
    else:
        vgate, vfirst = None, None
    r, lw, k2, v, lr, kkr, z = _rwkv_premix(xm, vfirst, w0_ref[...], w2_ref[...], a0_ref[...], a2_ref[...],
                                            kkp_ref[...], ka_ref[...], vgate)
    if not has_vgate:
        vf_out_ref[...] = v
    w = jnp.exp(lw)
    for h in range(H_A):
        sl = slice(h * HS, (h + 1) * HS)
        kk = kkr[:, sl]
        kk = kk * lax.rsqrt(jnp.maximum(jnp.sum(kk * kk, axis=-1, keepdims=True), 1e-24))
        ops_ref[0, :, sl] = -kk
        ops_ref[1, :, sl] = kk * lr[:, sl]
    ops_ref[2] = w
    ops_ref[3] = k2
    ops_ref[4] = v
    ops_ref[5] = r
    ri = lax.broadcasted_iota(jnp.int32, (HS, HS), 0)
    ci = lax.broadcasted_iota(jnp.int32, (HS, HS), 1)
    eye = jnp.where(ri == ci, 1.0, 0.0).astype(F32)

    heads = range(H_A)
    sls = [slice(h * HS, (h + 1) * HS) for h in heads]
    eye_b = eye.astype(BF16)
    rows_of = lambda x: jnp.broadcast_to(x, (HS, HS)).astype(BF16)
    for b in range(bt):
        row = lambda i: [ops_ref[i, b:b + 1, sl] for sl in sls]
        a_row, b_row, w_row, k_row, v_row, r_row = (row(i) for i in range(6))
        s = [s_in_ref[b, h] for h in heads]
        sa = [_bdot(s[h], rows_of(a_row[h]), NT) for h in heads]
        v_hi = [v_row[h].astype(BF16) for h in heads]
        v_lo = [(v_row[h] - v_hi[h].astype(F32)).astype(BF16) for h in heads]
        v_bc = [_bdot(eye_b, rows_of(v_hi[h]), NT) + _bdot(eye_b, rows_of(v_lo[h]), NT) for h in heads]
        s_new = [s[h] * w_row[h] + sa[h] * b_row[h] + v_bc[h] * k_row[h] for h in heads]
        for h in heads:
            s_out_ref[b, h] = s_new[h]
        y_bc = [_bdot(s_new[h], rows_of(r_row[h]), NT) for h in heads]
        for h in heads:
            y_ref[b:b + 1, sls[h]] = jnp.sum(eye * y_bc[h], axis=0, keepdims=True)
    y = y_ref[...]
    rk = rk_ref[...]
    gg = gg_ref[...]
    gb = gb_ref[...]
    for h in range(H_A):
        sl = slice(h * HS, (h + 1) * HS)
        o_ref[:, sl] = _head_post(y[:, sl], r[:, sl], k2[:, sl], v[:, sl], z[:, sl], rk[:, sl], gg[:, sl],
                                  gb[:, sl]).astype(o_ref.dtype)


def _rwkv_sample(ha, prev, s_in, vfirst, p, bt=8):
    bsz = ha.shape[0]
    has_vgate = vfirst is not None
    row_spec = lambda w: pl.BlockSpec((bt, w), lambda i: (i, 0))
    full = lambda a: pl.BlockSpec(a.shape, lambda i: (0,) * a.ndim)
    st_spec = pl.BlockSpec((bt, H_A, HS, HS), lambda i: (i, 0, 0, 0))
    ins = [ha, prev, s_in]
    in_specs = [row_spec(A_COLS), row_spec(A_COLS), st_spec]
    if has_vgate:
        ins.append(vfirst)
        in_specs.append(row_spec(D_A))
    names = ['mu', 'w0', 'w2', 'a0', 'a2', 'kk', 'ka', 'rk', 'gn_g', 'gn_b'] + (['v0', 'v1', 'v2'] if has_vgate else [])
    for nme in names:
        ins.append(p[nme])
        in_specs.append(full(p[nme]))
    out_shape = [jax.ShapeDtypeStruct((bsz, D_A), BF16)]
    out_specs = [row_spec(D_A)]
    if not has_vgate:
        out_shape.append(jax.ShapeDtypeStruct((bsz, D_A), F32))
        out_specs.append(row_spec(D_A))
    out_shape.append(jax.ShapeDtypeStruct((bsz, H_A, HS, HS), F32))
    out_specs.append(st_spec)
    outs = pl.pallas_call(
        functools.partial(_rwkv_sample_kernel, has_vgate=has_vgate, bt=bt),
        grid=(bsz // bt,),
        in_specs=in_specs,
        out_specs=out_specs,
        out_shape=out_shape,
        scratch_shapes=[pltpu.VMEM((6, bt, D_A), F32), pltpu.VMEM((bt, D_A), F32)],
        compiler_params=_params(("parallel",)),
    )(*ins)
    if has_vgate:
        return outs[0], vfirst, outs[1]
    return outs[0], outs[1], outs[2]


def _compress_rows(load_rows, w1_ref, b1, w2, b2, kv, nch):
    accs = [jnp.zeros((nch, 2 * CMP_HID), F32) for _ in range(G_KV)]
    for tau in range(CMP_STRIDE):
        for pair in range(G_KV // 2):
            rows = load_rows(tau, pair).astype(BF16)
            for parity in range(2):
                g = 2 * pair + parity
                accs[g] = accs[g] + jnp.dot(rows, w1_ref[kv, tau, parity], preferred_element_type=F32)
    outs = []
    for acc in accs:
        h = acc[:, :CMP_HID] + pltpu.roll(acc[:, CMP_HID:], nch - 1, axis=0) + b1
        outs.append(_bdot(_gelu_tanh(h), w2) + b2)
    return outs


def _kv_prep_kernel(kc_ref, ks_ref, kw_ref, w1_ref, b1_ref, w2_ref, b2_ref,
                    ks_o, vs_o, kw_o, vw_o, kc_o, vc_o, *, nch):
    npair = KV_W // LANES

    def put_transposed(out, pair, x):
        xt = x.T
        out[0, 2 * pair] = xt[0:HD].astype(BF16)
        out[0, 2 * pair + 1] = xt[HD:2 * HD].astype(BF16)

    for g in range(G_KV):
        sl = slice(g * HD, (g + 1) * HD)
        ks_o[0, g] = ks_ref[:, sl].astype(BF16)
        kw_o[0, g] = kw_ref[:, sl].astype(BF16)
    for pair in range(npair):
        sl2 = slice(KV_W + pair * LANES, KV_W + (pair + 1) * LANES)
        put_transposed(vs_o, pair, ks_ref[:, sl2])
        put_transposed(vw_o, pair, kw_ref[:, sl2])
    for kv in (0, 1):
        load = lambda tau, pair, kv=kv: kc_ref[pl.ds(2 * npair * tau + kv * npair + pair, nch,
                                                     stride=2 * npair * CMP_STRIDE), :]
        res = _compress_rows(load, w1_ref, b1_ref[kv], w2_ref[kv], b2_ref[kv], kv, nch)
        if kv == 0:
            for g in range(G_KV):
                kc_o[0, g] = res[g].astype(BF16)
        else:
            for pair in range(npair):
                put_transposed(vc_o, pair, jnp.concatenate([res[2 * pair], res[2 * pair + 1]], axis=1))


def _kv_prep(hb, hc, cp, bsz, t):
    nch = t // CMP_STRIDE
    blk = lambda j: pl.BlockSpec((t, 2 * KV_W), lambda b: (b, j))
    full = lambda a: pl.BlockSpec(a.shape, lambda b: (0,) * a.ndim)
    c0 = D_B // (2 * KV_W)
    lane_rows = 2 * KV_W // LANES
    hc = hc.reshape(bsz * t * lane_rows, LANES)
    def arr(n, transposed):
        shp = (G_KV, HD, n) if transposed else (G_KV, n, HD)
        return jax.ShapeDtypeStruct((bsz,) + shp, BF16), pl.BlockSpec((1,) + shp, lambda b: (b, 0, 0, 0))

    outs = [arr(t, False), arr(t, True), arr(t, False), arr(t, True), arr(nch, False), arr(nch, True)]
    return pl.pallas_call(
        functools.partial(_kv_prep_kernel, nch=nch),
        grid=(bsz,),
        in_specs=[pl.BlockSpec((t * lane_rows, LANES), lambda b: (b, 0)), blk(c0 + 1), blk(c0 + 2),
                  full(cp['w1']), full(cp['b1']), full(cp['w2']), full(cp['b2'])],
        out_specs=[o[1] for o in outs],
        out_shape=[o[0] for o in outs],
        compiler_params=_params(("parallel",)),
    )(hc, hb, hb, cp['w1'], cp['b1'], cp['w2'], cp['b2'])


def _softmax_pieces(qst, pieces, state):
    ss = []
    for k, vt, bias, mask in pieces:
        s = jnp.dot(k, qst, preferred_element_type=F32) + bias
        ss.append(s if mask is None else jnp.where(mask, s, NEG))
    m_new = jnp.max(ss[0], axis=0, keepdims=True)
    for s in ss[1:]:
        m_new = jnp.maximum(m_new, jnp.max(s, axis=0, keepdims=True))
    if state is not None:
        m_new = jnp.maximum(m_new, state[0])
    ps = [jnp.exp(s - m_new) for s in ss]
    l_new = jnp.sum(ps[0], axis=0, keepdims=True)
    for p in ps[1:]:
        l_new = l_new + jnp.sum(p, axis=0, keepdims=True)
    p_all = jnp.concatenate([p.astype(BF16) for p in ps], axis=0)
    vt_all = jnp.concatenate([vt for _, vt, _, _ in pieces], axis=1)
    acc_new = jnp.dot(vt_all, p_all, preferred_element_type=F32)
    if state is not None:
        alpha = jnp.exp(state[0] - m_new)
        l_new = l_new + alpha * state[1]
        acc_new = acc_new + alpha * state[2]
    return m_new, l_new, acc_new


def _nsa_prompt_kernel(q_ref, gl_ref, zb_ref, kc_ref, vc_ref, ks_ref, vs_ref, kw_ref, vw_ref,
                       bc_ref, d0_ref, d1_ref, far_ref, ov_ref, ex_ref, o_ref, sel_ref, *, t, nch):
    g = pl.program_id(1)
    qt = pl.program_id(2)
    nsel = t // SEL_LEN
    nkb = t // KB
    nwin = WIN // KB
    qt_t = (q_ref[...] * SCALE).T
    qst = jnp.concatenate([qt_t[r * HD:(r + 1) * HD, :] for r in range(R_Q)], axis=1).astype(BF16)
    qpos = qt * TQ + lax.broadcasted_iota(jnp.int32, (1, TQ), 1)
    rep = lambda x: jnp.concatenate([x] * R_Q, axis=1)

    ik = lax.broadcasted_iota(jnp.int32, (KB, TQ), 0)
    iq = lax.broadcasted_iota(jnp.int32, (KB, TQ), 1)
    causal = ik <= iq
    bias_d0 = jnp.concatenate([d0_ref[r] for r in range(R_Q)], axis=1)
    bias_d1 = jnp.concatenate([d1_ref[r] for r in range(R_Q)], axis=1)
    far_row = jnp.concatenate([jnp.full((1, TQ), far_ref[g * R_Q + r], F32) for r in range(R_Q)], axis=1)
    penalty = lambda valid: jnp.where(valid, 0.0, NEG)

    def kv(kref, vref, kb):
        off = pl.multiple_of(jnp.clip(kb, 0, nkb - 1) * KB, KB)
        return kref[0, 0, pl.ds(off, KB), :], vref[0, 0, :, pl.ds(off, KB)]

    pieces = [kv(kw_ref, vw_ref, qt) + (bias_d0, rep(causal)),
              kv(kw_ref, vw_ref, qt - 1) + (bias_d1 + penalty(qt >= 1), None)]
    for j in range(2, nwin):
        pieces.append(kv(kw_ref, vw_ref, qt - j) + (far_row + penalty(qt >= j), None))
    pieces.append(kv(kw_ref, vw_ref, qt - nwin) + (far_row + penalty(qt >= nwin), rep(ik > iq)))
    _, l_w, acc_w = _softmax_pieces(qst, pieces, None)
    o_w = acc_w / l_w

    nrow = lax.broadcasted_iota(jnp.int32, (nch, TQ), 0)
    mask_c = (qpos >= nrow * CMP_STRIDE + (CMP_LEN - 1)) & (nrow < nch - 1)
    mask_c4 = rep(mask_c)
    bias_c = jnp.concatenate([bc_ref[r] for r in range(R_Q)], axis=1)
    s = jnp.dot(kc_ref[0, 0], qst, preferred_element_type=F32) + bias_c
    s = jnp.where(mask_c4, s, NEG)
    p = jnp.where(mask_c4, jnp.exp(s - jnp.max(s, axis=0, keepdims=True)), 0.0)
    l = jnp.sum(p, axis=0, keepdims=True)
    p = p / jnp.where(l > 0.0, l, 1.0)
    o_c = jnp.dot(vc_ref[0, 0], p.astype(BF16), preferred_element_type=F32)
    psum = p[:, 0:TQ]
    for r in range(1, R_Q):
        psum = psum + p[:, r * TQ:(r + 1) * TQ]
    p_hi = psum.astype(BF16)
    p_lo = (psum - p_hi.astype(F32)).astype(BF16)
    ov = ov_ref[...]
    imp = jnp.dot(ov, p_hi, preferred_element_type=F32) + jnp.dot(ov, p_lo, preferred_element_type=F32)
    blk = lax.broadcasted_iota(jnp.int32, (nsel, TQ), 0)
    cur = qpos // SEL_LEN
    imp = jnp.where(blk * SEL_LEN <= qpos, imp, NEG)
    imp = jnp.where((blk == 0) | (blk == cur) | (blk == cur - 1), -NEG, imp)
    rank = jnp.zeros((nsel, TQ), F32)
    for s2 in range(nsel):
        other = imp[s2:s2 + 1, :]
        rank = rank + jnp.where((other > imp) | ((other == imp) & (blk > s2)), 1.0, 0.0)
    sel = jnp.where(rank < float(min(N_TOP, nsel)), 1.0, 0.0).astype(BF16)
    sel_ref[...] = jnp.dot(ex_ref[...], sel, preferred_element_type=F32)

    def sel_mask(kb, extra=None):
        off = pl.multiple_of(jnp.clip(kb, 0, nkb - 1) * KB, KB)
        mask = sel_ref[pl.ds(off, KB), :] > 0.5
        return rep(mask if extra is None else mask & extra)

    state = _softmax_pieces(qst, [kv(ks_ref, vs_ref, qt) + (bias_d0, sel_mask(qt, causal)),
                                  kv(ks_ref, vs_ref, qt - 1) + (bias_d1 + penalty(qt >= 1), sel_mask(qt - 1))], None)
    nfar = jnp.maximum(qt - 1, 0)

    def far_sel(gi, state):
        pieces = []
        for j in range(FAR_GROUP):
            kb = gi * FAR_GROUP + j
            pieces.append(kv(ks_ref, vs_ref, kb) + (far_row + penalty(kb < nfar), sel_mask(kb)))
        return _softmax_pieces(qst, pieces, state)

    _, l_s, acc_s = lax.fori_loop(0, (nfar + FAR_GROUP - 1) // FAR_GROUP, far_sel, state)
    o_s = acc_s / l_s

    gate = _sigmoid(gl_ref[...]).T
    outs = []
    for r in range(R_Q):
        cs = slice(r * TQ, (r + 1) * TQ)
        outs.append(gate[3 * r:3 * r + 1, :] * o_c[:, cs] + gate[3 * r + 1:3 * r + 2, :] * o_s[:, cs]
                    + gate[3 * r + 2:3 * r + 3, :] * o_w[:, cs])
    o = jnp.concatenate(outs, axis=0).T
    o_ref[...] = (o * _silu(zb_ref[...])).astype(o_ref.dtype)


def _nsa_prompt(hb, kvp, tabs, bsz, t):
    ks_t, vs_t, kw_t, vw_t, kc_t, vc_t = kvp
    nch = t // CMP_STRIDE
    nqt = t // TQ
    gw = R_Q * HD
    k_spec = lambda n: pl.BlockSpec((1, 1, n, HD), lambda b, g, i: (b, g, 0, 0))
    vt_spec = lambda n: pl.BlockSpec((1, 1, HD, n), lambda b, g, i: (b, g, 0, 0))
    gl0 = (D_B + 6 * KV_W) // GL_PAD
    zb0 = (D_B + 6 * KV_W + G_KV * GL_PAD) // gw
    return pl.pallas_call(
        functools.partial(_nsa_prompt_kernel, t=t, nch=nch),
        grid=(bsz, G_KV, nqt),
        in_specs=[pl.BlockSpec((TQ, gw), lambda b, g, i: (b * nqt + i, g)),
                  pl.BlockSpec((TQ, GL_PAD), lambda b, g, i: (b * nqt + i, gl0 + g)),
                  pl.BlockSpec((TQ, gw), lambda b, g, i: (b * nqt + i, zb0 + g)),
                  k_spec(nch), vt_spec(nch), k_spec(t), vt_spec(t), k_spec(t), vt_spec(t),
                  pl.BlockSpec((R_Q, nch, TQ), lambda b, g, i: (g, 0, i)),
                  pl.BlockSpec((R_Q, KB, TQ), lambda b, g, i: (g, 0, 0)),
                  pl.BlockSpec((R_Q, KB, TQ), lambda b, g, i: (g, 0, 0)),
                  pl.BlockSpec(memory_space=pltpu.SMEM),
                  pl.BlockSpec(tabs['overlap'].shape, lambda b, g, i: (0, 0)),
                  pl.BlockSpec(tabs['expand'].shape, lambda b, g, i: (0, 0))],
        out_specs=pl.BlockSpec((TQ, gw), lambda b, g, i: (b * nqt + i, g)),
        out_shape=jax.ShapeDtypeStruct((bsz * t, D_B), BF16),
        scratch_shapes=[pltpu.VMEM((t, TQ), F32)],
        compiler_params=_params(("parallel", "parallel", "arbitrary")),
    )(hb, hb, hb, kc_t, vc_t, ks_t, vs_t, kw_t, vw_t, tabs['bias_c'], tabs['d0'], tabs['d1'], tabs['far'],
      tabs['overlap'], tabs['expand'])


def _row_softmax(s, mask, s_new):
    sm = jnp.where(mask, s, NEG)
    m = jnp.maximum(jnp.max(sm, axis=-1, keepdims=True), s_new)
    p = jnp.where(mask, jnp.exp(sm - m), 0.0)
    p_new = jnp.exp(s_new - m)
    l = jnp.sum(p, axis=-1, keepdims=True) + p_new
    return p / l, p_new / l


def _nsa_sample_kernel(pt_ref, *refs, n_pages, past):
    page_refs = refs[:n_pages]
    (win_ref, qbd_ref, new_ref, gl_ref, zb_ref, w1_ref, b1_ref, w2_ref, b2_ref,
     bc_ref, bs_ref, bw_ref, b0_ref, ov_ref, gsum_ref, bdm_ref, ex_ref, o_ref, x_s, kc_s, vc_s) = refs[n_pages:]
    del pt_ref
    nch = past // CMP_STRIDE
    nsel = past // SEL_LEN + 1
    qbd = qbd_ref[0]
    qbd_f = qbd.astype(F32)
    npair = KV_W // LANES

    for pi, pr in enumerate(page_refs):
        for kv in range(2):
            for pair in range(npair):
                x_s[kv, pair, pi * PAGE_SIZE:(pi + 1) * PAGE_SIZE, :] = pr[0, 0, kv, pair * LANES:(pair + 1) * LANES, :].T
    for kv, dst in ((0, kc_s), (1, vc_s)):
        load = lambda tau, pair, kv=kv: x_s[kv, pair, pl.ds(tau, nch, stride=CMP_STRIDE), :]
        res = _compress_rows(load, w1_ref, b1_ref[kv], w2_ref[kv], b2_ref[kv], kv, nch)
        for g in range(G_KV):
            dst[:, g * HD:(g + 1) * HD] = res[g]
    ncol = lax.broadcasted_iota(jnp.int32, (H_B, nch), 1)
    s_c = _bdot(qbd, kc_s[...], NT) + bc_ref[...]
    mask_c = ncol < nch - 1
    sm = jnp.where(mask_c, s_c, NEG)
    p_c = jnp.where(mask_c, jnp.exp(sm - jnp.max(sm, axis=-1, keepdims=True)), 0.0)
    p_c = p_c / jnp.sum(p_c, axis=-1, keepdims=True)
    o_c = _bdot(p_c, vc_s[...])
    p_hi = p_c.astype(BF16)
    p_lo = (p_c - p_hi.astype(F32)).astype(BF16)
    ov = ov_ref[...]
    imp = jnp.dot(p_hi, ov, preferred_element_type=F32) + jnp.dot(p_lo, ov, preferred_element_type=F32)
    i_hi = imp.astype(BF16)
    i_lo = (imp - i_hi.astype(F32)).astype(BF16)
    gs = gsum_ref[...]
    imp = jnp.dot(gs, i_hi, preferred_element_type=F32) + jnp.dot(gs, i_lo, preferred_element_type=F32)
    nsp = imp.shape[1]
    blk = lax.broadcasted_iota(jnp.int32, (H_B, nsp), 1)
    cur = past // SEL_LEN
    imp = jnp.where((blk == 0) | (blk == cur) | (blk == cur - 1), -NEG, imp)
    imp = jnp.where(blk < nsel, imp, 2.0 * NEG)
    rank = jnp.zeros((H_B, nsp), F32)
    for s2 in range(nsel):
        other = imp[:, s2:s2 + 1]
        rank = rank + jnp.where((other > imp) | ((other == imp) & (blk > s2)), 1.0, 0.0)
    sel = jnp.where(rank < float(N_TOP), 1.0, 0.0).astype(BF16)
    mask_s = jnp.dot(sel, ex_ref[...], preferred_element_type=F32) > 0.5

    new = new_ref[0]
    ks_new = new[:, 2 * KV_W:3 * KV_W]
    vs_new = new[:, 3 * KV_W:4 * KV_W]
    kw_new = new[:, 4 * KV_W:5 * KV_W]
    vw_new = new[:, 5 * KV_W:6 * KV_W]
    b0 = b0_ref[...]
    s_s = jnp.concatenate([_bdot(qbd, pr[0, 0, 2]) for pr in page_refs], axis=1) + bs_ref[...]
    s_new = jnp.sum(qbd_f * ks_new, axis=-1, keepdims=True) + b0
    p_s, p_new = _row_softmax(s_s, mask_s, s_new)
    o_s = p_new * vs_new
    for pi, pr in enumerate(page_refs):
        o_s = o_s + _bdot(p_s[:, pi * PAGE_SIZE:(pi + 1) * PAGE_SIZE], pr[0, 0, 3], NT)
    nbuf = win_ref.shape[-1]
    wcol = lax.broadcasted_iota(jnp.int32, (H_B, nbuf), 1)
    s_w = _bdot(qbd, win_ref[0, 0, 0]) + bw_ref[...]
    s_wn = jnp.sum(qbd_f * kw_new, axis=-1, keepdims=True) + b0
    p_w, p_wn = _row_softmax(s_w, wcol >= nbuf + 1 - WIN, s_wn)
    o_w = _bdot(p_w, win_ref[0, 0, 1], NT) + p_wn * vw_new
    gate = _sigmoid(gl_ref[0])
    o = gate[:, 0:1] * o_c + gate[:, 1:2] * o_s + gate[:, 2:3] * o_w
    o = o * bdm_ref[...]
    o16 = o[:, 0:HD]
    for g in range(1, G_KV):
        o16 = o16 + o[:, g * HD:(g + 1) * HD]
    o_ref[0] = (o16 * _silu(zb_ref[0])).astype(o_ref.dtype)


def _nsa_sample(cache_l, l, page_table, win, qbd, new_rows, gl, zb, cp, tabs):
    bsz, n_pages = page_table.shape
    past = n_pages * PAGE_SIZE
    nch = past // CMP_STRIDE
    nbuf = win.shape[-1]
    full = lambda a: pl.BlockSpec(a.shape, lambda b, pt: (0,) * a.ndim)
    page_specs = [pl.BlockSpec((1, 1) + cache_l.shape[2:],
                               functools.partial(lambda b, pt, j: (l, pt[b, j], 0, 0, 0), j=j))
                  for j in range(n_pages)]
    consts = [cp['w1'], cp['b1'], cp['w2'], cp['b2'], tabs['bias_c'], tabs['bias_s'], tabs['bias_w'], tabs['bias_0'],
              tabs['overlap'], tabs['gsum'], tabs['bdmask'], tabs['expand']]
    grid_spec = pltpu.PrefetchScalarGridSpec(
        num_scalar_prefetch=1,
        grid=(bsz,),
        in_specs=page_specs + [
            pl.BlockSpec((1, 1, 2, KV_W, nbuf), lambda b, pt: (l, b, 0, 0, 0)),
            pl.BlockSpec((1, H_B, KV_W), lambda b, pt: (b, 0, 0)),
            pl.BlockSpec((1, 1, 6 * KV_W), lambda b, pt: (b, 0, 0)),
            pl.BlockSpec((1, H_B, 3), lambda b, pt: (b, 0, 0)),
            pl.BlockSpec((1, H_B, HD), lambda b, pt: (b, 0, 0)),
        ] + [full(a) for a in consts],
        out_specs=pl.BlockSpec((1, H_B, HD), lambda b, pt: (b, 0, 0)),
        scratch_shapes=[pltpu.VMEM((2, KV_W // LANES, past, LANES), F32), pltpu.VMEM((nch, KV_W), F32),
                        pltpu.VMEM((nch, KV_W), F32)],
    )
    return pl.pallas_call(
        functools.partial(_nsa_sample_kernel, n_pages=n_pages, past=past),
        grid_spec=grid_spec,
        out_shape=jax.ShapeDtypeStruct((bsz, H_B, HD), BF16),
        compiler_params=_params(("arbitrary",)),
    )(page_table, *([cache_l] * n_pages), win, qbd, new_rows, gl, zb, *consts)


def _t5_bucket(dist):
    n = jnp.maximum(dist, 0)
    max_exact = NUM_BUCKETS // 2
    nf = jnp.maximum(n, 1).astype(F32)
    large = max_exact + (jnp.log(nf / max_exact) / math.log(MAX_DIST / max_exact)
                         * (NUM_BUCKETS - max_exact)).astype(jnp.int32)
    large = jnp.minimum(large, NUM_BUCKETS - 1)
    return jnp.where(n < max_exact, n, large)


def _bias_of(rel_bias, dist):
    return jnp.moveaxis(rel_bias[_t5_bucket(dist)], -1, 0).astype(F32)


def _overlap(nch, nsel):
    ci = np.arange(nch)[:, None] * CMP_STRIDE
    sj = np.arange(nsel)[None, :] * SEL_LEN
    ov = ((ci < sj + SEL_LEN) & (ci + CMP_LEN > sj)).astype(np.float32)
    ov[nch - 1:, :] = 0.0
    return ov


def _skew(g, n, step, length):
    h, L = g.shape
    flat = jnp.tile(g, (1, n))[:, :n * (L - step)]
    return flat.reshape(h, n, L - step)[:, :, :length]


def _prompt_tables(rel_bias, t):
    nch = t // CMP_STRIDE
    nsel = t // SEL_LEN
    f = _bias_of(rel_bias, jnp.arange(max(t, 2 * KB), dtype=jnp.int32))
    f0 = lambda n: jnp.broadcast_to(f[:, :1], (H_B, n))
    lead = CMP_LEN - 1
    g_c = jnp.concatenate([f0(lead), f[:, :t - lead], f0(CMP_STRIDE * nch)], axis=1)
    g_0 = jnp.concatenate([f[:, :KB], f0(KB)], axis=1)
    g_1 = jnp.concatenate([f[:, KB:2 * KB], f[:, :KB]], axis=1)
    expand = (np.arange(t)[:, None] // SEL_LEN == np.arange(nsel)[None, :]).astype(np.float32)
    return {
        'bias_c': _skew(g_c, nch, CMP_STRIDE, t),
        'd0': _skew(g_0, KB, 1, TQ),
        'd1': _skew(g_1, KB, 1, TQ),
        'far': rel_bias[NUM_BUCKETS - 1].astype(F32),
        'overlap': jnp.asarray(_overlap(nch, nsel).T, BF16),
        'expand': jnp.asarray(expand, BF16),
    }


def _sample_tables(rel_bias, past, nbuf):
    nch = past // CMP_STRIDE
    nsel = past // SEL_LEN + 1
    nsp = -(-nsel // LANES) * LANES
    cmp_end = jnp.arange(nch, dtype=jnp.int32) * CMP_STRIDE + CMP_LEN - 1
    ov = np.zeros((nch, nsp), np.float32)
    ov[:, :nsel] = _overlap(nch, nsel)
    hh = np.arange(H_B)
    gsum = (hh[:, None] // R_Q == hh[None, :] // R_Q).astype(np.float32)
    bdm = (hh[:, None] // R_Q == np.arange(KV_W)[None, :] // HD).astype(np.float32)
    expand = (np.arange(nsp)[:, None] == np.arange(past)[None, :] // SEL_LEN).astype(np.float32)
    return {
        'bias_c': _bias_of(rel_bias, past - cmp_end),
        'bias_s': _bias_of(rel_bias, past - jnp.arange(past, dtype=jnp.int32)),
        'bias_w': _bias_of(rel_bias, nbuf - jnp.arange(nbuf, dtype=jnp.int32)),
        'bias_0': _bias_of(rel_bias, jnp.zeros((1,), jnp.int32)),
        'overlap': jnp.asarray(ov, BF16),
        'gsum': jnp.asarray(gsum, BF16),
        'bdmask': jnp.asarray(bdm, F32),
        'expand': jnp.asarray(expand, BF16),
    }


def _layer_params(l, w_in, mu_shift, rw_w0, rw_w2, rw_a0, rw_a2, rw_kk, rw_ka, rw_rk, rw_gn_g, rw_gn_b,
                  rw_v0, rw_v1, rw_v2, cmp_w1, cmp_b1, cmp_w2, cmp_b2, w_up_a, w_up_b, w_out, ln_g, ln_b):
    w = w_in[l]
    b0 = A_COLS
    q_kv = w[:, b0:b0 + D_B + 6 * KV_W]
    gl = w[:, b0 + D_B + 6 * KV_W:b0 + D_B + 6 * KV_W + 3 * H_B].reshape(D_MODEL, G_KV, 3 * R_Q)
    gl = jnp.pad(gl, ((0, 0), (0, 0), (0, GL_PAD - 3 * R_Q))).reshape(D_MODEL, G_KV * GL_PAD)
    zb = w[:, b0 + D_B + 6 * KV_W + 3 * H_B:b0 + D_B + 6 * KV_W + 3 * H_B + D_B]
    half = CMP_STRIDE * HD
    w1 = cmp_w1[l]
    w1r = jnp.concatenate([w1[:, :half].reshape(2, CMP_STRIDE, HD, CMP_HID),
                           w1[:, half:].reshape(2, CMP_STRIDE, HD, CMP_HID)], axis=-1)
    zero = jnp.zeros_like(w1r)
    w1r = jnp.stack([jnp.concatenate([w1r, zero], axis=2), jnp.concatenate([zero, w1r], axis=2)], axis=2)
    row = lambda a: a.reshape(1, -1).astype(F32)
    p = {
        'wa': w[:, :A_COLS].astype(BF16),
        'wb': jnp.concatenate([q_kv, gl, zb], axis=1).astype(BF16),
        'wg': w[:, b0 + D_B + 6 * KV_W + 3 * H_B + D_B:].astype(BF16),
        'wc': w[:, b0 + D_B:b0 + D_B + 2 * KV_W].astype(BF16),
        'mu': row(mu_shift[l]), 'w0': row(rw_w0[l]), 'w2': rw_w2[l].astype(BF16), 'a0': row(rw_a0[l]),
        'a2': rw_a2[l].astype(BF16), 'kk': row(rw_kk[l]), 'ka': row(rw_ka[l]), 'rk': row(rw_rk[l]),
        'gn_g': row(rw_gn_g[l]), 'gn_b': row(rw_gn_b[l]),
        'cmp': {'w1': w1r.astype(BF16), 'b1': cmp_b1[l].reshape(2, 1, CMP_HID).astype(F32),
                'w2': cmp_w2[l].astype(BF16), 'b2': cmp_b2[l].reshape(2, 1, HD).astype(F32)},
        'w_up_a': w_up_a[l].astype(BF16), 'w_up_b': w_up_b[l].astype(BF16), 'w_out': w_out[l].astype(BF16),
        'ln_g': row(ln_g[l]), 'ln_b': row(ln_b[l]),
    }
    if l > 0:
        p['v0'] = row(rw_v0[l - 1])
        p['v1'] = rw_v1[l - 1].astype(BF16)
        p['v2'] = rw_v2[l - 1].astype(BF16)
    return p


def _project(x2, p):
    m = x2.shape[0]
    ha = _matmul(x2, p['wa'], PROJ_TM, A_COLS // 3)
    hb = _matmul(x2, p['wb'], PROJ_TM, PROJ_TN)
    hg = _matmul(x2, p['wg'], PROJ_TM, PROJ_TN)
    return ha, hb, hg


def _finish(x2, o_a, o_b, hg, p):
    merged = _up_gate(o_a, o_b, p['w_up_a'], p['w_up_b'], hg)
    return _out_ln(merged, p['w_out'], x2, p['ln_g'], p['ln_b'])


def _prompt_layer(x2, vfirst, p, tabs, bsz, t):
    ha, hb, hg = _project(x2, p)
    o_a, vfirst, wkv = _rwkv_prompt(ha, vfirst, p, bsz, t)
    hc = _matmul(x2, p['wc'], PROJ_TM, 2 * KV_W)
    kvp = _kv_prep(hb, hc, p['cmp'], bsz, t)
    o_b = _nsa_prompt(hb, kvp, tabs, bsz, t)
    y = _finish(x2, o_a, o_b, hg, p)
    kvc = D_B
    new_rows = hb[:, kvc:kvc + 4 * KV_W].reshape(bsz, t, 4, G_KV, HD)
    nwin = min(WIN, t)
    win_state = hb[:, kvc + 4 * KV_W:kvc + 6 * KV_W].reshape(bsz, t, 2, G_KV, HD)[:, t - nwin:]
    shift = ha.reshape(bsz, t, A_COLS)[:, t - 1]
    return y, vfirst, (new_rows, win_state, wkv, shift)


def _sample_layer(x2, vfirst, l, p, tabs, cache_l, win_l, cache_win_kv, state_wkv, state_shift, page_table):
    bsz = x2.shape[0]
    ha, hb, hg = _project(x2, p)
    o_a, vfirst, wkv = _rwkv_sample(ha, state_shift[l], state_wkv[l], vfirst, p)
    kvc = D_B
    q = hb[:, :D_B].reshape(bsz, G_KV, R_Q, HD) * SCALE
    eye = jnp.eye(G_KV, dtype=F32)
    qbd = (q[:, :, :, None, :] * eye[None, :, None, :, None]).reshape(bsz, H_B, KV_W).astype(BF16)
    new6 = hb[:, kvc:kvc + 6 * KV_W]
    gl0 = kvc + 6 * KV_W
    gl = hb[:, gl0:gl0 + G_KV * GL_PAD].reshape(bsz, G_KV, GL_PAD)[:, :, :3 * R_Q].reshape(bsz, H_B, 3)
    zb = hb[:, gl0 + G_KV * GL_PAD:].reshape(bsz, H_B, HD)
    nbuf = cache_win_kv.shape[2]
    o_b = _nsa_sample(cache_l, l, page_table, win_l, qbd, new6[:, None, :], gl, zb, p['cmp'], tabs)
    y = _finish(x2, o_a, o_b.reshape(bsz, D_B), hg, p)
    new_rows = new6[:, :4 * KV_W].reshape(bsz, 1, 4, G_KV, HD)
    new_win = new6[:, 4 * KV_W:].reshape(bsz, 1, 2, G_KV, HD)
    win_state = jnp.concatenate([cache_win_kv[l], new_win], axis=1)[:, -nbuf:]
    return y, vfirst, (new_rows, win_state, wkv, ha)


def kernel(x_prompt, x_sample, cache_kv, cache_win_kv, state_wkv, state_shift, page_table, w_in, mu_shift, rw_w0, rw_w2, rw_a0, rw_a2, rw_kk, rw_ka, rw_rk, rw_gn_g, rw_gn_b, rw_v0, rw_v1, rw_v2, cmp_w1, cmp_b1, cmp_w2, cmp_b2, rel_bias, w_up_a, w_up_b, w_out, ln_g, ln_b):
    bsz, t, _ = x_prompt.shape
    dec_b = x_sample.shape[0]
    n_pages = page_table.shape[1]
    depth, n_phys = cache_kv.shape[:2]
    cache_l = jnp.transpose(cache_kv, (0, 1, 3, 4, 5, 2)).reshape(depth, n_phys, 4, KV_W, PAGE_SIZE)
    win_l = jnp.transpose(cache_win_kv, (0, 1, 3, 4, 5, 2)).reshape(depth, dec_b, 2, KV_W, cache_win_kv.shape[2])
    tabs_p = _prompt_tables(rel_bias, t)
    tabs_s = _sample_tables(rel_bias, n_pages * PAGE_SIZE, cache_win_kv.shape[2])
    y_p = x_prompt.reshape(bsz * t, D_MODEL)
    y_s = x_sample.reshape(dec_b, D_MODEL)
    vf_p, vf_s = None, None
    st_p, st_s = [], []
    for l in range(depth):
        p = _layer_params(l, w_in, mu_shift, rw_w0, rw_w2, rw_a0, rw_a2, rw_kk, rw_ka, rw_rk, rw_gn_g, rw_gn_b,
                          rw_v0, rw_v1, rw_v2, cmp_w1, cmp_b1, cmp_w2, cmp_b2, w_up_a, w_up_b, w_out, ln_g, ln_b)
        y_p, vf_p, sp = _prompt_layer(y_p, vf_p, p, tabs_p, bsz, t)
        y_s, vf_s, ss = _sample_layer(y_s, vf_s, l, p, tabs_s, cache_l, win_l, cache_win_kv, state_wkv, state_shift,
                                      page_table)
        st_p.append(sp)
        st_s.append(ss)
    stack = lambda st, i: jnp.stack([s[i] for s in st])
    return (y_p.reshape(bsz, t, D_MODEL), y_s.reshape(dec_b, 1, D_MODEL),
            stack(st_p, 0), stack(st_p, 1), stack(st_p, 2), stack(st_p, 3),
            stack(st_s, 0), stack(st_s, 1), stack(st_s, 2), stack(st_s, 3))
```

```python
import functools
import math

import numpy as np
import jax
import jax.numpy as jnp
from jax import lax
from jax.experimental import pallas as pl
from jax.experimental.pallas import tpu as pltpu

D_MODEL = 2048
DEPTH = 2
PAGE_SIZE = 128
HS = 64
D_A = D_MODEL // 2
H_A = D_A // HS
R_W = 64
R_A = 64
R_V = 32
GN_EPS = 64e-5
HD = 64
D_B = D_MODEL // 2
H_B = D_B // HD
G_KV = 4
R_Q = H_B // G_KV
KV_W = G_KV * HD
CMP_LEN = 32
CMP_STRIDE = 16
CMP_HID = 128
SEL_LEN = 64
N_TOP = 16
WIN = 512
NUM_BUCKETS = 32
MAX_DIST = 128
SCALE = HD ** -0.5
A_COLS = 4 * D_A + R_W + R_A
ALPHA = (2 * DEPTH) ** 0.25
LN_EPS = 1e-5
NEG = -1e30
LOG2E = 1.4426950408889634

F32 = jnp.float32
BF16 = jnp.bfloat16

LANES = 128
VMEM_LIMIT = 56 * 1024 * 1024
CHUNK = 64
RWKV_ROWS = 1
UNIT_GROUP = 16
TQ = 128
KB = 128
FAR_GROUP = 4
PROJ_TM = 1024
PROJ_TN = 1024
UP_TM = 512
LN_TM = 512
GL_PAD = LANES
HB_COLS = D_B + 6 * KV_W + G_KV * GL_PAD + D_B

NT = (((1,), (1,)), ((), ()))
TN = (((0,), (0,)), ((), ()))


def _params(sem):
    return pltpu.CompilerParams(dimension_semantics=sem, vmem_limit_bytes=VMEM_LIMIT)


def _bdot(a, b, dims=None):
    a = a.astype(BF16)
    b = b.astype(BF16)
    if dims is None:
        return jnp.dot(a, b, preferred_element_type=F32)
    return lax.dot_general(a, b, dims, preferred_element_type=F32)


def _sigmoid(x):
    return 1.0 / (1.0 + jnp.exp(-x))


def _silu(x):
    return x * _sigmoid(x)


def _gelu_tanh(x):
    return 0.5 * x * (1.0 + jnp.tanh(math.sqrt(2.0 / math.pi) * (x + 0.044715 * (x * x * x))))


def _mm_kernel(x_ref, w_ref, o_ref):
    o_ref[...] = _bdot(x_ref[...], w_ref[...])


def _matmul(x, w, tm, tn):
    m, k = x.shape
    n = w.shape[1]
    tm = min(tm, m)
    return pl.pallas_call(
        _mm_kernel,
        grid=(m // tm, n // tn),
        in_specs=[pl.BlockSpec((tm, k), lambda i, j: (i, 0)),
                  pl.BlockSpec((k, tn), lambda i, j: (0, j))],
        out_specs=pl.BlockSpec((tm, tn), lambda i, j: (i, j)),
        out_shape=jax.ShapeDtypeStruct((m, n), F32),
        compiler_params=_params(("parallel", "parallel")),
    )(x, w)


def _transpose_kernel(x_ref, o_ref):
    o_ref[0] = x_ref[...].T


def _cols_transposed(h, bsz, t, col0, ncols, last):
    per = t // last
    return pl.pallas_call(
        _transpose_kernel,
        grid=(bsz, ncols // LANES),
        in_specs=[pl.BlockSpec((last, LANES), lambda b, j: (b * per + per - 1, col0 // LANES + j))],
        out_specs=pl.BlockSpec((1, LANES, last), lambda b, j: (b, j, 0)),
        out_shape=jax.ShapeDtypeStruct((bsz, ncols, last), F32),
        compiler_params=_params(("parallel", "parallel")),
    )(h)


def _up_kernel(oa_ref, ob_ref, wa_ref, wb_ref, ga_ref, gb_ref, o_ref):
    ua = _bdot(oa_ref[...], wa_ref[...])
    ub = _bdot(ob_ref[...], wb_ref[...])
    o_ref[...] = (_sigmoid(ga_ref[...]) * ua + _sigmoid(gb_ref[...]) * ub).astype(o_ref.dtype)


def _up_gate(o_a, o_b, w_up_a, w_up_b, hg, tm=UP_TM, tn=D_MODEL):
    m = o_a.shape[0]
    tm = min(tm, m)
    nb = D_MODEL // tn
    return pl.pallas_call(
        _up_kernel,
        grid=(m // tm, nb),
        in_specs=[pl.BlockSpec((tm, D_A), lambda i, j: (i, 0)),
                  pl.BlockSpec((tm, D_B), lambda i, j: (i, 0)),
                  pl.BlockSpec((D_A, tn), lambda i, j: (0, j)),
                  pl.BlockSpec((D_B, tn), lambda i, j: (0, j)),
                  pl.BlockSpec((tm, tn), lambda i, j: (i, j)),
                  pl.BlockSpec((tm, tn), lambda i, j: (i, j + nb))],
        out_specs=pl.BlockSpec((tm, tn), lambda i, j: (i, j)),
        out_shape=jax.ShapeDtypeStruct((m, D_MODEL), BF16),
        compiler_params=_params(("parallel", "parallel")),
    )(o_a, o_b, w_up_a, w_up_b, hg, hg)


def _out_ln_kernel(m_ref, w_ref, x_ref, g_ref, b_ref, o_ref):
    u = ALPHA * x_ref[...] + _bdot(m_ref[...], w_ref[...])
    mu = jnp.mean(u, axis=-1, keepdims=True)
    d = u - mu
    var = jnp.mean(d * d, axis=-1, keepdims=True)
    o_ref[...] = d * lax.rsqrt(var + LN_EPS) * g_ref[...] + b_ref[...]


def _out_ln(merged, w_out, x, ln_g, ln_b, tm=LN_TM):
    m = x.shape[0]
    tm = min(tm, m)
    return pl.pallas_call(
        _out_ln_kernel,
        grid=(m // tm,),
        in_specs=[pl.BlockSpec((tm, D_MODEL), lambda i: (i, 0)),
                  pl.BlockSpec((D_MODEL, D_MODEL), lambda i: (0, 0)),
                  pl.BlockSpec((tm, D_MODEL), lambda i: (i, 0)),
                  pl.BlockSpec((1, D_MODEL), lambda i: (0, 0)),
                  pl.BlockSpec((1, D_MODEL), lambda i: (0, 0))],
        out_specs=pl.BlockSpec((tm, D_MODEL), lambda i: (i, 0)),
        out_shape=jax.ShapeDtypeStruct((m, D_MODEL), F32),
        compiler_params=_params(("parallel",)),
    )(merged, w_out, x, ln_g, ln_b)


def _rwkv_premix(xm, vfirst, w0, w2, a0, a2, kkp, ka, vgate):
    r = xm[:, 0:D_A]
    k = xm[:, D_A:2 * D_A]
    v = xm[:, 2 * D_A:3 * D_A]
    w_lo = xm[:, 3 * D_A:3 * D_A + R_W]
    a_lo = xm[:, 3 * D_A + R_W:3 * D_A + R_W + R_A]
    z = xm[:, 3 * D_A + R_W + R_A:A_COLS]
    t = w0 + _bdot(jnp.tanh(w_lo), w2)
    lw = -math.exp(-0.5) * _sigmoid(t)
    if vgate is not None:
        v0, v1, v2 = vgate
        vg = _sigmoid(v0 + _bdot(_bdot(v, v1), v2))
        v = v + (vfirst - v) * vg
    lr = _sigmoid(a0 + _bdot(a_lo, a2))
    kkr = k * kkp
    k2 = k * (1.0 + (lr - 1.0) * ka)
    return r, lw, k2, v, lr, kkr, z


def _head_post(y, r, k2, v, z, rk, gn_g, gn_b):
    mu = jnp.mean(y, axis=-1, keepdims=True)
    d = y - mu
    var = jnp.mean(d * d, axis=-1, keepdims=True)
    yn = d * lax.rsqrt(var + GN_EPS) * gn_g + gn_b
    bonus = jnp.sum(r * k2 * rk, axis=-1, keepdims=True) * v
    return (yn + bonus) * _silu(z)


def _rwkv_prompt_kernel(*refs, has_vgate):
    if has_vgate:
        (fa_ref, vf_ref, mu_ref, w0_ref, w2_ref, a0_ref, a2_ref, kkp_ref, ka_ref, rk_ref, gg_ref, gb_ref,
         v0_ref, v1_ref, v2_ref, o_ref, s_out_ref, s_ref, last_ref) = refs
    else:
        (fa_ref, mu_ref, w0_ref, w2_ref, a0_ref, a2_ref, kkp_ref, ka_ref, rk_ref, gg_ref, gb_ref,
         o_ref, vf_out_ref, s_out_ref, s_ref, last_ref) = refs
    c = pl.program_id(1)
    nc = pl.num_programs(1)
    C = CHUNK

    @pl.when(c == 0)
    def _():
        s_ref[...] = jnp.zeros_like(s_ref)
        last_ref[...] = jnp.zeros_like(last_ref)

    ri = lax.broadcasted_iota(jnp.int32, (C, C), 0)
    ci = lax.broadcasted_iota(jnp.int32, (C, C), 1)
    tri_i = ri >= ci
    tri_s = ri > ci
    tri_b = jnp.where(tri_i, 1.0, 0.0).astype(BF16)
    eye = jnp.where(ri == ci, 1.0, 0.0).astype(F32)
    row = lax.broadcasted_iota(jnp.int32, (C, 1), 0)
    rk = rk_ref[...]
    gg = gg_ref[...]
    gb = gb_ref[...]
    nb = fa_ref.shape[0]

    pre = []
    for bi in range(nb):
        x = fa_ref[bi]
        prev = jnp.where(row == 0, last_ref[bi, 0:1, :], pltpu.roll(x, 1, axis=0))
        last_ref[bi, 0:1, :] = x[C - 1:C, :]
        xm = x + (prev - x) * mu_ref[...]
        if has_vgate:
            vgate = (v0_ref[...], v1_ref[...], v2_ref[...])
            vfirst = vf_ref[bi]
        else:
            vgate, vfirst = None, None
        r, lw, k2, v, lr, kkr, z = _rwkv_premix(xm, vfirst, w0_ref[...], w2_ref[...], a0_ref[...], a2_ref[...],
                                                kkp_ref[...], ka_ref[...], vgate)
        if not has_vgate:
            vf_out_ref[bi] = v
        lw_hi = lw.astype(BF16)
        lw_lo = (lw - lw_hi.astype(F32)).astype(BF16)
        L = jnp.dot(tri_b, lw_hi, preferred_element_type=F32) + jnp.dot(tri_b, lw_lo, preferred_element_type=F32)
        LC = L[C - 1:C, :]
        pre.append(dict(r=r, k2=k2, v=v, lr=lr, kkr=kkr, z=z, e_in=jnp.exp(L), e_ex=jnp.exp(L - lw),
                        e_neg=jnp.exp(-L), e_rem=jnp.exp(LC - L), pc=jnp.exp(LC)))

    def run(units):
        heads = range(len(units))
        sls = [slice(h * HS, (h + 1) * HS) for _, h in units]
        col = lambda name: [pre[bi][name][:, sls[u]] for u, (bi, _) in enumerate(units)]
        bf = lambda xs: [x.astype(BF16) for x in xs]
        kk = []
        for u in col('kkr'):
            kk.append(u * lax.rsqrt(jnp.maximum(jnp.sum(u * u, axis=-1, keepdims=True), 1e-24)))
        r_h, k_h, v_h, lr_h, z_h = col('r'), col('k2'), col('v'), col('lr'), col('z')
        e_in, e_ex, e_neg, e_rem, pc = col('e_in'), col('e_ex'), col('e_neg'), col('e_rem'), col('pc')
        v_b = bf(v_h)
        b_h = [kk[h] * lr_h[h] for h in heads]
        at = [(-kk[h]) * e_ex[h] for h in heads]
        rt = [r_h[h] * e_in[h] for h in heads]
        bt = [b_h[h] * e_neg[h] for h in heads]
        kt = [k_h[h] * e_neg[h] for h in heads]
        bh = bf([b_h[h] * e_rem[h] for h in heads])
        kh = bf([k_h[h] * e_rem[h] for h in heads])
        g = [_bdot(jnp.concatenate([at[h], rt[h]], axis=0), jnp.concatenate([bt[h], kt[h]], axis=0), NT)
             for h in heads]
        a_ab = bf([jnp.where(tri_s, x[:C, :C], 0.0) for x in g])
        a_ak = bf([jnp.where(tri_s, x[:C, C:], 0.0) for x in g])
        a_rb = bf([jnp.where(tri_i, x[C:, :C], 0.0) for x in g])
        a_rk = bf([jnp.where(tri_i, x[C:, C:], 0.0) for x in g])
        tm = [eye + x.astype(F32) for x in a_ab]
        ap = bf([_bdot(x, x) for x in a_ab])
        n = 2
        while n < C:
            tm_next = [tm[h] + _bdot(tm[h], ap[h]) for h in heads]
            if 2 * n < C:
                ap = bf([_bdot(x, x) for x in ap])
            tm = tm_next
            n *= 2
        akv = [_bdot(a_ak[h], v_b[h]) for h in heads]
        wu = bf([_bdot(tm[h], jnp.concatenate([at[h], akv[h]], axis=1)) for h in heads])
        arw = [_bdot(a_rb[h], wu[h]) for h in heads]
        yh = [arw[h][:, HS:] + _bdot(a_rk[h], v_b[h]) for h in heads]
        s_old = [s_ref[bi, h] for bi, h in units]
        s_b = bf(s_old)
        y = [_bdot(rt[h] + arw[h][:, :HS], s_b[h], NT) + yh[h] for h in heads]
        bw = [_bdot(bh[h], wu[h][:, :HS], TN) for h in heads]
        nt = [_bdot(jnp.concatenate([wu[h][:, HS:], v_b[h]], axis=0), jnp.concatenate([bh[h], kh[h]], axis=0), TN)
              for h in heads]
        for u, (bi, h) in enumerate(units):
            s_ref[bi, h] = s_old[u] * pc[u] + _bdot(s_b[u], bw[u], NT) + nt[u]
        for u, (bi, h) in enumerate(units):
            sl = sls[u]
            o_ref[bi, :, sl] = _head_post(y[u], r_h[u], k_h[u], v_h[u], z_h[u], rk[:, sl], gg[:, sl],
                                          gb[:, sl]).astype(o_ref.dtype)

    all_units = [(bi, h) for bi in range(nb) for h in range(H_A)]
    for g0 in range(0, len(all_units), UNIT_GROUP):
        run(all_units[g0:g0 + UNIT_GROUP])

    @pl.when(c == nc - 1)
    def _():
        s_out_ref[...] = s_ref[...]


def _rwkv_prompt(ha, vfirst, p, bsz, t):
    has_vgate = vfirst is not None
    nc = t // CHUNK
    nb = RWKV_ROWS if bsz % RWKV_ROWS == 0 else 1
    row_spec = lambda w: pl.BlockSpec((nb, CHUNK, w), lambda b, c: (b, c, 0))
    full = lambda a: pl.BlockSpec(a.shape, lambda b, c: (0,) * a.ndim)
    ins = [ha.reshape(bsz, t, A_COLS)]
    in_specs = [row_spec(A_COLS)]
    if has_vgate:
        ins.append(vfirst.reshape(bsz, t, D_A))
        in_specs.append(row_spec(D_A))
    names = ['mu', 'w0', 'w2', 'a0', 'a2', 'kk', 'ka', 'rk', 'gn_g', 'gn_b'] + (['v0', 'v1', 'v2'] if has_vgate else [])
    for nme in names:
        ins.append(p[nme])
        in_specs.append(full(p[nme]))
    out_shape = [jax.ShapeDtypeStruct((bsz, t, D_A), BF16)]
    out_specs = [row_spec(D_A)]
    if not has_vgate:
        out_shape.append(jax.ShapeDtypeStruct((bsz, t, D_A), F32))
        out_specs.append(row_spec(D_A))
    out_shape.append(jax.ShapeDtypeStruct((bsz, H_A, HS, HS), F32))
    out_specs.append(pl.BlockSpec((nb, H_A, HS, HS), lambda b, c: (b, 0, 0, 0)))
    outs = pl.pallas_call(
        functools.partial(_rwkv_prompt_kernel, has_vgate=has_vgate),
        grid=(bsz // nb, nc),
        in_specs=in_specs,
        out_specs=out_specs,
        out_shape=out_shape,
        scratch_shapes=[pltpu.VMEM((nb, H_A, HS, HS), F32), pltpu.VMEM((nb, 8, A_COLS), F32)],
        compiler_params=_params(("parallel", "arbitrary")),
    )(*ins)
    o_a = outs[0].reshape(bsz * t, D_A)
    if has_vgate:
        return o_a, vfirst, outs[1]
    return o_a, outs[1].reshape(bsz * t, D_A), outs[2]


def _rwkv_sample_kernel(*refs, has_vgate, bt):
    if has_vgate:
        (fa_ref, prev_ref, s_in_ref, vf_ref, mu_ref, w0_ref, w2_ref, a0_ref, a2_ref, kkp_ref, ka_ref, rk_ref,
         gg_ref, gb_ref, v0_ref, v1_ref, v2_ref, o_ref, s_out_ref, ops_ref, y_ref) = refs
    else:
        (fa_ref, prev_ref, s_in_ref, mu_ref, w0_ref, w2_ref, a0_ref, a2_ref, kkp_ref, ka_ref, rk_ref,
         gg_ref, gb_ref, o_ref, vf_out_ref, s_out_ref, ops_ref, y_ref) = refs
    x = fa_ref[...]
    xm = x + (prev_ref[...] - x) * mu_ref[...]
    if has_vgate:
        vgate = (v0_ref[...], v1_ref[...], v2_ref[...])
        vfirst = vf_ref[...]
    else:
        vgate, vfirst = None, None
    r, lw, k2, v, lr, kkr, z = _rwkv_premix(xm, vfirst, w0_ref[...], w2_ref[...], a0_ref[...], a2_ref[...],
                                            kkp_ref[...], ka_ref[...], vgate)
    if not has_vgate:
        vf_out_ref[...] = v
    w = jnp.exp(lw)
    for h in range(H_A):
        sl = slice(h * HS, (h + 1) * HS)
        kk = kkr[:, sl]
        kk = kk * lax.rsqrt(jnp.maximum(jnp.sum(kk * kk, axis=-1, keepdims=True), 1e-24))
        ops_ref[0, :, sl] = -kk
        ops_ref[1, :, sl] = kk * lr[:, sl]
    ops_ref[2] = w
    ops_ref[3] = k2
    ops_ref[4] = v
    ops_ref[5] = r
    ri = lax.broadcasted_iota(jnp.int32, (HS, HS), 0)
    ci = lax.broadcasted_iota(jnp.int32, (HS, HS), 1)
    eye = jnp.where(ri == ci, 1.0, 0.0).astype(F32)

    heads = range(H_A)
    sls = [slice(h * HS, (h + 1) * HS) for h in heads]
    eye_b = eye.astype(BF16)
    rows_of = lambda x: jnp.broadcast_to(x, (HS, HS)).astype(BF16)
    for b in range(bt):
        row = lambda i: [ops_ref[i, b:b + 1, sl] for sl in sls]
        a_row, b_row, w_row, k_row, v_row, r_row = (row(i) for i in range(6))
        s = [s_in_ref[b, h] for h in heads]
        sa = [_bdot(s[h], rows_of(a_row[h]), NT) for h in heads]
        v_hi = [v_row[h].astype(BF16) for h in heads]
        v_lo = [(v_row[h] - v_hi[h].astype(F32)).astype(BF16) for h in heads]
        v_bc = [_bdot(eye_b, rows_of(v_hi[h]), NT) + _bdot(eye_b, rows_of(v_lo[h]), NT) for h in heads]
        s_new = [s[h] * w_row[h] + sa[h] * b_row[h] + v_bc[h] * k_row[h] for h in heads]
        for h in heads:
            s_out_ref[b, h] = s_new[h]
        y_bc = [_bdot(s_new[h], rows_of(r_row[h]), NT) for h in heads]
        for h in heads:
            y_ref[b:b + 1, sls[h]] = jnp.sum(eye * y_bc[h], axis=0, keepdims=True)
    y = y_ref[...]
    rk = rk_ref[...]
    gg = gg_ref[...]
    gb = gb_ref[...]
    for h in range(H_A):
        sl = slice(h * HS, (h + 1) * HS)
        o_ref[:, sl] = _head_post(y[:, sl], r[:, sl], k2[:, sl], v[:, sl], z[:, sl], rk[:, sl], gg[:, sl],
                                  gb[:, sl]).astype(o_ref.dtype)


def _rwkv_sample(ha, prev, s_in, vfirst, p, bt=8):
    bsz = ha.shape[0]
    has_vgate = vfirst is not None
    row_spec = lambda w: pl.BlockSpec((bt, w), lambda i: (i, 0))
    full = lambda a: pl.BlockSpec(a.shape, lambda i: (0,) * a.ndim)
    st_spec = pl.BlockSpec((bt, H_A, HS, HS), lambda i: (i, 0, 0, 0))
    ins = [ha, prev, s_in]
    in_specs = [row_spec(A_COLS), row_spec(A_COLS), st_spec]
    if has_vgate:
        ins.append(vfirst)
        in_specs.append(row_spec(D_A))
    names = ['mu', 'w0', 'w2', 'a0', 'a2', 'kk', 'ka', 'rk', 'gn_g', 'gn_b'] + (['v0', 'v1', 'v2'] if has_vgate else [])
    for nme in names:
        ins.append(p[nme])
        in_specs.append(full(p[nme]))
    out_shape = [jax.ShapeDtypeStruct((bsz, D_A), BF16)]
    out_specs = [row_spec(D_A)]
    if not has_vgate:
        out_shape.append(jax.ShapeDtypeStruct((bsz, D_A), F32))
        out_specs.append(row_spec(D_A))
    out_shape.append(jax.ShapeDtypeStruct((bsz, H_A, HS, HS), F32))
    out_specs.append(st_spec)
    outs = pl.pallas_call(
        functools.partial(_rwkv_sample_kernel, has_vgate=has_vgate, bt=bt),
        grid=(bsz // bt,),
        in_specs=in_specs,
        out_specs=out_specs,
        out_shape=out_shape,
        scratch_shapes=[pltpu.VMEM((6, bt, D_A), F32), pltpu.VMEM((bt, D_A), F32)],
        compiler_params=_params(("parallel",)),
    )(*ins)
    if has_vgate:
        return outs[0], vfirst, outs[1]
    return outs[0], outs[1], outs[2]


def _compress_rows(load_rows, w1_ref, b1, w2, b2, kv, nch):
    accs = [jnp.zeros((nch, 2 * CMP_HID), F32) for _ in range(G_KV)]
    for tau in range(0, CMP_STRIDE, 2):
        for pair in range(G_KV // 2):
            rows = jnp.concatenate([load_rows(tau, pair), load_rows(tau + 1, pair)], axis=1).astype(BF16)
            for parity in range(2):
                g = 2 * pair + parity
                accs[g] = accs[g] + jnp.dot(rows, w1_ref[kv, tau // 2, parity], preferred_element_type=F32)
    outs = []
    for acc in accs:
        h = acc[:, :CMP_HID] + pltpu.roll(acc[:, CMP_HID:], nch - 1, axis=0) + b1
        outs.append(_bdot(_gelu_tanh(h), w2) + b2)
    return outs


def _kv_prep_kernel(kc_ref, ks_ref, kw_ref, w1_ref, b1_ref, w2_ref, b2_ref,
                    ks_o, vs_o, kw_o, vw_o, kc_o, vc_o, *, nch):
    npair = KV_W // LANES

    def put_transposed(out, pair, x):
        xt = x.T
        out[0, 2 * pair] = xt[0:HD].astype(BF16)
        out[0, 2 * pair + 1] = xt[HD:2 * HD].astype(BF16)

    for g in range(G_KV):
        sl = slice(g * HD, (g + 1) * HD)
        ks_o[0, g] = ks_ref[:, sl].astype(BF16)
        kw_o[0, g] = kw_ref[:, sl].astype(BF16)
    for pair in range(npair):
        sl2 = slice(KV_W + pair * LANES, KV_W + (pair + 1) * LANES)
        put_transposed(vs_o, pair, ks_ref[:, sl2])
        put_transposed(vw_o, pair, kw_ref[:, sl2])
    for kv in (0, 1):
        load = lambda tau, pair, kv=kv: kc_ref[pl.ds(2 * npair * tau + kv * npair + pair, nch,
                                                     stride=2 * npair * CMP_STRIDE), :]
        res = _compress_rows(load, w1_ref, b1_ref[kv], w2_ref[kv], b2_ref[kv], kv, nch)
        if kv == 0:
            for g in range(G_KV):
                kc_o[0, g] = res[g].astype(BF16)
        else:
            for pair in range(npair):
                put_transposed(vc_o, pair, jnp.concatenate([res[2 * pair], res[2 * pair + 1]], axis=1))


def _kv_prep(hb, hc, cp, bsz, t):
    nch = t // CMP_STRIDE
    blk = lambda j: pl.BlockSpec((t, 2 * KV_W), lambda b: (b, j))
    full = lambda a: pl.BlockSpec(a.shape, lambda b: (0,) * a.ndim)
    c0 = D_B // (2 * KV_W)
    lane_rows = 2 * KV_W // LANES
    hc = hc.reshape(bsz * t * lane_rows, LANES)
    def arr(n, transposed):
        shp = (G_KV, HD, n) if transposed else (G_KV, n, HD)
        return jax.ShapeDtypeStruct((bsz,) + shp, BF16), pl.BlockSpec((1,) + shp, lambda b: (b, 0, 0, 0))

    outs = [arr(t, False), arr(t, True), arr(t, False), arr(t, True), arr(nch, False), arr(nch, True)]
    return pl.pallas_call(
        functools.partial(_kv_prep_kernel, nch=nch),
        grid=(bsz,),
        in_specs=[pl.BlockSpec((t * lane_rows, LANES), lambda b: (b, 0)), blk(c0 + 1), blk(c0 + 2),
                  full(cp['w1']), full(cp['b1']), full(cp['w2']), full(cp['b2'])],
        out_specs=[o[1] for o in outs],
        out_shape=[o[0] for o in outs],
        compiler_params=_params(("parallel",)),
    )(hc, hb, hb, cp['w1'], cp['b1'], cp['w2'], cp['b2'])


def _softmax_pieces(qst, pieces, state):
    ss = []
    for k, vt, bias, mask in pieces:
        s = jnp.dot(k, qst, preferred_element_type=F32) + bias
        ss.append(s if mask is None else jnp.where(mask, s, NEG))
    m_new = jnp.max(ss[0], axis=0, keepdims=True)
    for s in ss[1:]:
        m_new = jnp.maximum(m_new, jnp.max(s, axis=0, keepdims=True))
    if state is not None:
        m_new = jnp.maximum(m_new, state[0])
    ps = [jnp.exp2(s - m_new) for s in ss]
    l_new = jnp.sum(ps[0], axis=0, keepdims=True)
    for p in ps[1:]:
        l_new = l_new + jnp.sum(p, axis=0, keepdims=True)
    p_all = jnp.concatenate([p.astype(BF16) for p in ps], axis=0)
    vt_all = jnp.concatenate([vt for _, vt, _, _ in pieces], axis=1)
    acc_new = jnp.dot(vt_all, p_all, preferred_element_type=F32)
    if state is not None:
        alpha = jnp.exp2(state[0] - m_new)
        l_new = l_new + alpha * state[1]
        acc_new = acc_new + alpha * state[2]
    return m_new, l_new, acc_new


def _nsa_prompt_kernel(q_ref, gl_ref, zb_ref, kc_ref, vc_ref, ks_ref, vs_ref, kw_ref, vw_ref,
                       bc_ref, d0_ref, d1_ref, far_ref, ov_ref, ex_ref, o_ref, sel_ref, *, t, nch):
    g = pl.program_id(1)
    qt = pl.program_id(2)
    nsel = t // SEL_LEN
    nkb = t // KB
    nwin = WIN // KB
    qt_t = (q_ref[...] * (SCALE * LOG2E)).T
    qst = jnp.concatenate([qt_t[r * HD:(r + 1) * HD, :] for r in range(R_Q)], axis=1).astype(BF16)
    qpos = qt * TQ + lax.broadcasted_iota(jnp.int32, (1, TQ), 1)
    rep = lambda x: jnp.concatenate([x] * R_Q, axis=1)

    ik = lax.broadcasted_iota(jnp.int32, (KB, TQ), 0)
    iq = lax.broadcasted_iota(jnp.int32, (KB, TQ), 1)
    causal = ik <= iq
    bias_d0 = jnp.concatenate([d0_ref[r] for r in range(R_Q)], axis=1)
    bias_d1 = jnp.concatenate([d1_ref[r] for r in range(R_Q)], axis=1)
    far_row = jnp.concatenate([jnp.full((1, TQ), far_ref[g * R_Q + r], F32) for r in range(R_Q)], axis=1)
    penalty = lambda valid: jnp.where(valid, 0.0, NEG)

    def kv(kref, vref, kb):
        off = pl.multiple_of(jnp.clip(kb, 0, nkb - 1) * KB, KB)
        return kref[0, 0, pl.ds(off, KB), :], vref[0, 0, :, pl.ds(off, KB)]

    pieces = [kv(kw_ref, vw_ref, qt) + (bias_d0, rep(causal)),
              kv(kw_ref, vw_ref, qt - 1) + (bias_d1 + penalty(qt >= 1), None)]
    for j in range(2, nwin):
        pieces.append(kv(kw_ref, vw_ref, qt - j) + (far_row + penalty(qt >= j), None))
    pieces.append(kv(kw_ref, vw_ref, qt - nwin) + (far_row + penalty(qt >= nwin), rep(ik > iq)))
    _, l_w, acc_w = _softmax_pieces(qst, pieces, None)
    o_w = acc_w / l_w

    nrow = lax.broadcasted_iota(jnp.int32, (nch, TQ), 0)
    mask_c = (qpos >= nrow * CMP_STRIDE + (CMP_LEN - 1)) & (nrow < nch - 1)
    mask_c4 = rep(mask_c)
    bias_c = jnp.concatenate([bc_ref[r] for r in range(R_Q)], axis=1)
    s = jnp.dot(kc_ref[0, 0], qst, preferred_element_type=F32) + bias_c
    s = jnp.where(mask_c4, s, NEG)
    p = jnp.where(mask_c4, jnp.exp2(s - jnp.max(s, axis=0, keepdims=True)), 0.0)
    l = jnp.sum(p, axis=0, keepdims=True)
    p = p / jnp.where(l > 0.0, l, 1.0)
    o_c = jnp.dot(vc_ref[0, 0], p.astype(BF16), preferred_element_type=F32)
    psum = p[:, 0:TQ]
    for r in range(1, R_Q):
        psum = psum + p[:, r * TQ:(r + 1) * TQ]
    p_hi = psum.astype(BF16)
    p_lo = (psum - p_hi.astype(F32)).astype(BF16)
    ov = ov_ref[...]
    imp = jnp.dot(ov, p_hi, preferred_element_type=F32) + jnp.dot(ov, p_lo, preferred_element_type=F32)
    blk = lax.broadcasted_iota(jnp.int32, (nsel, TQ), 0)
    cur = qpos // SEL_LEN
    imp = jnp.where(blk * SEL_LEN <= qpos, imp, NEG)
    imp = jnp.where((blk == 0) | (blk == cur) | (blk == cur - 1), -NEG, imp)
    rank = jnp.zeros((nsel, TQ), F32)
    for s2 in range(nsel):
        other = imp[s2:s2 + 1, :]
        rank = rank + jnp.where((other > imp) | ((other == imp) & (blk > s2)), 1.0, 0.0)
    sel = jnp.where(rank < float(min(N_TOP, nsel)), 1.0, 0.0).astype(BF16)
    sel_ref[...] = jnp.dot(ex_ref[...], sel, preferred_element_type=F32)

    def sel_mask(kb, extra=None):
        off = pl.multiple_of(jnp.clip(kb, 0, nkb - 1) * KB, KB)
        mask = sel_ref[pl.ds(off, KB), :] > 0.5
        return rep(mask if extra is None else mask & extra)

    state = _softmax_pieces(qst, [kv(ks_ref, vs_ref, qt) + (bias_d0, sel_mask(qt, causal)),
                                  kv(ks_ref, vs_ref, qt - 1) + (bias_d1 + penalty(qt >= 1), sel_mask(qt - 1))], None)
    nfar = jnp.maximum(qt - 1, 0)

    def far_sel(gi, state):
        pieces = []
        for j in range(FAR_GROUP):
            kb = gi * FAR_GROUP + j
            pieces.append(kv(ks_ref, vs_ref, kb) + (far_row + penalty(kb < nfar), sel_mask(kb)))
        return _softmax_pieces(qst, pieces, state)

    _, l_s, acc_s = lax.fori_loop(0, (nfar + FAR_GROUP - 1) // FAR_GROUP, far_sel, state)
    o_s = acc_s / l_s

    gate = _sigmoid(gl_ref[...]).T
    outs = []
    for r in range(R_Q):
        cs = slice(r * TQ, (r + 1) * TQ)
        outs.append(gate[3 * r:3 * r + 1, :] * o_c[:, cs] + gate[3 * r + 1:3 * r + 2, :] * o_s[:, cs]
                    + gate[3 * r + 2:3 * r + 3, :] * o_w[:, cs])
    o = jnp.concatenate(outs, axis=0).T
    o_ref[...] = (o * _silu(zb_ref[...])).astype(o_ref.dtype)


def _nsa_prompt(hb, kvp, tabs, bsz, t):
    ks_t, vs_t, kw_t, vw_t, kc_t, vc_t = kvp
    nch = t // CMP_STRIDE
    nqt = t // TQ
    gw = R_Q * HD
    k_spec = lambda n: pl.BlockSpec((1, 1, n, HD), lambda b, g, i: (b, g, 0, 0))
    vt_spec = lambda n: pl.BlockSpec((1, 1, HD, n), lambda b, g, i: (b, g, 0, 0))
    gl0 = (D_B + 6 * KV_W) // GL_PAD
    zb0 = (D_B + 6 * KV_W + G_KV * GL_PAD) // gw
    return pl.pallas_call(
        functools.partial(_nsa_prompt_kernel, t=t, nch=nch),
        grid=(bsz, G_KV, nqt),
        in_specs=[pl.BlockSpec((TQ, gw), lambda b, g, i: (b * nqt + i, g)),
                  pl.BlockSpec((TQ, GL_PAD), lambda b, g, i: (b * nqt + i, gl0 + g)),
                  pl.BlockSpec((TQ, gw), lambda b, g, i: (b * nqt + i, zb0 + g)),
                  k_spec(nch), vt_spec(nch), k_spec(t), vt_spec(t), k_spec(t), vt_spec(t),
                  pl.BlockSpec((R_Q, nch, TQ), lambda b, g, i: (g, 0, i)),
                  pl.BlockSpec((R_Q, KB, TQ), lambda b, g, i: (g, 0, 0)),
                  pl.BlockSpec((R_Q, KB, TQ), lambda b, g, i: (g, 0, 0)),
                  pl.BlockSpec(memory_space=pltpu.SMEM),
                  pl.BlockSpec(tabs['overlap'].shape, lambda b, g, i: (0, 0)),
                  pl.BlockSpec(tabs['expand'].shape, lambda b, g, i: (0, 0))],
        out_specs=pl.BlockSpec((TQ, gw), lambda b, g, i: (b * nqt + i, g)),
        out_shape=jax.ShapeDtypeStruct((bsz * t, D_B), BF16),
        scratch_shapes=[pltpu.VMEM((t, TQ), F32)],
        compiler_params=_params(("parallel", "parallel", "arbitrary")),
    )(hb, hb, hb, kc_t, vc_t, ks_t, vs_t, kw_t, vw_t, tabs['bias_c'], tabs['d0'], tabs['d1'], tabs['far'],
      tabs['overlap'], tabs['expand'])


def _row_softmax(s, mask, s_new):
    sm = jnp.where(mask, s, NEG)
    m = jnp.maximum(jnp.max(sm, axis=-1, keepdims=True), s_new)
    p = jnp.where(mask, jnp.exp(sm - m), 0.0)
    p_new = jnp.exp(s_new - m)
    l = jnp.sum(p, axis=-1, keepdims=True) + p_new
    return p / l, p_new / l


def _nsa_sample_kernel(pt_ref, *refs, n_pages, past):
    page_refs = refs[:n_pages]
    (win_ref, qbd_ref, new_ref, gl_ref, zb_ref, w1_ref, b1_ref, w2_ref, b2_ref,
     bc_ref, bs_ref, bw_ref, b0_ref, ov_ref, gsum_ref, bdm_ref, ex_ref, o_ref, x_s, kc_s, vc_s) = refs[n_pages:]
    del pt_ref
    nch = past // CMP_STRIDE
    nsel = past // SEL_LEN + 1
    qbd = qbd_ref[0]
    qbd_f = qbd.astype(F32)
    npair = KV_W // LANES

    for pi, pr in enumerate(page_refs):
        for kv in range(2):
            for pair in range(npair):
                x_s[kv, pair, pi * PAGE_SIZE:(pi + 1) * PAGE_SIZE, :] = pr[0, 0, kv, pair * LANES:(pair + 1) * LANES, :].T
    for kv, dst in ((0, kc_s), (1, vc_s)):
        load = lambda tau, pair, kv=kv: x_s[kv, pair, pl.ds(tau, nch, stride=CMP_STRIDE), :]
        res = _compress_rows(load, w1_ref, b1_ref[kv], w2_ref[kv], b2_ref[kv], kv, nch)
        for g in range(G_KV):
            dst[:, g * HD:(g + 1) * HD] = res[g]
    ncol = lax.broadcasted_iota(jnp.int32, (H_B, nch), 1)
    s_c = _bdot(qbd, kc_s[...], NT) + bc_ref[...]
    mask_c = ncol < nch - 1
    sm = jnp.where(mask_c, s_c, NEG)
    p_c = jnp.where(mask_c, jnp.exp(sm - jnp.max(sm, axis=-1, keepdims=True)), 0.0)
    p_c = p_c / jnp.sum(p_c, axis=-1, keepdims=True)
    o_c = _bdot(p_c, vc_s[...])
    p_hi = p_c.astype(BF16)
    p_lo = (p_c - p_hi.astype(F32)).astype(BF16)
    ov = ov_ref[...]
    imp = jnp.dot(p_hi, ov, preferred_element_type=F32) + jnp.dot(p_lo, ov, preferred_element_type=F32)
    i_hi = imp.astype(BF16)
    i_lo = (imp - i_hi.astype(F32)).astype(BF16)
    gs = gsum_ref[...]
    imp = jnp.dot(gs, i_hi, preferred_element_type=F32) + jnp.dot(gs, i_lo, preferred_element_type=F32)
    nsp = imp.shape[1]
    blk = lax.broadcasted_iota(jnp.int32, (H_B, nsp), 1)
    cur = past // SEL_LEN
    imp = jnp.where((blk == 0) | (blk == cur) | (blk == cur - 1), -NEG, imp)
    imp = jnp.where(blk < nsel, imp, 2.0 * NEG)
    rank = jnp.zeros((H_B, nsp), F32)
    for s2 in range(nsel):
        other = imp[:, s2:s2 + 1]
        rank = rank + jnp.where((other > imp) | ((other == imp) & (blk > s2)), 1.0, 0.0)
    sel = jnp.where(rank < float(N_TOP), 1.0, 0.0).astype(BF16)
    mask_s = jnp.dot(sel, ex_ref[...], preferred_element_type=F32) > 0.5

    new = new_ref[0]
    ks_new = new[:, 2 * KV_W:3 * KV_W]
    vs_new = new[:, 3 * KV_W:4 * KV_W]
    kw_new = new[:, 4 * KV_W:5 * KV_W]
    vw_new = new[:, 5 * KV_W:6 * KV_W]
    b0 = b0_ref[...]
    s_s = jnp.concatenate([_bdot(qbd, pr[0, 0, 2]) for pr in page_refs], axis=1) + bs_ref[...]
    s_new = jnp.sum(qbd_f * ks_new, axis=-1, keepdims=True) + b0
    p_s, p_new = _row_softmax(s_s, mask_s, s_new)
    o_s = p_new * vs_new
    for pi, pr in enumerate(page_refs):
        o_s = o_s + _bdot(p_s[:, pi * PAGE_SIZE:(pi + 1) * PAGE_SIZE], pr[0, 0, 3], NT)
    nbuf = win_ref.shape[-1]
    wcol = lax.broadcasted_iota(jnp.int32, (H_B, nbuf), 1)
    s_w = _bdot(qbd, win_ref[0, 0, 0]) + bw_ref[...]
    s_wn = jnp.sum(qbd_f * kw_new, axis=-1, keepdims=True) + b0
    p_w, p_wn = _row_softmax(s_w, wcol >= nbuf + 1 - WIN, s_wn)
    o_w = _bdot(p_w, win_ref[0, 0, 1], NT) + p_wn * vw_new
    gate = _sigmoid(gl_ref[0])
    o = gate[:, 0:1] * o_c + gate[:, 1:2] * o_s + gate[:, 2:3] * o_w
    o = o * bdm_ref[...]
    o16 = o[:, 0:HD]
    for g in range(1, G_KV):
        o16 = o16 + o[:, g * HD:(g + 1) * HD]
    o_ref[0] = (o16 * _silu(zb_ref[0])).astype(o_ref.dtype)


def _nsa_sample(cache_l, l, page_table, win, qbd, new_rows, gl, zb, cp, tabs):
    bsz, n_pages = page_table.shape
    past = n_pages * PAGE_SIZE
    nch = past // CMP_STRIDE
    nbuf = win.shape[-1]
    full = lambda a: pl.BlockSpec(a.shape, lambda b, pt: (0,) * a.ndim)
    page_specs = [pl.BlockSpec((1, 1) + cache_l.shape[2:],
                               functools.partial(lambda b, pt, j: (l, pt[b, j], 0, 0, 0), j=j))
                  for j in range(n_pages)]
    consts = [cp['w1'], cp['b1'], cp['w2'], cp['b2'], tabs['bias_c'], tabs['bias_s'], tabs['bias_w'], tabs['bias_0'],
              tabs['overlap'], tabs['gsum'], tabs['bdmask'], tabs['expand']]
    grid_spec = pltpu.PrefetchScalarGridSpec(
        num_scalar_prefetch=1,
        grid=(bsz,),
        in_specs=page_specs + [
            pl.BlockSpec((1, 1, 2, KV_W, nbuf), lambda b, pt: (l, b, 0, 0, 0)),
            pl.BlockSpec((1, H_B, KV_W), lambda b, pt: (b, 0, 0)),
            pl.BlockSpec((1, 1, 6 * KV_W), lambda b, pt: (b, 0, 0)),
            pl.BlockSpec((1, H_B, 3), lambda b, pt: (b, 0, 0)),
            pl.BlockSpec((1, H_B, HD), lambda b, pt: (b, 0, 0)),
        ] + [full(a) for a in consts],
        out_specs=pl.BlockSpec((1, H_B, HD), lambda b, pt: (b, 0, 0)),
        scratch_shapes=[pltpu.VMEM((2, KV_W // LANES, past, LANES), F32), pltpu.VMEM((nch, KV_W), F32),
                        pltpu.VMEM((nch, KV_W), F32)],
    )
    return pl.pallas_call(
        functools.partial(_nsa_sample_kernel, n_pages=n_pages, past=past),
        grid_spec=grid_spec,
        out_shape=jax.ShapeDtypeStruct((bsz, H_B, HD), BF16),
        compiler_params=_params(("arbitrary",)),
    )(page_table, *([cache_l] * n_pages), win, qbd, new_rows, gl, zb, *consts)


def _win_update_kernel(win_ref, new_ref, o_ref):
    nbuf = win_ref.shape[-1]
    ri = lax.broadcasted_iota(jnp.int32, (KV_W, KV_W), 0)
    ci = lax.broadcasted_iota(jnp.int32, (KV_W, KV_W), 1)
    eye = jnp.where(ri == ci, 1.0, 0.0).astype(BF16)
    lane = lax.broadcasted_iota(jnp.int32, (KV_W, nbuf), 1)
    for c in range(2):
        rem = new_ref[0, 0, :, c * KV_W:(c + 1) * KV_W]
        col = jnp.zeros((KV_W, LANES), F32)
        for _ in range(3):
            part = rem.astype(BF16)
            rem = rem - part.astype(F32)
            col = col + lax.dot_general(eye, jnp.broadcast_to(part, (LANES, KV_W)), NT, preferred_element_type=F32)
        col = jnp.concatenate([col] * (nbuf // LANES), axis=1)
        o_ref[0, 0, c] = jnp.where(lane == nbuf - 1, col, pltpu.roll(win_ref[0, 0, c], nbuf - 1, axis=1))


def _win_update(win_l, new_win):
    depth, bsz = win_l.shape[:2]
    blk = (1, 1) + win_l.shape[2:]
    return pl.pallas_call(
        _win_update_kernel,
        grid=(depth, bsz),
        in_specs=[pl.BlockSpec(blk, lambda l, b: (l, b, 0, 0, 0)),
                  pl.BlockSpec((1, 1, 1, 2 * KV_W), lambda l, b: (l, b, 0, 0))],
        out_specs=pl.BlockSpec(blk, lambda l, b: (l, b, 0, 0, 0)),
        out_shape=jax.ShapeDtypeStruct(win_l.shape, F32),
        compiler_params=_params(("parallel", "parallel")),
    )(win_l, new_win)


def _t5_bucket(dist):
    n = jnp.maximum(dist, 0)
    max_exact = NUM_BUCKETS // 2
    nf = jnp.maximum(n, 1).astype(F32)
    large = max_exact + (jnp.log(nf / max_exact) / math.log(MAX_DIST / max_exact)
                         * (NUM_BUCKETS - max_exact)).astype(jnp.int32)
    large = jnp.minimum(large, NUM_BUCKETS - 1)
    return jnp.where(n < max_exact, n, large)


def _bias_of(rel_bias, dist):
    return jnp.moveaxis(rel_bias[_t5_bucket(dist)], -1, 0).astype(F32)


def _overlap(nch, nsel):
    ci = np.arange(nch)[:, None] * CMP_STRIDE
    sj = np.arange(nsel)[None, :] * SEL_LEN
    ov = ((ci < sj + SEL_LEN) & (ci + CMP_LEN > sj)).astype(np.float32)
    ov[nch - 1:, :] = 0.0
    return ov


def _skew(g, n, step, length):
    h, L = g.shape
    flat = jnp.tile(g, (1, n))[:, :n * (L - step)]
    return flat.reshape(h, n, L - step)[:, :, :length]


def _prompt_tables(rel_bias, t):
    nch = t // CMP_STRIDE
    nsel = t // SEL_LEN
    f = _bias_of(rel_bias, jnp.arange(max(t, 2 * KB), dtype=jnp.int32)) * LOG2E
    f0 = lambda n: jnp.broadcast_to(f[:, :1], (H_B, n))
    lead = CMP_LEN - 1
    g_c = jnp.concatenate([f0(lead), f[:, :t - lead], f0(CMP_STRIDE * nch)], axis=1)
    g_0 = jnp.concatenate([f[:, :KB], f0(KB)], axis=1)
    g_1 = jnp.concatenate([f[:, KB:2 * KB], f[:, :KB]], axis=1)
    expand = (np.arange(t)[:, None] // SEL_LEN == np.arange(nsel)[None, :]).astype(np.float32)
    return {
        'bias_c': _skew(g_c, nch, CMP_STRIDE, t),
        'd0': _skew(g_0, KB, 1, TQ),
        'd1': _skew(g_1, KB, 1, TQ),
        'far': rel_bias[NUM_BUCKETS - 1].astype(F32) * LOG2E,
        'overlap': jnp.asarray(_overlap(nch, nsel).T, BF16),
        'expand': jnp.asarray(expand, BF16),
    }


def _sample_tables(rel_bias, past, nbuf):
    nch = past // CMP_STRIDE
    nsel = past // SEL_LEN + 1
    nsp = -(-nsel // LANES) * LANES
    cmp_end = jnp.arange(nch, dtype=jnp.int32) * CMP_STRIDE + CMP_LEN - 1
    ov = np.zeros((nch, nsp), np.float32)
    ov[:, :nsel] = _overlap(nch, nsel)
    hh = np.arange(H_B)
    gsum = (hh[:, None] // R_Q == hh[None, :] // R_Q).astype(np.float32)
    bdm = (hh[:, None] // R_Q == np.arange(KV_W)[None, :] // HD).astype(np.float32)
    expand = (np.arange(nsp)[:, None] == np.arange(past)[None, :] // SEL_LEN).astype(np.float32)
    return {
        'bias_c': _bias_of(rel_bias, past - cmp_end),
        'bias_s': _bias_of(rel_bias, past - jnp.arange(past, dtype=jnp.int32)),
        'bias_w': _bias_of(rel_bias, nbuf - jnp.arange(nbuf, dtype=jnp.int32)),
        'bias_0': _bias_of(rel_bias, jnp.zeros((1,), jnp.int32)),
        'overlap': jnp.asarray(ov, BF16),
        'gsum': jnp.asarray(gsum, BF16),
        'bdmask': jnp.asarray(bdm, F32),
        'expand': jnp.asarray(expand, BF16),
    }


def _layer_params(l, w_in, mu_shift, rw_w0, rw_w2, rw_a0, rw_a2, rw_kk, rw_ka, rw_rk, rw_gn_g, rw_gn_b,
                  rw_v0, rw_v1, rw_v2, cmp_w1, cmp_b1, cmp_w2, cmp_b2, w_up_a, w_up_b, w_out, ln_g, ln_b):
    w = w_in[l]
    b0 = A_COLS
    q_kv = w[:, b0:b0 + D_B + 6 * KV_W]
    gl = w[:, b0 + D_B + 6 * KV_W:b0 + D_B + 6 * KV_W + 3 * H_B].reshape(D_MODEL, G_KV, 3 * R_Q)
    gl = jnp.pad(gl, ((0, 0), (0, 0), (0, GL_PAD - 3 * R_Q))).reshape(D_MODEL, G_KV * GL_PAD)
    zb = w[:, b0 + D_B + 6 * KV_W + 3 * H_B:b0 + D_B + 6 * KV_W + 3 * H_B + D_B]
    half = CMP_STRIDE * HD
    w1 = cmp_w1[l]
    w1r = jnp.concatenate([w1[:, :half].reshape(2, CMP_STRIDE, HD, CMP_HID),
                           w1[:, half:].reshape(2, CMP_STRIDE, HD, CMP_HID)], axis=-1)
    zero = jnp.zeros_like(w1r)
    w1r = jnp.stack([jnp.concatenate([w1r, zero], axis=2), jnp.concatenate([zero, w1r], axis=2)], axis=2)
    w1r = w1r.reshape(2, CMP_STRIDE // 2, 2, 2, LANES, 2 * CMP_HID).transpose(0, 1, 3, 2, 4, 5)
    w1r = w1r.reshape(2, CMP_STRIDE // 2, 2, 2 * LANES, 2 * CMP_HID)
    row = lambda a: a.reshape(1, -1).astype(F32)
    p = {
        'wa': w[:, :A_COLS].astype(BF16),
        'wb': jnp.concatenate([q_kv, gl, zb], axis=1).astype(BF16),
        'wg': w[:, b0 + D_B + 6 * KV_W + 3 * H_B + D_B:].astype(BF16),
        'wc': w[:, b0 + D_B:b0 + D_B + 2 * KV_W].astype(BF16),
        'mu': row(mu_shift[l]), 'w0': row(rw_w0[l]), 'w2': rw_w2[l].astype(BF16), 'a0': row(rw_a0[l]),
        'a2': rw_a2[l].astype(BF16), 'kk': row(rw_kk[l]), 'ka': row(rw_ka[l]), 'rk': row(rw_rk[l]),
        'gn_g': row(rw_gn_g[l]), 'gn_b': row(rw_gn_b[l]),
        'cmp': {'w1': w1r.astype(BF16), 'b1': cmp_b1[l].reshape(2, 1, CMP_HID).astype(F32),
                'w2': cmp_w2[l].astype(BF16), 'b2': cmp_b2[l].reshape(2, 1, HD).astype(F32)},
        'w_up_a': w_up_a[l].astype(BF16), 'w_up_b': w_up_b[l].astype(BF16), 'w_out': w_out[l].astype(BF16),
        'ln_g': row(ln_g[l]), 'ln_b': row(ln_b[l]),
    }
    if l > 0:
        p['v0'] = row(rw_v0[l - 1])
        p['v1'] = rw_v1[l - 1].astype(BF16)
        p['v2'] = rw_v2[l - 1].astype(BF16)
    return p


def _project(x2, p):
    m = x2.shape[0]
    ha = _matmul(x2, p['wa'], PROJ_TM, A_COLS // 3)
    hb = _matmul(x2, p['wb'], PROJ_TM, PROJ_TN)
    hg = _matmul(x2, p['wg'], PROJ_TM, PROJ_TN)
    return ha, hb, hg


def _finish(x2, o_a, o_b, hg, p):
    merged = _up_gate(o_a, o_b, p['w_up_a'], p['w_up_b'], hg)
    return _out_ln(merged, p['w_out'], x2, p['ln_g'], p['ln_b'])


def _prompt_layer(x2, vfirst, p, tabs, bsz, t):
    ha, hb, hg = _project(x2, p)
    o_a, vfirst, wkv = _rwkv_prompt(ha, vfirst, p, bsz, t)
    hc = _matmul(x2, p['wc'], PROJ_TM, 2 * KV_W)
    kvp = _kv_prep(hb, hc, p['cmp'], bsz, t)
    o_b = _nsa_prompt(hb, kvp, tabs, bsz, t)
    y = _finish(x2, o_a, o_b, hg, p)
    kvc = D_B
    nwin = min(WIN, t)
    to_rows = lambda x, n: jnp.transpose(x.reshape(bsz, n, G_KV, HD, x.shape[-1]), (0, 4, 1, 2, 3))
    new_rows = to_rows(_cols_transposed(hb, bsz, t, kvc, 4 * KV_W, t), 4)
    win_state = to_rows(_cols_transposed(hb, bsz, t, kvc + 4 * KV_W, 2 * KV_W, nwin), 2)
    shift = ha.reshape(bsz, t, A_COLS)[:, t - 1]
    return y, vfirst, (new_rows, win_state, wkv, shift)


def _sample_layer(x2, vfirst, l, p, tabs, cache_l, win_l, cache_win_kv, state_wkv, state_shift, page_table):
    bsz = x2.shape[0]
    ha, hb, hg = _project(x2, p)
    o_a, vfirst, wkv = _rwkv_sample(ha, state_shift[l], state_wkv[l], vfirst, p)
    kvc = D_B
    q = hb[:, :D_B].reshape(bsz, G_KV, R_Q, HD) * SCALE
    eye = jnp.eye(G_KV, dtype=F32)
    qbd = (q[:, :, :, None, :] * eye[None, :, None, :, None]).reshape(bsz, H_B, KV_W).astype(BF16)
    new6 = hb[:, kvc:kvc + 6 * KV_W]
    gl0 = kvc + 6 * KV_W
    gl = hb[:, gl0:gl0 + G_KV * GL_PAD].reshape(bsz, G_KV, GL_PAD)[:, :, :3 * R_Q].reshape(bsz, H_B, 3)
    zb = hb[:, gl0 + G_KV * GL_PAD:].reshape(bsz, H_B, HD)
    o_b = _nsa_sample(cache_l, l, page_table, win_l, qbd, new6[:, None, :], gl, zb, p['cmp'], tabs)
    y = _finish(x2, o_a, o_b.reshape(bsz, D_B), hg, p)
    new_rows = new6[:, :4 * KV_W].reshape(bsz, 1, 4, G_KV, HD)
    new_win = new6[:, None, 4 * KV_W:]
    return y, vfirst, (new_rows, new_win, wkv, ha)


def kernel(x_prompt, x_sample, cache_kv, cache_win_kv, state_wkv, state_shift, page_table, w_in, mu_shift, rw_w0, rw_w2, rw_a0, rw_a2, rw_kk, rw_ka, rw_rk, rw_gn_g, rw_gn_b, rw_v0, rw_v1, rw_v2, cmp_w1, cmp_b1, cmp_w2, cmp_b2, rel_bias, w_up_a, w_up_b, w_out, ln_g, ln_b):
    bsz, t, _ = x_prompt.shape
    dec_b = x_sample.shape[0]
    n_pages = page_table.shape[1]
    depth, n_phys = cache_kv.shape[:2]
    cache_l = jnp.transpose(cache_kv, (0, 1, 3, 4, 5, 2)).reshape(depth, n_phys, 4, KV_W, PAGE_SIZE)
    win_l = jnp.transpose(cache_win_kv, (0, 1, 3, 4, 5, 2)).reshape(depth, dec_b, 2, KV_W, cache_win_kv.shape[2])
    tabs_p = _prompt_tables(rel_bias, t)
    tabs_s = _sample_tables(rel_bias, n_pages * PAGE_SIZE, cache_win_kv.shape[2])
    y_p = x_prompt.reshape(bsz * t, D_MODEL)
    y_s = x_sample.reshape(dec_b, D_MODEL)
    vf_p, vf_s = None, None
    st_p, st_s = [], []
    for l in range(depth):
        p = _layer_params(l, w_in, mu_shift, rw_w0, rw_w2, rw_a0, rw_a2, rw_kk, rw_ka, rw_rk, rw_gn_g, rw_gn_b,
                          rw_v0, rw_v1, rw_v2, cmp_w1, cmp_b1, cmp_w2, cmp_b2, w_up_a, w_up_b, w_out, ln_g, ln_b)
        y_p, vf_p, sp = _prompt_layer(y_p, vf_p, p, tabs_p, bsz, t)
        y_s, vf_s, ss = _sample_layer(y_s, vf_s, l, p, tabs_s, cache_l, win_l, cache_win_kv, state_wkv, state_shift,
                                      page_table)
        st_p.append(sp)
        st_s.append(ss)
    stack = lambda st, i: jnp.stack([s[i] for s in st])
    nbuf = cache_win_kv.shape[2]
    win_next = _win_update(win_l, stack(st_s, 1)).reshape(depth, dec_b, 2, G_KV, HD, nbuf)
    win_next = jnp.transpose(win_next, (0, 1, 5, 2, 3, 4))
    return (y_p.reshape(bsz, t, D_MODEL), y_s.reshape(dec_b, 1, D_MODEL),
            stack(st_p, 0), stack(st_p, 1), stack(st_p, 2), stack(st_p, 3),
            stack(st_s, 0), win_next, stack(st_s, 2), stack(st_s, 3))
```

```python
import functools
import math

import numpy as np
import jax
import jax.numpy as jnp
from jax import lax
from jax.experimental import pallas as pl
from jax.experimental.pallas import tpu as pltpu

D_MODEL = 2048
DEPTH = 2
PAGE_SIZE = 128
HS = 64
D_A = D_MODEL // 2
H_A = D_A // HS
R_W = 64
R_A = 64
R_V = 32
GN_EPS = 64e-5
HD = 64
D_B = D_MODEL // 2
H_B = D_B // HD
G_KV = 4
R_Q = H_B // G_KV
KV_W = G_KV * HD
CMP_LEN = 32
CMP_STRIDE = 16
CMP_HID = 128
SEL_LEN = 64
N_TOP = 16
WIN = 512
NUM_BUCKETS = 32
MAX_DIST = 128
SCALE = HD ** -0.5
A_COLS = 4 * D_A + R_W + R_A
ALPHA = (2 * DEPTH) ** 0.25
LN_EPS = 1e-5
NEG = -1e30
LOG2E = 1.4426950408889634

F32 = jnp.float32
BF16 = jnp.bfloat16

LANES = 128
VMEM_LIMIT = 56 * 1024 * 1024
CHUNK = 64
RWKV_ROWS = 1
UNIT_GROUP = 16
TQ = 128
KB = 128
FAR_GROUP = 4
PROJ_TM = 1024
PROJ_TN = 1024
UP_TM = 512
LN_TM = 512
GL_PAD = LANES
HB_COLS = D_B + 6 * KV_W + G_KV * GL_PAD + D_B

NT = (((1,), (1,)), ((), ()))
TN = (((0,), (0,)), ((), ()))


def _params(sem):
    return pltpu.CompilerParams(dimension_semantics=sem, vmem_limit_bytes=VMEM_LIMIT)


def _bdot(a, b, dims=None):
    a = a.astype(BF16)
    b = b.astype(BF16)
    if dims is None:
        return jnp.dot(a, b, preferred_element_type=F32)
    return lax.dot_general(a, b, dims, preferred_element_type=F32)


def _sigmoid(x):
    return 1.0 / (1.0 + jnp.exp(-x))


def _silu(x):
    return x * _sigmoid(x)


def _gelu_tanh(x):
    return 0.5 * x * (1.0 + jnp.tanh(math.sqrt(2.0 / math.pi) * (x + 0.044715 * (x * x * x))))


def _mm_kernel(x_ref, w_ref, o_ref):
    o_ref[...] = _bdot(x_ref[...], w_ref[...])


def _matmul(x, w, tm, tn):
    m, k = x.shape
    n = w.shape[1]
    tm = min(tm, m)
    return pl.pallas_call(
        _mm_kernel,
        grid=(m // tm, n // tn),
        in_specs=[pl.BlockSpec((tm, k), lambda i, j: (i, 0)),
                  pl.BlockSpec((k, tn), lambda i, j: (0, j))],
        out_specs=pl.BlockSpec((tm, tn), lambda i, j: (i, j)),
        out_shape=jax.ShapeDtypeStruct((m, n), F32),
        compiler_params=_params(("parallel", "parallel")),
    )(x, w)


def _transpose_kernel(x_ref, o_ref):
    o_ref[0] = x_ref[...].T


def _cols_transposed(h, bsz, t, col0, ncols, last):
    per = t // last
    return pl.pallas_call(
        _transpose_kernel,
        grid=(bsz, ncols // LANES),
        in_specs=[pl.BlockSpec((last, LANES), lambda b, j: (b * per + per - 1, col0 // LANES + j))],
        out_specs=pl.BlockSpec((1, LANES, last), lambda b, j: (b, j, 0)),
        out_shape=jax.ShapeDtypeStruct((bsz, ncols, last), F32),
        compiler_params=_params(("parallel", "parallel")),
    )(h)


def _up_kernel(oa_ref, ob_ref, wa_ref, wb_ref, ga_ref, gb_ref, o_ref):
    ua = _bdot(oa_ref[...], wa_ref[...])
    ub = _bdot(ob_ref[...], wb_ref[...])
    o_ref[...] = (_sigmoid(ga_ref[...]) * ua + _sigmoid(gb_ref[...]) * ub).astype(o_ref.dtype)


def _up_gate(o_a, o_b, w_up_a, w_up_b, hg, tm=UP_TM, tn=D_MODEL):
    m = o_a.shape[0]
    tm = min(tm, m)
    nb = D_MODEL // tn
    return pl.pallas_call(
        _up_kernel,
        grid=(m // tm, nb),
        in_specs=[pl.BlockSpec((tm, D_A), lambda i, j: (i, 0)),
                  pl.BlockSpec((tm, D_B), lambda i, j: (i, 0)),
                  pl.BlockSpec((D_A, tn), lambda i, j: (0, j)),
                  pl.BlockSpec((D_B, tn), lambda i, j: (0, j)),
                  pl.BlockSpec((tm, tn), lambda i, j: (i, j)),
                  pl.BlockSpec((tm, tn), lambda i, j: (i, j + nb))],
        out_specs=pl.BlockSpec((tm, tn), lambda i, j: (i, j)),
        out_shape=jax.ShapeDtypeStruct((m, D_MODEL), BF16),
        compiler_params=_params(("parallel", "parallel")),
    )(o_a, o_b, w_up_a, w_up_b, hg, hg)


def _out_ln_kernel(m_ref, w_ref, x_ref, g_ref, b_ref, o_ref):
    u = ALPHA * x_ref[...] + _bdot(m_ref[...], w_ref[...])
    mu = jnp.mean(u, axis=-1, keepdims=True)
    d = u - mu
    var = jnp.mean(d * d, axis=-1, keepdims=True)
    o_ref[...] = d * lax.rsqrt(var + LN_EPS) * g_ref[...] + b_ref[...]


def _out_ln(merged, w_out, x, ln_g, ln_b, tm=LN_TM):
    m = x.shape[0]
    tm = min(tm, m)
    return pl.pallas_call(
        _out_ln_kernel,
        grid=(m // tm,),
        in_specs=[pl.BlockSpec((tm, D_MODEL), lambda i: (i, 0)),
                  pl.BlockSpec((D_MODEL, D_MODEL), lambda i: (0, 0)),
                  pl.BlockSpec((tm, D_MODEL), lambda i: (i, 0)),
                  pl.BlockSpec((1, D_MODEL), lambda i: (0, 0)),
                  pl.BlockSpec((1, D_MODEL), lambda i: (0, 0))],
        out_specs=pl.BlockSpec((tm, D_MODEL), lambda i: (i, 0)),
        out_shape=jax.ShapeDtypeStruct((m, D_MODEL), F32),
        compiler_params=_params(("parallel",)),
    )(merged, w_out, x, ln_g, ln_b)


def _rwkv_premix(xm, vfirst, w0, w2, a0, a2, kkp, ka, vgate):
    r = xm[:, 0:D_A]
    k = xm[:, D_A:2 * D_A]
    v = xm[:, 2 * D_A:3 * D_A]
    w_lo = xm[:, 3 * D_A:3 * D_A + R_W]
    a_lo = xm[:, 3 * D_A + R_W:3 * D_A + R_W + R_A]
    z = xm[:, 3 * D_A + R_W + R_A:A_COLS]
    t = w0 + _bdot(jnp.tanh(w_lo), w2)
    lw = -math.exp(-0.5) * _sigmoid(t)
    if vgate is not None:
        v0, v1, v2 = vgate
        vg = _sigmoid(v0 + _bdot(_bdot(v, v1), v2))
        v = v + (vfirst - v) * vg
    lr = _sigmoid(a0 + _bdot(a_lo, a2))
    kkr = k * kkp
    k2 = k * (1.0 + (lr - 1.0) * ka)
    return r, lw, k2, v, lr, kkr, z


def _head_post(y, r, k2, v, z, rk, gn_g, gn_b):
    mu = jnp.mean(y, axis=-1, keepdims=True)
    d = y - mu
    var = jnp.mean(d * d, axis=-1, keepdims=True)
    yn = d * lax.rsqrt(var + GN_EPS) * gn_g + gn_b
    bonus = jnp.sum(r * k2 * rk, axis=-1, keepdims=True) * v
    return (yn + bonus) * _silu(z)


def _rwkv_prompt_kernel(*refs, has_vgate):
    if has_vgate:
        (fa_ref, vf_ref, mu_ref, w0_ref, w2_ref, a0_ref, a2_ref, kkp_ref, ka_ref, rk_ref, gg_ref, gb_ref,
         v0_ref, v1_ref, v2_ref, o_ref, s_out_ref, s_ref, last_ref) = refs
    else:
        (fa_ref, mu_ref, w0_ref, w2_ref, a0_ref, a2_ref, kkp_ref, ka_ref, rk_ref, gg_ref, gb_ref,
         o_ref, vf_out_ref, s_out_ref, s_ref, last_ref) = refs
    c = pl.program_id(1)
    nc = pl.num_programs(1)
    C = CHUNK

    @pl.when(c == 0)
    def _():
        s_ref[...] = jnp.zeros_like(s_ref)
        last_ref[...] = jnp.zeros_like(last_ref)

    ri = lax.broadcasted_iota(jnp.int32, (C, C), 0)
    ci = lax.broadcasted_iota(jnp.int32, (C, C), 1)
    tri_i = ri >= ci
    tri_s = ri > ci
    tri_b = jnp.where(tri_i, 1.0, 0.0).astype(BF16)
    eye = jnp.where(ri == ci, 1.0, 0.0).astype(F32)
    row = lax.broadcasted_iota(jnp.int32, (C, 1), 0)
    rk = rk_ref[...]
    gg = gg_ref[...]
    gb = gb_ref[...]
    nb = fa_ref.shape[0]

    pre = []
    for bi in range(nb):
        x = fa_ref[bi]
        prev = jnp.where(row == 0, last_ref[bi, 0:1, :], pltpu.roll(x, 1, axis=0))
        last_ref[bi, 0:1, :] = x[C - 1:C, :]
        xm = x + (prev - x) * mu_ref[...]
        if has_vgate:
            vgate = (v0_ref[...], v1_ref[...], v2_ref[...])
            vfirst = vf_ref[bi]
        else:
            vgate, vfirst = None, None
        r, lw, k2, v, lr, kkr, z = _rwkv_premix(xm, vfirst, w0_ref[...], w2_ref[...], a0_ref[...], a2_ref[...],
                                                kkp_ref[...], ka_ref[...], vgate)
        if not has_vgate:
            vf_out_ref[bi] = v
        lw_hi = lw.astype(BF16)
        lw_lo = (lw - lw_hi.astype(F32)).astype(BF16)
        L = jnp.dot(tri_b, lw_hi, preferred_element_type=F32) + jnp.dot(tri_b, lw_lo, preferred_element_type=F32)
        LC = L[C - 1:C, :]
        pre.append(dict(r=r, k2=k2, v=v, lr=lr, kkr=kkr, z=z, e_in=jnp.exp(L), e_ex=jnp.exp(L - lw),
                        e_neg=jnp.exp(-L), e_rem=jnp.exp(LC - L), pc=jnp.exp(LC)))

    def run(units):
        heads = range(len(units))
        sls = [slice(h * HS, (h + 1) * HS) for _, h in units]
        col = lambda name: [pre[bi][name][:, sls[u]] for u, (bi, _) in enumerate(units)]
        bf = lambda xs: [x.astype(BF16) for x in xs]
        kk = []
        for u in col('kkr'):
            kk.append(u * lax.rsqrt(jnp.maximum(jnp.sum(u * u, axis=-1, keepdims=True), 1e-24)))
        r_h, k_h, v_h, lr_h, z_h = col('r'), col('k2'), col('v'), col('lr'), col('z')
        e_in, e_ex, e_neg, e_rem, pc = col('e_in'), col('e_ex'), col('e_neg'), col('e_rem'), col('pc')
        v_b = bf(v_h)
        b_h = [kk[h] * lr_h[h] for h in heads]
        at = [(-kk[h]) * e_ex[h] for h in heads]
        rt = [r_h[h] * e_in[h] for h in heads]
        bt = [b_h[h] * e_neg[h] for h in heads]
        kt = [k_h[h] * e_neg[h] for h in heads]
        bh = bf([b_h[h] * e_rem[h] for h in heads])
        kh = bf([k_h[h] * e_rem[h] for h in heads])
        g = [_bdot(jnp.concatenate([at[h], rt[h]], axis=0), jnp.concatenate([bt[h], kt[h]], axis=0), NT)
             for h in heads]
        a_ab = bf([jnp.where(tri_s, x[:C, :C], 0.0) for x in g])
        a_ak = bf([jnp.where(tri_s, x[:C, C:], 0.0) for x in g])
        a_rb = bf([jnp.where(tri_i, x[C:, :C], 0.0) for x in g])
        a_rk = bf([jnp.where(tri_i, x[C:, C:], 0.0) for x in g])
        tm = [eye + x.astype(F32) for x in a_ab]
        ap = bf([_bdot(x, x) for x in a_ab])
        n = 2
        while n < C:
            tm_next = [tm[h] + _bdot(tm[h], ap[h]) for h in heads]
            if 2 * n < C:
                ap = bf([_bdot(x, x) for x in ap])
            tm = tm_next
            n *= 2
        akv = [_bdot(a_ak[h], v_b[h]) for h in heads]
        wu = bf([_bdot(tm[h], jnp.concatenate([at[h], akv[h]], axis=1)) for h in heads])
        arw = [_bdot(a_rb[h], wu[h]) for h in heads]
        yh = [arw[h][:, HS:] + _bdot(a_rk[h], v_b[h]) for h in heads]
        s_old = [s_ref[bi, h] for bi, h in units]
        s_b = bf(s_old)
        y = [_bdot(rt[h] + arw[h][:, :HS], s_b[h], NT) + yh[h] for h in heads]
        bw = [_bdot(bh[h], wu[h][:, :HS], TN) for h in heads]
        nt = [_bdot(jnp.concatenate([wu[h][:, HS:], v_b[h]], axis=0), jnp.concatenate([bh[h], kh[h]], axis=0), TN)
              for h in heads]
        for u, (bi, h) in enumerate(units):
            s_ref[bi, h] = s_old[u] * pc[u] + _bdot(s_b[u], bw[u], NT) + nt[u]
        for u, (bi, h) in enumerate(units):
            sl = sls[u]
            o_ref[bi, :, sl] = _head_post(y[u], r_h[u], k_h[u], v_h[u], z_h[u], rk[:, sl], gg[:, sl],
                                          gb[:, sl]).astype(o_ref.dtype)

    all_units = [(bi, h) for bi in range(nb) for h in range(H_A)]
    for g0 in range(0, len(all_units), UNIT_GROUP):
        run(all_units[g0:g0 + UNIT_GROUP])

    @pl.when(c == nc - 1)
    def _():
        s_out_ref[...] = s_ref[...]


def _rwkv_prompt(ha, vfirst, p, bsz, t):
    has_vgate = vfirst is not None
    nc = t // CHUNK
    nb = RWKV_ROWS if bsz % RWKV_ROWS == 0 else 1
    row_spec = lambda w: pl.BlockSpec((nb, CHUNK, w), lambda b, c: (b, c, 0))
    full = lambda a: pl.BlockSpec(a.shape, lambda b, c: (0,) * a.ndim)
    ins = [ha.reshape(bsz, t, A_COLS)]
    in_specs = [row_spec(A_COLS)]
    if has_vgate:
        ins.append(vfirst.reshape(bsz, t, D_A))
        in_specs.append(row_spec(D_A))
    names = ['mu', 'w0', 'w2', 'a0', 'a2', 'kk', 'ka', 'rk', 'gn_g', 'gn_b'] + (['v0', 'v1', 'v2'] if has_vgate else [])
    for nme in names:
        ins.append(p[nme])
        in_specs.append(full(p[nme]))
    out_shape = [jax.ShapeDtypeStruct((bsz, t, D_A), BF16)]
    out_specs = [row_spec(D_A)]
    if not has_vgate:
        out_shape.append(jax.ShapeDtypeStruct((bsz, t, D_A), F32))
        out_specs.append(row_spec(D_A))
    out_shape.append(jax.ShapeDtypeStruct((bsz, H_A, HS, HS), F32))
    out_specs.append(pl.BlockSpec((nb, H_A, HS, HS), lambda b, c: (b, 0, 0, 0)))
    outs = pl.pallas_call(
        functools.partial(_rwkv_prompt_kernel, has_vgate=has_vgate),
        grid=(bsz // nb, nc),
        in_specs=in_specs,
        out_specs=out_specs,
        out_shape=out_shape,
        scratch_shapes=[pltpu.VMEM((nb, H_A, HS, HS), F32), pltpu.VMEM((nb, 8, A_COLS), F32)],
        compiler_params=_params(("parallel", "arbitrary")),
    )(*ins)
    o_a = outs[0].reshape(bsz * t, D_A)
    if has_vgate:
        return o_a, vfirst, outs[1]
    return o_a, outs[1].reshape(bsz * t, D_A), outs[2]


def _rwkv_sample_kernel(*refs, has_vgate, bt):
    if has_vgate:
        (fa_ref, prev_ref, s_in_ref, vf_ref, mu_ref, w0_ref, w2_ref, a0_ref, a2_ref, kkp_ref, ka_ref, rk_ref,
         gg_ref, gb_ref, v0_ref, v1_ref, v2_ref, o_ref, s_out_ref, ops_ref, y_ref) = refs
    else:
        (fa_ref, prev_ref, s_in_ref, mu_ref, w0_ref, w2_ref, a0_ref, a2_ref, kkp_ref, ka_ref, rk_ref,
         gg_ref, gb_ref, o_ref, vf_out_ref, s_out_ref, ops_ref, y_ref) = refs
    x = fa_ref[...]
    xm = x + (prev_ref[...] - x) * mu_ref[...]
    if has_vgate:
        vgate = (v0_ref[...], v1_ref[...], v2_ref[...])
        vfirst = vf_ref[...]
    else:
        vgate, vfirst = None, None
    r, lw, k2, v, lr, kkr, z = _rwkv_premix(xm, vfirst, w0_ref[...], w2_ref[...], a0_ref[...], a2_ref[...],
                                            kkp_ref[...], ka_ref[...], vgate)
    if not has_vgate:
        vf_out_ref[...] = v
    w = jnp.exp(lw)
    for h in range(H_A):
        sl = slice(h * HS, (h + 1) * HS)
        kk = kkr[:, sl]
        kk = kk * lax.rsqrt(jnp.maximum(jnp.sum(kk * kk, axis=-1, keepdims=True), 1e-24))
        ops_ref[0, :, sl] = -kk
        ops_ref[1, :, sl] = kk * lr[:, sl]
    ops_ref[2] = w
    ops_ref[3] = k2
    ops_ref[4] = v
    ops_ref[5] = r
    ri = lax.broadcasted_iota(jnp.int32, (HS, HS), 0)
    ci = lax.broadcasted_iota(jnp.int32, (HS, HS), 1)
    eye = jnp.where(ri == ci, 1.0, 0.0).astype(F32)

    heads = range(H_A)
    sls = [slice(h * HS, (h + 1) * HS) for h in heads]
    eye_b = eye.astype(BF16)
    rows_of = lambda x: jnp.broadcast_to(x, (HS, HS)).astype(BF16)
    for b in range(bt):
        row = lambda i: [ops_ref[i, b:b + 1, sl] for sl in sls]
        a_row, b_row, w_row, k_row, v_row, r_row = (row(i) for i in range(6))
        s = [s_in_ref[b, h] for h in heads]
        sa = [_bdot(s[h], rows_of(a_row[h]), NT) for h in heads]
        v_hi = [v_row[h].astype(BF16) for h in heads]
        v_lo = [(v_row[h] - v_hi[h].astype(F32)).astype(BF16) for h in heads]
        v_bc = [_bdot(eye_b, rows_of(v_hi[h]), NT) + _bdot(eye_b, rows_of(v_lo[h]), NT) for h in heads]
        s_new = [s[h] * w_row[h] + sa[h] * b_row[h] + v_bc[h] * k_row[h] for h in heads]
        for h in heads:
            s_out_ref[b, h] = s_new[h]
        y_bc = [_bdot(s_new[h], rows_of(r_row[h]), NT) for h in heads]
        for h in heads:
            y_ref[b:b + 1, sls[h]] = jnp.sum(eye * y_bc[h], axis=0, keepdims=True)
    y = y_ref[...]
    rk = rk_ref[...]
    gg = gg_ref[...]
    gb = gb_ref[...]
    for h in range(H_A):
        sl = slice(h * HS, (h + 1) * HS)
        o_ref[:, sl] = _head_post(y[:, sl], r[:, sl], k2[:, sl], v[:, sl], z[:, sl], rk[:, sl], gg[:, sl],
                                  gb[:, sl]).astype(o_ref.dtype)


def _rwkv_sample(ha, prev, s_in, vfirst, p, bt=8):
    bsz = ha.shape[0]
    has_vgate = vfirst is not None
    row_spec = lambda w: pl.BlockSpec((bt, w), lambda i: (i, 0))
    full = lambda a: pl.BlockSpec(a.shape, lambda i: (0,) * a.ndim)
    st_spec = pl.BlockSpec((bt, H_A, HS, HS), lambda i: (i, 0, 0, 0))
    ins = [ha, prev, s_in]
    in_specs = [row_spec(A_COLS), row_spec(A_COLS), st_spec]
    if has_vgate:
        ins.append(vfirst)
        in_specs.append(row_spec(D_A))
    names = ['mu', 'w0', 'w2', 'a0', 'a2', 'kk', 'ka', 'rk', 'gn_g', 'gn_b'] + (['v0', 'v1', 'v2'] if has_vgate else [])
    for nme in names:
        ins.append(p[nme])
        in_specs.append(full(p[nme]))
    out_shape = [jax.ShapeDtypeStruct((bsz, D_A), BF16)]
    out_specs = [row_spec(D_A)]
    if not has_vgate:
        out_shape.append(jax.ShapeDtypeStruct((bsz, D_A), F32))
        out_specs.append(row_spec(D_A))
    out_shape.append(jax.ShapeDtypeStruct((bsz, H_A, HS, HS), F32))
    out_specs.append(st_spec)
    outs = pl.pallas_call(
        functools.partial(_rwkv_sample_kernel, has_vgate=has_vgate, bt=bt),
        grid=(bsz // bt,),
        in_specs=in_specs,
        out_specs=out_specs,
        out_shape=out_shape,
        scratch_shapes=[pltpu.VMEM((6, bt, D_A), F32), pltpu.VMEM((bt, D_A), F32)],
        compiler_params=_params(("parallel",)),
    )(*ins)
    if has_vgate:
        return outs[0], vfirst, outs[1]
    return outs[0], outs[1], outs[2]


def _compress_rows(load_rows, w1_ref, b1, w2, b2, kv, nch):
    accs = [jnp.zeros((nch, 2 * CMP_HID), F32) for _ in range(G_KV)]
    for tau in range(0, CMP_STRIDE, 2):
        for pair in range(G_KV // 2):
            rows = jnp.concatenate([load_rows(tau, pair), load_rows(tau + 1, pair)], axis=1).astype(BF16)
            for parity in range(2):
                g = 2 * pair + parity
                accs[g] = accs[g] + jnp.dot(rows, w1_ref[kv, tau // 2, parity], preferred_element_type=F32)
    outs = []
    for acc in accs:
        h = acc[:, :CMP_HID] + pltpu.roll(acc[:, CMP_HID:], nch - 1, axis=0) + b1
        outs.append(_bdot(_gelu_tanh(h), w2) + b2)
    return outs


def _kv_prep_kernel(kc_ref, ks_ref, kw_ref, w1_ref, b1_ref, w2_ref, b2_ref,
                    ks_o, vs_o, kw_o, vw_o, kc_o, vc_o, *, nch):
    npair = KV_W // LANES

    def put_transposed(out, pair, x):
        xt = x.T
        out[0, 2 * pair] = xt[0:HD].astype(BF16)
        out[0, 2 * pair + 1] = xt[HD:2 * HD].astype(BF16)

    t = ks_ref.shape[0]
    nsel = t // SEL_LEN
    trow = lax.broadcasted_iota(jnp.int32, (t, HD), 0)
    ccol = lax.broadcasted_iota(jnp.int32, (t, HD), 1)
    extra = jnp.where(((ccol < nsel) & (trow // SEL_LEN == ccol)) | (ccol == nsel) | (ccol == nsel + 1), 1.0, 0.0)
    extra = extra.astype(BF16)
    for g in range(G_KV):
        sl = slice(g * HD, (g + 1) * HD)
        ks_o[0, g] = jnp.concatenate([ks_ref[:, sl].astype(BF16), extra], axis=1)
        kw_o[0, g] = jnp.concatenate([kw_ref[:, sl].astype(BF16), extra], axis=1)
    for pair in range(npair):
        sl2 = slice(KV_W + pair * LANES, KV_W + (pair + 1) * LANES)
        put_transposed(vs_o, pair, ks_ref[:, sl2])
        put_transposed(vw_o, pair, kw_ref[:, sl2])
    for kv in (0, 1):
        load = lambda tau, pair, kv=kv: kc_ref[pl.ds(2 * npair * tau + kv * npair + pair, nch,
                                                     stride=2 * npair * CMP_STRIDE), :]
        res = _compress_rows(load, w1_ref, b1_ref[kv], w2_ref[kv], b2_ref[kv], kv, nch)
        if kv == 0:
            for g in range(G_KV):
                kc_o[0, g] = res[g].astype(BF16)
        else:
            for pair in range(npair):
                put_transposed(vc_o, pair, jnp.concatenate([res[2 * pair], res[2 * pair + 1]], axis=1))


def _kv_prep(hb, hc, cp, bsz, t):
    nch = t // CMP_STRIDE
    blk = lambda j: pl.BlockSpec((t, 2 * KV_W), lambda b: (b, j))
    full = lambda a: pl.BlockSpec(a.shape, lambda b: (0,) * a.ndim)
    c0 = D_B // (2 * KV_W)
    lane_rows = 2 * KV_W // LANES
    hc = hc.reshape(bsz * t * lane_rows, LANES)
    def arr(n, transposed, width=HD):
        shp = (G_KV, HD, n) if transposed else (G_KV, n, width)
        return jax.ShapeDtypeStruct((bsz,) + shp, BF16), pl.BlockSpec((1,) + shp, lambda b: (b, 0, 0, 0))

    outs = [arr(t, False, 2 * HD), arr(t, True), arr(t, False, 2 * HD), arr(t, True), arr(nch, False), arr(nch, True)]
    return pl.pallas_call(
        functools.partial(_kv_prep_kernel, nch=nch),
        grid=(bsz,),
        in_specs=[pl.BlockSpec((t * lane_rows, LANES), lambda b: (b, 0)), blk(c0 + 1), blk(c0 + 2),
                  full(cp['w1']), full(cp['b1']), full(cp['w2']), full(cp['b2'])],
        out_specs=[o[1] for o in outs],
        out_shape=[o[0] for o in outs],
        compiler_params=_params(("parallel",)),
    )(hc, hb, hb, cp['w1'], cp['b1'], cp['w2'], cp['b2'])


def _softmax_pieces(pieces, state):
    return _softmax_update(_piece_scores(pieces), [piece[2] for piece in pieces], state)


def _piece_scores(pieces):
    ss = []
    for k, q, vt, bias, mask in pieces:
        s = jnp.dot(k, q, preferred_element_type=F32)
        if bias is not None:
            s = s + bias
        ss.append(s if mask is None else jnp.where(mask, s, NEG))
    return ss


def _softmax_update(ss, vts, state):
    m_new = jnp.max(ss[0], axis=0, keepdims=True)
    for s in ss[1:]:
        m_new = jnp.maximum(m_new, jnp.max(s, axis=0, keepdims=True))
    if state is not None:
        m_new = jnp.maximum(m_new, state[0])
    ps = [jnp.exp2(s - m_new) for s in ss]
    l_new = jnp.sum(ps[0], axis=0, keepdims=True)
    for p in ps[1:]:
        l_new = l_new + jnp.sum(p, axis=0, keepdims=True)
    p_all = jnp.concatenate([p.astype(BF16) for p in ps], axis=0)
    vt_all = jnp.concatenate(vts, axis=1)
    acc_new = jnp.dot(vt_all, p_all, preferred_element_type=F32)
    if state is not None:
        alpha = jnp.exp2(state[0] - m_new)
        l_new = l_new + alpha * state[1]
        acc_new = acc_new + alpha * state[2]
    return m_new, l_new, acc_new


def _tree_sum(xs):
    while len(xs) > 1:
        xs = [xs[i] + xs[i + 1] for i in range(0, len(xs) - 1, 2)] + ([xs[-1]] if len(xs) % 2 else [])
    return xs[0]


def _nsa_prompt_kernel(q_ref, gl_ref, zb_ref, kc_ref, vc_ref, ks_ref, vs_ref, kw_ref, vw_ref,
                       bc_ref, d0_ref, d1_ref, far_ref, ov_ref, o_ref, *, t, nch):
    g = pl.program_id(1)
    qt = pl.program_id(2)
    nsel = t // SEL_LEN
    nkb = t // KB
    nwin = WIN // KB
    ncols = R_Q * TQ
    qt_t = (q_ref[...] * (SCALE * LOG2E)).T
    qst = jnp.concatenate([qt_t[r * HD:(r + 1) * HD, :] for r in range(R_Q)], axis=1).astype(BF16)
    qpos = qt * TQ + lax.broadcasted_iota(jnp.int32, (1, TQ), 1)
    rep = lambda x: jnp.concatenate([x] * R_Q, axis=1)

    ik = lax.broadcasted_iota(jnp.int32, (KB, TQ), 0)
    iq = lax.broadcasted_iota(jnp.int32, (KB, TQ), 1)
    causal = rep(ik <= iq)
    bias_d0 = jnp.concatenate([d0_ref[r] for r in range(R_Q)], axis=1)
    bias_d1 = jnp.concatenate([d1_ref[r] for r in range(R_Q)], axis=1)
    far_row = jnp.concatenate([jnp.full((1, TQ), far_ref[g * R_Q + r], F32) for r in range(R_Q)], axis=1)
    far_hi = far_row.astype(BF16).astype(F32)
    far_lo = far_row - far_hi
    srow = lax.broadcasted_iota(jnp.int32, (HD, ncols), 0)

    def q_operand(allowed, with_far):
        extra = jnp.where((srow < nsel) & jnp.logical_not(allowed), NEG, 0.0)
        if with_far:
            extra = jnp.where(srow == nsel, far_hi, jnp.where(srow == nsel + 1, far_lo, extra))
        return jnp.concatenate([qst, extra.astype(BF16)], axis=0)

    def piece(kref, vref, kb, q, bias=None, mask=None):
        kbc = jnp.where(kb < 0, nkb - 1, jnp.minimum(kb, nkb - 1))
        off = pl.multiple_of(kbc * KB, KB)
        return kref[0, 0, pl.ds(off, KB), :], q, vref[0, 0, :, pl.ds(off, KB)], bias, mask

    q_near = q_operand(srow < 2 * (qt + 1), False)
    q_far = q_operand(srow < 2 * (qt - 1), True)
    pieces = [piece(kw_ref, vw_ref, qt, q_near, bias_d0, causal), piece(kw_ref, vw_ref, qt - 1, q_near, bias_d1)]
    for j in range(2, nwin):
        pieces.append(piece(kw_ref, vw_ref, qt - j, q_far))
    pieces.append(piece(kw_ref, vw_ref, qt - nwin, q_far, None, rep(ik > iq)))
    _, l_w, acc_w = _softmax_pieces(pieces, None)
    o_w = acc_w / l_w

    nrow = lax.broadcasted_iota(jnp.int32, (nch, TQ), 0)
    mask_c = (qpos >= nrow * CMP_STRIDE + (CMP_LEN - 1)) & (nrow < nch - 1)
    mask_c4 = rep(mask_c)
    bias_c = jnp.concatenate([bc_ref[r] for r in range(R_Q)], axis=1)
    s = jnp.dot(kc_ref[0, 0], qst, preferred_element_type=F32) + bias_c
    s = jnp.where(mask_c4, s, NEG)
    p = jnp.where(mask_c4, jnp.exp2(s - jnp.max(s, axis=0, keepdims=True)), 0.0)
    l = jnp.sum(p, axis=0, keepdims=True)
    p = p / jnp.where(l > 0.0, l, 1.0)
    o_c = jnp.dot(vc_ref[0, 0], p.astype(BF16), preferred_element_type=F32)
    psum = p[:, 0:TQ]
    for r in range(1, R_Q):
        psum = psum + p[:, r * TQ:(r + 1) * TQ]
    p_hi = psum.astype(BF16)
    p_lo = (psum - p_hi.astype(F32)).astype(BF16)
    ov = ov_ref[...]
    imp = jnp.dot(ov, p_hi, preferred_element_type=F32) + jnp.dot(ov, p_lo, preferred_element_type=F32)
    blk = lax.broadcasted_iota(jnp.int32, (nsel, TQ), 0)
    cur = qpos // SEL_LEN
    imp = jnp.where(blk * SEL_LEN <= qpos, imp, NEG)
    imp = jnp.where((blk == 0) | (blk == cur) | (blk == cur - 1), -NEG, imp)
    beaten = []
    for s2 in range(nsel):
        other = imp[s2:s2 + 1, :]
        beaten.append(jnp.where((other > imp) | ((other == imp) & (blk > s2)), 1.0, 0.0))
    chosen = jnp.where(_tree_sum(beaten) < float(min(N_TOP, nsel)), 1.0, 0.0)
    sel = rep(jnp.concatenate([chosen, jnp.zeros((HD - nsel, TQ), F32)], axis=0)) > 0.5

    q_near = q_operand(sel & (srow < 2 * (qt + 1)), False)
    state = _softmax_pieces([piece(ks_ref, vs_ref, qt, q_near, bias_d0, causal),
                             piece(ks_ref, vs_ref, qt - 1, q_near, bias_d1)], None)
    nfar = jnp.maximum(qt - 1, 0)
    q_far = q_operand(sel & (srow < 2 * nfar), True)

    def far_sel(gi, state):
        return _softmax_pieces([piece(ks_ref, vs_ref, gi * FAR_GROUP + j, q_far) for j in range(FAR_GROUP)], state)

    _, l_s, acc_s = lax.fori_loop(0, (nfar + FAR_GROUP - 1) // FAR_GROUP, far_sel, state)
    o_s = acc_s / l_s

    gate = _sigmoid(gl_ref[...]).T
    outs = []
    for r in range(R_Q):
        cs = slice(r * TQ, (r + 1) * TQ)
        outs.append(gate[3 * r:3 * r + 1, :] * o_c[:, cs] + gate[3 * r + 1:3 * r + 2, :] * o_s[:, cs]
                    + gate[3 * r + 2:3 * r + 3, :] * o_w[:, cs])
    o = jnp.concatenate(outs, axis=0).T
    o_ref[...] = (o * _silu(zb_ref[...])).astype(o_ref.dtype)


def _nsa_prompt(hb, kvp, tabs, bsz, t):
    ks_t, vs_t, kw_t, vw_t, kc_t, vc_t = kvp
    nch = t // CMP_STRIDE
    nqt = t // TQ
    gw = R_Q * HD
    k_spec = lambda n, w=HD: pl.BlockSpec((1, 1, n, w), lambda b, g, i: (b, g, 0, 0))
    vt_spec = lambda n: pl.BlockSpec((1, 1, HD, n), lambda b, g, i: (b, g, 0, 0))
    gl0 = (D_B + 6 * KV_W) // GL_PAD
    zb0 = (D_B + 6 * KV_W + G_KV * GL_PAD) // gw
    return pl.pallas_call(
        functools.partial(_nsa_prompt_kernel, t=t, nch=nch),
        grid=(bsz, G_KV, nqt),
        in_specs=[pl.BlockSpec((TQ, gw), lambda b, g, i: (b * nqt + i, g)),
                  pl.BlockSpec((TQ, GL_PAD), lambda b, g, i: (b * nqt + i, gl0 + g)),
                  pl.BlockSpec((TQ, gw), lambda b, g, i: (b * nqt + i, zb0 + g)),
                  k_spec(nch), vt_spec(nch), k_spec(t, 2 * HD), vt_spec(t), k_spec(t, 2 * HD), vt_spec(t),
                  pl.BlockSpec((R_Q, nch, TQ), lambda b, g, i: (g, 0, i)),
                  pl.BlockSpec((R_Q, KB, TQ), lambda b, g, i: (g, 0, 0)),
                  pl.BlockSpec((R_Q, KB, TQ), lambda b, g, i: (g, 0, 0)),
                  pl.BlockSpec(memory_space=pltpu.SMEM),
                  pl.BlockSpec(tabs['overlap'].shape, lambda b, g, i: (0, 0))],
        out_specs=pl.BlockSpec((TQ, gw), lambda b, g, i: (b * nqt + i, g)),
        out_shape=jax.ShapeDtypeStruct((bsz * t, D_B), BF16),
        compiler_params=_params(("parallel", "parallel", "arbitrary")),
    )(hb, hb, hb, kc_t, vc_t, ks_t, vs_t, kw_t, vw_t, tabs['bias_c'], tabs['d0'], tabs['d1'], tabs['far'],
      tabs['overlap'])


def _row_softmax(s, mask, s_new):
    sm = jnp.where(mask, s, NEG)
    m = jnp.maximum(jnp.max(sm, axis=-1, keepdims=True), s_new)
    p = jnp.where(mask, jnp.exp(sm - m), 0.0)
    p_new = jnp.exp(s_new - m)
    l = jnp.sum(p, axis=-1, keepdims=True) + p_new
    return p / l, p_new / l


def _nsa_sample_kernel(pt_ref, *refs, n_pages, past):
    page_refs = refs[:n_pages]
    (win_ref, qbd_ref, new_ref, gl_ref, zb_ref, w1_ref, b1_ref, w2_ref, b2_ref,
     bc_ref, bs_ref, bw_ref, b0_ref, ov_ref, gsum_ref, bdm_ref, ex_ref, o_ref, x_s, kc_s, vc_s) = refs[n_pages:]
    del pt_ref
    nch = past // CMP_STRIDE
    nsel = past // SEL_LEN + 1
    qbd = qbd_ref[0]
    qbd_f = qbd.astype(F32)
    npair = KV_W // LANES

    for pi, pr in enumerate(page_refs):
        for kv in range(2):
            for pair in range(npair):
                x_s[kv, pair, pi * PAGE_SIZE:(pi + 1) * PAGE_SIZE, :] = pr[0, 0, kv, pair * LANES:(pair + 1) * LANES, :].T
    for kv, dst in ((0, kc_s), (1, vc_s)):
        load = lambda tau, pair, kv=kv: x_s[kv, pair, pl.ds(tau, nch, stride=CMP_STRIDE), :]
        res = _compress_rows(load, w1_ref, b1_ref[kv], w2_ref[kv], b2_ref[kv], kv, nch)
        for g in range(G_KV):
            dst[:, g * HD:(g + 1) * HD] = res[g]
    ncol = lax.broadcasted_iota(jnp.int32, (H_B, nch), 1)
    s_c = _bdot(qbd, kc_s[...], NT) + bc_ref[...]
    mask_c = ncol < nch - 1
    sm = jnp.where(mask_c, s_c, NEG)
    p_c = jnp.where(mask_c, jnp.exp(sm - jnp.max(sm, axis=-1, keepdims=True)), 0.0)
    p_c = p_c / jnp.sum(p_c, axis=-1, keepdims=True)
    o_c = _bdot(p_c, vc_s[...])
    p_hi = p_c.astype(BF16)
    p_lo = (p_c - p_hi.astype(F32)).astype(BF16)
    ov = ov_ref[...]
    imp = jnp.dot(p_hi, ov, preferred_element_type=F32) + jnp.dot(p_lo, ov, preferred_element_type=F32)
    i_hi = imp.astype(BF16)
    i_lo = (imp - i_hi.astype(F32)).astype(BF16)
    gs = gsum_ref[...]
    imp = jnp.dot(gs, i_hi, preferred_element_type=F32) + jnp.dot(gs, i_lo, preferred_element_type=F32)
    nsp = imp.shape[1]
    blk = lax.broadcasted_iota(jnp.int32, (H_B, nsp), 1)
    cur = past // SEL_LEN
    imp = jnp.where((blk == 0) | (blk == cur) | (blk == cur - 1), -NEG, imp)
    imp = jnp.where(blk < nsel, imp, 2.0 * NEG)
    rank = jnp.zeros((H_B, nsp), F32)
    for s2 in range(nsel):
        other = imp[:, s2:s2 + 1]
        rank = rank + jnp.where((other > imp) | ((other == imp) & (blk > s2)), 1.0, 0.0)
    sel = jnp.where(rank < float(N_TOP), 1.0, 0.0).astype(BF16)
    mask_s = jnp.dot(sel, ex_ref[...], preferred_element_type=F32) > 0.5

    new = new_ref[0]
    ks_new = new[:, 2 * KV_W:3 * KV_W]
    vs_new = new[:, 3 * KV_W:4 * KV_W]
    kw_new = new[:, 4 * KV_W:5 * KV_W]
    vw_new = new[:, 5 * KV_W:6 * KV_W]
    b0 = b0_ref[...]
    s_s = jnp.concatenate([_bdot(qbd, pr[0, 0, 2]) for pr in page_refs], axis=1) + bs_ref[...]
    s_new = jnp.sum(qbd_f * ks_new, axis=-1, keepdims=True) + b0
    p_s, p_new = _row_softmax(s_s, mask_s, s_new)
    o_s = p_new * vs_new
    for pi, pr in enumerate(page_refs):
        o_s = o_s + _bdot(p_s[:, pi * PAGE_SIZE:(pi + 1) * PAGE_SIZE], pr[0, 0, 3], NT)
    nbuf = win_ref.shape[-1]
    wcol = lax.broadcasted_iota(jnp.int32, (H_B, nbuf), 1)
    s_w = _bdot(qbd, win_ref[0, 0, 0]) + bw_ref[...]
    s_wn = jnp.sum(qbd_f * kw_new, axis=-1, keepdims=True) + b0
    p_w, p_wn = _row_softmax(s_w, wcol >= nbuf + 1 - WIN, s_wn)
    o_w = _bdot(p_w, win_ref[0, 0, 1], NT) + p_wn * vw_new
    gate = _sigmoid(gl_ref[0])
    o = gate[:, 0:1] * o_c + gate[:, 1:2] * o_s + gate[:, 2:3] * o_w
    o = o * bdm_ref[...]
    o16 = o[:, 0:HD]
    for g in range(1, G_KV):
        o16 = o16 + o[:, g * HD:(g + 1) * HD]
    o_ref[0] = (o16 * _silu(zb_ref[0])).astype(o_ref.dtype)


def _nsa_sample(cache_l, l, page_table, win, qbd, new_rows, gl, zb, cp, tabs):
    bsz, n_pages = page_table.shape
    past = n_pages * PAGE_SIZE
    nch = past // CMP_STRIDE
    nbuf = win.shape[-1]
    full = lambda a: pl.BlockSpec(a.shape, lambda b, pt: (0,) * a.ndim)
    page_specs = [pl.BlockSpec((1, 1) + cache_l.shape[2:],
                               functools.partial(lambda b, pt, j: (l, pt[b, j], 0, 0, 0), j=j))
                  for j in range(n_pages)]
    consts = [cp['w1'], cp['b1'], cp['w2'], cp['b2'], tabs['bias_c'], tabs['bias_s'], tabs['bias_w'], tabs['bias_0'],
              tabs['overlap'], tabs['gsum'], tabs['bdmask'], tabs['expand']]
    grid_spec = pltpu.PrefetchScalarGridSpec(
        num_scalar_prefetch=1,
        grid=(bsz,),
        in_specs=page_specs + [
            pl.BlockSpec((1, 1, 2, KV_W, nbuf), lambda b, pt: (l, b, 0, 0, 0)),
            pl.BlockSpec((1, H_B, KV_W), lambda b, pt: (b, 0, 0)),
            pl.BlockSpec((1, 1, 6 * KV_W), lambda b, pt: (b, 0, 0)),
            pl.BlockSpec((1, H_B, 3), lambda b, pt: (b, 0, 0)),
            pl.BlockSpec((1, H_B, HD), lambda b, pt: (b, 0, 0)),
        ] + [full(a) for a in consts],
        out_specs=pl.BlockSpec((1, H_B, HD), lambda b, pt: (b, 0, 0)),
        scratch_shapes=[pltpu.VMEM((2, KV_W // LANES, past, LANES), F32), pltpu.VMEM((nch, KV_W), F32),
                        pltpu.VMEM((nch, KV_W), F32)],
    )
    return pl.pallas_call(
        functools.partial(_nsa_sample_kernel, n_pages=n_pages, past=past),
        grid_spec=grid_spec,
        out_shape=jax.ShapeDtypeStruct((bsz, H_B, HD), BF16),
        compiler_params=_params(("arbitrary",)),
    )(page_table, *([cache_l] * n_pages), win, qbd, new_rows, gl, zb, *consts)


def _win_update_kernel(win_ref, new_ref, o_ref):
    nbuf = win_ref.shape[-1]
    ri = lax.broadcasted_iota(jnp.int32, (KV_W, KV_W), 0)
    ci = lax.broadcasted_iota(jnp.int32, (KV_W, KV_W), 1)
    eye = jnp.where(ri == ci, 1.0, 0.0).astype(BF16)
    lane = lax.broadcasted_iota(jnp.int32, (KV_W, nbuf), 1)
    for c in range(2):
        rem = new_ref[0, 0, :, c * KV_W:(c + 1) * KV_W]
        col = jnp.zeros((KV_W, LANES), F32)
        for _ in range(3):
            part = rem.astype(BF16)
            rem = rem - part.astype(F32)
            col = col + lax.dot_general(eye, jnp.broadcast_to(part, (LANES, KV_W)), NT, preferred_element_type=F32)
        col = jnp.concatenate([col] * (nbuf // LANES), axis=1)
        o_ref[0, 0, c] = jnp.where(lane == nbuf - 1, col, pltpu.roll(win_ref[0, 0, c], nbuf - 1, axis=1))


def _win_update(win_l, new_win):
    depth, bsz = win_l.shape[:2]
    blk = (1, 1) + win_l.shape[2:]
    return pl.pallas_call(
        _win_update_kernel,
        grid=(depth, bsz),
        in_specs=[pl.BlockSpec(blk, lambda l, b: (l, b, 0, 0, 0)),
                  pl.BlockSpec((1, 1, 1, 2 * KV_W), lambda l, b: (l, b, 0, 0))],
        out_specs=pl.BlockSpec(blk, lambda l, b: (l, b, 0, 0, 0)),
        out_shape=jax.ShapeDtypeStruct(win_l.shape, F32),
        compiler_params=_params(("parallel", "parallel")),
    )(win_l, new_win)


def _t5_bucket(dist):
    n = jnp.maximum(dist, 0)
    max_exact = NUM_BUCKETS // 2
    nf = jnp.maximum(n, 1).astype(F32)
    large = max_exact + (jnp.log(nf / max_exact) / math.log(MAX_DIST / max_exact)
                         * (NUM_BUCKETS - max_exact)).astype(jnp.int32)
    large = jnp.minimum(large, NUM_BUCKETS - 1)
    return jnp.where(n < max_exact, n, large)


def _bias_of(rel_bias, dist):
    return jnp.moveaxis(rel_bias[_t5_bucket(dist)], -1, 0).astype(F32)


def _overlap(nch, nsel):
    ci = np.arange(nch)[:, None] * CMP_STRIDE
    sj = np.arange(nsel)[None, :] * SEL_LEN
    ov = ((ci < sj + SEL_LEN) & (ci + CMP_LEN > sj)).astype(np.float32)
    ov[nch - 1:, :] = 0.0
    return ov


def _skew(g, n, step, length):
    h, L = g.shape
    flat = jnp.tile(g, (1, n))[:, :n * (L - step)]
    return flat.reshape(h, n, L - step)[:, :, :length]


def _prompt_tables(rel_bias, t):
    nch = t // CMP_STRIDE
    nsel = t // SEL_LEN
    f = _bias_of(rel_bias, jnp.arange(max(t, 2 * KB), dtype=jnp.int32)) * LOG2E
    f0 = lambda n: jnp.broadcast_to(f[:, :1], (H_B, n))
    lead = CMP_LEN - 1
    g_c = jnp.concatenate([f0(lead), f[:, :t - lead], f0(CMP_STRIDE * nch)], axis=1)
    g_0 = jnp.concatenate([f[:, :KB], f0(KB)], axis=1)
    g_1 = jnp.concatenate([f[:, KB:2 * KB], f[:, :KB]], axis=1)
    return {
        'bias_c': _skew(g_c, nch, CMP_STRIDE, t),
        'd0': _skew(g_0, KB, 1, TQ),
        'd1': _skew(g_1, KB, 1, TQ),
        'far': rel_bias[NUM_BUCKETS - 1].astype(F32) * LOG2E,
        'overlap': jnp.asarray(_overlap(nch, nsel).T, BF16),
    }


def _sample_tables(rel_bias, past, nbuf):
    nch = past // CMP_STRIDE
    nsel = past // SEL_LEN + 1
    nsp = -(-nsel // LANES) * LANES
    cmp_end = jnp.arange(nch, dtype=jnp.int32) * CMP_STRIDE + CMP_LEN - 1
    ov = np.zeros((nch, nsp), np.float32)
    ov[:, :nsel] = _overlap(nch, nsel)
    hh = np.arange(H_B)
    gsum = (hh[:, None] // R_Q == hh[None, :] // R_Q).astype(np.float32)
    bdm = (hh[:, None] // R_Q == np.arange(KV_W)[None, :] // HD).astype(np.float32)
    expand = (np.arange(nsp)[:, None] == np.arange(past)[None, :] // SEL_LEN).astype(np.float32)
    return {
        'bias_c': _bias_of(rel_bias, past - cmp_end),
        'bias_s': _bias_of(rel_bias, past - jnp.arange(past, dtype=jnp.int32)),
        'bias_w': _bias_of(rel_bias, nbuf - jnp.arange(nbuf, dtype=jnp.int32)),
        'bias_0': _bias_of(rel_bias, jnp.zeros((1,), jnp.int32)),
        'overlap': jnp.asarray(ov, BF16),
        'gsum': jnp.asarray(gsum, BF16),
        'bdmask': jnp.asarray(bdm, F32),
        'expand': jnp.asarray(expand, BF16),
    }


def _layer_params(l, w_in, mu_shift, rw_w0, rw_w2, rw_a0, rw_a2, rw_kk, rw_ka, rw_rk, rw_gn_g, rw_gn_b,
                  rw_v0, rw_v1, rw_v2, cmp_w1, cmp_b1, cmp_w2, cmp_b2, w_up_a, w_up_b, w_out, ln_g, ln_b):
    w = w_in[l]
    b0 = A_COLS
    q_kv = w[:, b0:b0 + D_B + 6 * KV_W]
    gl = w[:, b0 + D_B + 6 * KV_W:b0 + D_B + 6 * KV_W + 3 * H_B].reshape(D_MODEL, G_KV, 3 * R_Q)
    gl = jnp.pad(gl, ((0, 0), (0, 0), (0, GL_PAD - 3 * R_Q))).reshape(D_MODEL, G_KV * GL_PAD)
    zb = w[:, b0 + D_B + 6 * KV_W + 3 * H_B:b0 + D_B + 6 * KV_W + 3 * H_B + D_B]
    half = CMP_STRIDE * HD
    w1 = cmp_w1[l]
    w1r = jnp.concatenate([w1[:, :half].reshape(2, CMP_STRIDE, HD, CMP_HID),
                           w1[:, half:].reshape(2, CMP_STRIDE, HD, CMP_HID)], axis=-1)
    zero = jnp.zeros_like(w1r)
    w1r = jnp.stack([jnp.concatenate([w1r, zero], axis=2), jnp.concatenate([zero, w1r], axis=2)], axis=2)
    w1r = w1r.reshape(2, CMP_STRIDE // 2, 2, 2, LANES, 2 * CMP_HID).transpose(0, 1, 3, 2, 4, 5)
    w1r = w1r.reshape(2, CMP_STRIDE // 2, 2, 2 * LANES, 2 * CMP_HID)
    row = lambda a: a.reshape(1, -1).astype(F32)
    p = {
        'wa': w[:, :A_COLS].astype(BF16),
        'wb': jnp.concatenate([q_kv, gl, zb], axis=1).astype(BF16),
        'wg': w[:, b0 + D_B + 6 * KV_W + 3 * H_B + D_B:].astype(BF16),
        'wc': w[:, b0 + D_B:b0 + D_B + 2 * KV_W].astype(BF16),
        'mu': row(mu_shift[l]), 'w0': row(rw_w0[l]), 'w2': rw_w2[l].astype(BF16), 'a0': row(rw_a0[l]),
        'a2': rw_a2[l].astype(BF16), 'kk': row(rw_kk[l]), 'ka': row(rw_ka[l]), 'rk': row(rw_rk[l]),
        'gn_g': row(rw_gn_g[l]), 'gn_b': row(rw_gn_b[l]),
        'cmp': {'w1': w1r.astype(BF16), 'b1': cmp_b1[l].reshape(2, 1, CMP_HID).astype(F32),
                'w2': cmp_w2[l].astype(BF16), 'b2': cmp_b2[l].reshape(2, 1, HD).astype(F32)},
        'w_up_a': w_up_a[l].astype(BF16), 'w_up_b': w_up_b[l].astype(BF16), 'w_out': w_out[l].astype(BF16),
        'ln_g': row(ln_g[l]), 'ln_b': row(ln_b[l]),
    }
    if l > 0:
        p['v0'] = row(rw_v0[l - 1])
        p['v1'] = rw_v1[l - 1].astype(BF16)
        p['v2'] = rw_v2[l - 1].astype(BF16)
    return p


def _project(x2, p):
    m = x2.shape[0]
    ha = _matmul(x2, p['wa'], PROJ_TM, A_COLS // 3)
    hb = _matmul(x2, p['wb'], PROJ_TM, PROJ_TN)
    hg = _matmul(x2, p['wg'], PROJ_TM, PROJ_TN)
    return ha, hb, hg


def _finish(x2, o_a, o_b, hg, p):
    merged = _up_gate(o_a, o_b, p['w_up_a'], p['w_up_b'], hg)
    return _out_ln(merged, p['w_out'], x2, p['ln_g'], p['ln_b'])


def _prompt_layer(x2, vfirst, p, tabs, bsz, t):
    ha, hb, hg = _project(x2, p)
    o_a, vfirst, wkv = _rwkv_prompt(ha, vfirst, p, bsz, t)
    hc = _matmul(x2, p['wc'], PROJ_TM, 2 * KV_W)
    kvp = _kv_prep(hb, hc, p['cmp'], bsz, t)
    o_b = _nsa_prompt(hb, kvp, tabs, bsz, t)
    y = _finish(x2, o_a, o_b, hg, p)
    kvc = D_B
    nwin = min(WIN, t)
    to_rows = lambda x, n: jnp.transpose(x.reshape(bsz, n, G_KV, HD, x.shape[-1]), (0, 4, 1, 2, 3))
    new_rows = to_rows(_cols_transposed(hb, bsz, t, kvc, 4 * KV_W, t), 4)
    win_state = to_rows(_cols_transposed(hb, bsz, t, kvc + 4 * KV_W, 2 * KV_W, nwin), 2)
    shift = ha.reshape(bsz, t, A_COLS)[:, t - 1]
    return y, vfirst, (new_rows, win_state, wkv, shift)


def _sample_layer(x2, vfirst, l, p, tabs, cache_l, win_l, cache_win_kv, state_wkv, state_shift, page_table):
    bsz = x2.shape[0]
    ha, hb, hg = _project(x2, p)
    o_a, vfirst, wkv = _rwkv_sample(ha, state_shift[l], state_wkv[l], vfirst, p)
    kvc = D_B
    q = hb[:, :D_B].reshape(bsz, G_KV, R_Q, HD) * SCALE
    eye = jnp.eye(G_KV, dtype=F32)
    qbd = (q[:, :, :, None, :] * eye[None, :, None, :, None]).reshape(bsz, H_B, KV_W).astype(BF16)
    new6 = hb[:, kvc:kvc + 6 * KV_W]
    gl0 = kvc + 6 * KV_W
    gl = hb[:, gl0:gl0 + G_KV * GL_PAD].reshape(bsz, G_KV, GL_PAD)[:, :, :3 * R_Q].reshape(bsz, H_B, 3)
    zb = hb[:, gl0 + G_KV * GL_PAD:].reshape(bsz, H_B, HD)
    o_b = _nsa_sample(cache_l, l, page_table, win_l, qbd, new6[:, None, :], gl, zb, p['cmp'], tabs)
    y = _finish(x2, o_a, o_b.reshape(bsz, D_B), hg, p)
    new_rows = new6[:, :4 * KV_W].reshape(bsz, 1, 4, G_KV, HD)
    new_win = new6[:, None, 4 * KV_W:]
    return y, vfirst, (new_rows, new_win, wkv, ha)


def kernel(x_prompt, x_sample, cache_kv, cache_win_kv, state_wkv, state_shift, page_table, w_in, mu_shift, rw_w0, rw_w2, rw_a0, rw_a2, rw_kk, rw_ka, rw_rk, rw_gn_g, rw_gn_b, rw_v0, rw_v1, rw_v2, cmp_w1, cmp_b1, cmp_w2, cmp_b2, rel_bias, w_up_a, w_up_b, w_out, ln_g, ln_b):
    bsz, t, _ = x_prompt.shape
    dec_b = x_sample.shape[0]
    n_pages = page_table.shape[1]
    depth, n_phys = cache_kv.shape[:2]
    cache_l = jnp.transpose(cache_kv, (0, 1, 3, 4, 5, 2)).reshape(depth, n_phys, 4, KV_W, PAGE_SIZE)
    win_l = jnp.transpose(cache_win_kv, (0, 1, 3, 4, 5, 2)).reshape(depth, dec_b, 2, KV_W, cache_win_kv.shape[2])
    tabs_p = _prompt_tables(rel_bias, t)
    tabs_s = _sample_tables(rel_bias, n_pages * PAGE_SIZE, cache_win_kv.shape[2])
    y_p = x_prompt.reshape(bsz * t, D_MODEL)
    y_s = x_sample.reshape(dec_b, D_MODEL)
    vf_p, vf_s = None, None
    st_p, st_s = [], []
    for l in range(depth):
        p = _layer_params(l, w_in, mu_shift, rw_w0, rw_w2, rw_a0, rw_a2, rw_kk, rw_ka, rw_rk, rw_gn_g, rw_gn_b,
                          rw_v0, rw_v1, rw_v2, cmp_w1, cmp_b1, cmp_w2, cmp_b2, w_up_a, w_up_b, w_out, ln_g, ln_b)
        y_p, vf_p, sp = _prompt_layer(y_p, vf_p, p, tabs_p, bsz, t)
        y_s, vf_s, ss = _sample_layer(y_s, vf_s, l, p, tabs_s, cache_l, win_l, cache_win_kv, state_wkv, state_shift,
                                      page_table)
        st_p.append(sp)
        st_s.append(ss)
    stack = lambda st, i: jnp.stack([s[i] for s in st])
    nbuf = cache_win_kv.shape[2]
    win_next = _win_update(win_l, stack(st_s, 1)).reshape(depth, dec_b, 2, G_KV, HD, nbuf)
    win_next = jnp.transpose(win_next, (0, 1, 5, 2, 3, 4))
    return (y_p.reshape(bsz, t, D_MODEL), y_s.reshape(dec_b, 1, D_MODEL),
            stack(st_p, 0), stack(st_p, 1), stack(st_p, 2), stack(st_p, 3),
            stack(st_s, 0), win_next, stack(st_s, 2), stack(st_s, 3))
```

```python
import functools
import math

import numpy as np
import jax
import jax.numpy as jnp
from jax import lax
from jax.experimental import pallas as pl
from jax.experimental.pallas import tpu as pltpu

D_MODEL = 2048
DEPTH = 2
PAGE_SIZE = 128
HS = 64
D_A = D_MODEL // 2
H_A = D_A // HS
R_W = 64
R_A = 64
R_V = 32
GN_EPS = 64e-5
HD = 64
D_B = D_MODEL // 2
H_B = D_B // HD
G_KV = 4
R_Q = H_B // G_KV
KV_W = G_KV * HD
CMP_LEN = 32
CMP_STRIDE = 16
CMP_HID = 128
SEL_LEN = 64
N_TOP = 16
WIN = 512
NUM_BUCKETS = 32
MAX_DIST = 128
SCALE = HD ** -0.5
A_COLS = 4 * D_A + R_W + R_A
ALPHA = (2 * DEPTH) ** 0.25
LN_EPS = 1e-5
NEG = -1e30
LOG2E = 1.4426950408889634

F32 = jnp.float32
BF16 = jnp.bfloat16

LANES = 128
VMEM_LIMIT = 56 * 1024 * 1024
CHUNK = 64
RWKV_ROWS = 1
UNIT_GROUP = 16
TQ = 128
KB = 128
FAR_GROUP = 4
NSA_GROUPS = 4
WIN_ROWS = 4
PROJ_TM = 1024
PROJ_TN = 1024
UP_TM = 512
LN_TM = 512
GL_PAD = LANES
HB_COLS = D_B + 6 * KV_W + G_KV * GL_PAD + D_B

NT = (((1,), (1,)), ((), ()))
TN = (((0,), (0,)), ((), ()))


def _params(sem):
    return pltpu.CompilerParams(dimension_semantics=sem, vmem_limit_bytes=VMEM_LIMIT)


def _bdot(a, b, dims=None):
    a = a.astype(BF16)
    b = b.astype(BF16)
    if dims is None:
        return jnp.dot(a, b, preferred_element_type=F32)
    return lax.dot_general(a, b, dims, preferred_element_type=F32)


def _sigmoid(x):
    return 1.0 / (1.0 + jnp.exp(-x))


def _silu(x):
    return x * _sigmoid(x)


def _gelu_tanh(x):
    return 0.5 * x * (1.0 + jnp.tanh(math.sqrt(2.0 / math.pi) * (x + 0.044715 * (x * x * x))))


def _mm_kernel(x_ref, w_ref, o_ref):
    o_ref[...] = _bdot(x_ref[...], w_ref[...])


def _matmul(x, w, tm, tn):
    m, k = x.shape
    n = w.shape[1]
    tm = min(tm, m)
    return pl.pallas_call(
        _mm_kernel,
        grid=(m // tm, n // tn),
        in_specs=[pl.BlockSpec((tm, k), lambda i, j: (i, 0)),
                  pl.BlockSpec((k, tn), lambda i, j: (0, j))],
        out_specs=pl.BlockSpec((tm, tn), lambda i, j: (i, j)),
        out_shape=jax.ShapeDtypeStruct((m, n), F32),
        compiler_params=_params(("parallel", "parallel")),
    )(x, w)


def _transpose_kernel(x_ref, o_ref):
    o_ref[0] = x_ref[...].T


def _cols_transposed(h, bsz, t, col0, ncols, last):
    per = t // last
    return pl.pallas_call(
        _transpose_kernel,
        grid=(bsz, ncols // LANES),
        in_specs=[pl.BlockSpec((last, LANES), lambda b, j: (b * per + per - 1, col0 // LANES + j))],
        out_specs=pl.BlockSpec((1, LANES, last), lambda b, j: (b, j, 0)),
        out_shape=jax.ShapeDtypeStruct((bsz, ncols, last), F32),
        compiler_params=_params(("parallel", "parallel")),
    )(h)


def _up_kernel(oa_ref, ob_ref, wa_ref, wb_ref, ga_ref, gb_ref, o_ref):
    ua = _bdot(oa_ref[...], wa_ref[...])
    ub = _bdot(ob_ref[...], wb_ref[...])
    o_ref[...] = (_sigmoid(ga_ref[...]) * ua + _sigmoid(gb_ref[...]) * ub).astype(o_ref.dtype)


def _up_gate(o_a, o_b, w_up_a, w_up_b, hg, tm=UP_TM, tn=D_MODEL):
    m = o_a.shape[0]
    tm = min(tm, m)
    nb = D_MODEL // tn
    return pl.pallas_call(
        _up_kernel,
        grid=(m // tm, nb),
        in_specs=[pl.BlockSpec((tm, D_A), lambda i, j: (i, 0)),
                  pl.BlockSpec((tm, D_B), lambda i, j: (i, 0)),
                  pl.BlockSpec((D_A, tn), lambda i, j: (0, j)),
                  pl.BlockSpec((D_B, tn), lambda i, j: (0, j)),
                  pl.BlockSpec((tm, tn), lambda i, j: (i, j)),
                  pl.BlockSpec((tm, tn), lambda i, j: (i, j + nb))],
        out_specs=pl.BlockSpec((tm, tn), lambda i, j: (i, j)),
        out_shape=jax.ShapeDtypeStruct((m, D_MODEL), BF16),
        compiler_params=_params(("parallel", "parallel")),
    )(o_a, o_b, w_up_a, w_up_b, hg, hg)


def _out_ln_kernel(m_ref, w_ref, x_ref, g_ref, b_ref, o_ref):
    u = ALPHA * x_ref[...] + _bdot(m_ref[...], w_ref[...])
    mu = jnp.mean(u, axis=-1, keepdims=True)
    d = u - mu
    var = jnp.mean(d * d, axis=-1, keepdims=True)
    o_ref[...] = d * lax.rsqrt(var + LN_EPS) * g_ref[...] + b_ref[...]


def _out_ln(merged, w_out, x, ln_g, ln_b, tm=LN_TM):
    m = x.shape[0]
    tm = min(tm, m)
    return pl.pallas_call(
        _out_ln_kernel,
        grid=(m // tm,),
        in_specs=[pl.BlockSpec((tm, D_MODEL), lambda i: (i, 0)),
                  pl.BlockSpec((D_MODEL, D_MODEL), lambda i: (0, 0)),
                  pl.BlockSpec((tm, D_MODEL), lambda i: (i, 0)),
                  pl.BlockSpec((1, D_MODEL), lambda i: (0, 0)),
                  pl.BlockSpec((1, D_MODEL), lambda i: (0, 0))],
        out_specs=pl.BlockSpec((tm, D_MODEL), lambda i: (i, 0)),
        out_shape=jax.ShapeDtypeStruct((m, D_MODEL), F32),
        compiler_params=_params(("parallel",)),
    )(merged, w_out, x, ln_g, ln_b)


def _rwkv_premix(xm, vfirst, w0, w2, a0, a2, kkp, ka, vgate):
    r = xm[:, 0:D_A]
    k = xm[:, D_A:2 * D_A]
    v = xm[:, 2 * D_A:3 * D_A]
    w_lo = xm[:, 3 * D_A:3 * D_A + R_W]
    a_lo = xm[:, 3 * D_A + R_W:3 * D_A + R_W + R_A]
    z = xm[:, 3 * D_A + R_W + R_A:A_COLS]
    t = w0 + _bdot(jnp.tanh(w_lo), w2)
    lw = -math.exp(-0.5) * _sigmoid(t)
    if vgate is not None:
        v0, v1, v2 = vgate
        vg = _sigmoid(v0 + _bdot(_bdot(v, v1), v2))
        v = v + (vfirst - v) * vg
    lr = _sigmoid(a0 + _bdot(a_lo, a2))
    kkr = k * kkp
    k2 = k * (1.0 + (lr - 1.0) * ka)
    return r, lw, k2, v, lr, kkr, z


def _head_post(y, r, k2, v, z, rk, gn_g, gn_b):
    mu = jnp.mean(y, axis=-1, keepdims=True)
    d = y - mu
    var = jnp.mean(d * d, axis=-1, keepdims=True)
    yn = d * lax.rsqrt(var + GN_EPS) * gn_g + gn_b
    bonus = jnp.sum(r * k2 * rk, axis=-1, keepdims=True) * v
    return (yn + bonus) * _silu(z)


def _rwkv_prompt_kernel(*refs, has_vgate):
    if has_vgate:
        (fa_ref, vf_ref, mu_ref, w0_ref, w2_ref, a0_ref, a2_ref, kkp_ref, ka_ref, rk_ref, gg_ref, gb_ref,
         v0_ref, v1_ref, v2_ref, o_ref, s_out_ref, s_ref, last_ref) = refs
    else:
        (fa_ref, mu_ref, w0_ref, w2_ref, a0_ref, a2_ref, kkp_ref, ka_ref, rk_ref, gg_ref, gb_ref,
         o_ref, vf_out_ref, s_out_ref, s_ref, last_ref) = refs
    c = pl.program_id(1)
    nc = pl.num_programs(1)
    C = CHUNK

    @pl.when(c == 0)
    def _():
        s_ref[...] = jnp.zeros_like(s_ref)
        last_ref[...] = jnp.zeros_like(last_ref)

    ri = lax.broadcasted_iota(jnp.int32, (C, C), 0)
    ci = lax.broadcasted_iota(jnp.int32, (C, C), 1)
    tri_i = ri >= ci
    tri_s = ri > ci
    tri_b = jnp.where(tri_i, 1.0, 0.0).astype(BF16)
    eye = jnp.where(ri == ci, 1.0, 0.0).astype(F32)
    row = lax.broadcasted_iota(jnp.int32, (C, 1), 0)
    rk = rk_ref[...]
    gg = gg_ref[...]
    gb = gb_ref[...]
    nb = fa_ref.shape[0]

    pre = []
    for bi in range(nb):
        x = fa_ref[bi]
        prev = jnp.where(row == 0, last_ref[bi, 0:1, :], pltpu.roll(x, 1, axis=0))
        last_ref[bi, 0:1, :] = x[C - 1:C, :]
        xm = x + (prev - x) * mu_ref[...]
        if has_vgate:
            vgate = (v0_ref[...], v1_ref[...], v2_ref[...])
            vfirst = vf_ref[bi]
        else:
            vgate, vfirst = None, None
        r, lw, k2, v, lr, kkr, z = _rwkv_premix(xm, vfirst, w0_ref[...], w2_ref[...], a0_ref[...], a2_ref[...],
                                                kkp_ref[...], ka_ref[...], vgate)
        if not has_vgate:
            vf_out_ref[bi] = v
        lw_hi = lw.astype(BF16)
        lw_lo = (lw - lw_hi.astype(F32)).astype(BF16)
        L = jnp.dot(tri_b, lw_hi, preferred_element_type=F32) + jnp.dot(tri_b, lw_lo, preferred_element_type=F32)
        LC = L[C - 1:C, :]
        pre.append(dict(r=r, k2=k2, v=v, lr=lr, kkr=kkr, z=z, e_in=jnp.exp(L), e_ex=jnp.exp(L - lw),
                        e_neg=jnp.exp(-L), e_rem=jnp.exp(LC - L), pc=jnp.exp(LC)))

    def run(units):
        heads = range(len(units))
        sls = [slice(h * HS, (h + 1) * HS) for _, h in units]
        col = lambda name: [pre[bi][name][:, sls[u]] for u, (bi, _) in enumerate(units)]
        bf = lambda xs: [x.astype(BF16) for x in xs]
        kk = []
        for u in col('kkr'):
            kk.append(u * lax.rsqrt(jnp.maximum(jnp.sum(u * u, axis=-1, keepdims=True), 1e-24)))
        r_h, k_h, v_h, lr_h, z_h = col('r'), col('k2'), col('v'), col('lr'), col('z')
        e_in, e_ex, e_neg, e_rem, pc = col('e_in'), col('e_ex'), col('e_neg'), col('e_rem'), col('pc')
        v_b = bf(v_h)
        b_h = [kk[h] * lr_h[h] for h in heads]
        at = [(-kk[h]) * e_ex[h] for h in heads]
        rt = [r_h[h] * e_in[h] for h in heads]
        bt = [b_h[h] * e_neg[h] for h in heads]
        kt = [k_h[h] * e_neg[h] for h in heads]
        bh = bf([b_h[h] * e_rem[h] for h in heads])
        kh = bf([k_h[h] * e_rem[h] for h in heads])
        g = [_bdot(jnp.concatenate([at[h], rt[h]], axis=0), jnp.concatenate([bt[h], kt[h]], axis=0), NT)
             for h in heads]
        a_ab = bf([jnp.where(tri_s, x[:C, :C], 0.0) for x in g])
        a_ak = bf([jnp.where(tri_s, x[:C, C:], 0.0) for x in g])
        a_rb = bf([jnp.where(tri_i, x[C:, :C], 0.0) for x in g])
        a_rk = bf([jnp.where(tri_i, x[C:, C:], 0.0) for x in g])
        tm = [eye + x.astype(F32) for x in a_ab]
        ap = bf([_bdot(x, x) for x in a_ab])
        n = 2
        while n < C:
            tm_next = [tm[h] + _bdot(tm[h], ap[h]) for h in heads]
            if 2 * n < C:
                ap = bf([_bdot(x, x) for x in ap])
            tm = tm_next
            n *= 2
        akv = [_bdot(a_ak[h], v_b[h]) for h in heads]
        wu = bf([_bdot(tm[h], jnp.concatenate([at[h], akv[h]], axis=1)) for h in heads])
        arw = [_bdot(a_rb[h], wu[h]) for h in heads]
        yh = [arw[h][:, HS:] + _bdot(a_rk[h], v_b[h]) for h in heads]
        s_old = [s_ref[bi, h] for bi, h in units]
        s_b = bf(s_old)
        y = [_bdot(rt[h] + arw[h][:, :HS], s_b[h], NT) + yh[h] for h in heads]
        bw = [_bdot(bh[h], wu[h][:, :HS], TN) for h in heads]
        nt = [_bdot(jnp.concatenate([wu[h][:, HS:], v_b[h]], axis=0), jnp.concatenate([bh[h], kh[h]], axis=0), TN)
              for h in heads]
        for u, (bi, h) in enumerate(units):
            s_ref[bi, h] = s_old[u] * pc[u] + _bdot(s_b[u], bw[u], NT) + nt[u]
        for u, (bi, h) in enumerate(units):
            sl = sls[u]
            o_ref[bi, :, sl] = _head_post(y[u], r_h[u], k_h[u], v_h[u], z_h[u], rk[:, sl], gg[:, sl],
                                          gb[:, sl]).astype(o_ref.dtype)

    all_units = [(bi, h) for bi in range(nb) for h in range(H_A)]
    for g0 in range(0, len(all_units), UNIT_GROUP):
        run(all_units[g0:g0 + UNIT_GROUP])

    @pl.when(c == nc - 1)
    def _():
        s_out_ref[...] = s_ref[...]


def _rwkv_prompt(ha, vfirst, p, bsz, t):
    has_vgate = vfirst is not None
    nc = t // CHUNK
    nb = RWKV_ROWS if bsz % RWKV_ROWS == 0 else 1
    row_spec = lambda w: pl.BlockSpec((nb, CHUNK, w), lambda b, c: (b, c, 0))
    full = lambda a: pl.BlockSpec(a.shape, lambda b, c: (0,) * a.ndim)
    ins = [ha.reshape(bsz, t, A_COLS)]
    in_specs = [row_spec(A_COLS)]
    if has_vgate:
        ins.append(vfirst.reshape(bsz, t, D_A))
        in_specs.append(row_spec(D_A))
    names = ['mu', 'w0', 'w2', 'a0', 'a2', 'kk', 'ka', 'rk', 'gn_g', 'gn_b'] + (['v0', 'v1', 'v2'] if has_vgate else [])
    for nme in names:
        ins.append(p[nme])
        in_specs.append(full(p[nme]))
    out_shape = [jax.ShapeDtypeStruct((bsz, t, D_A), BF16)]
    out_specs = [row_spec(D_A)]
    if not has_vgate:
        out_shape.append(jax.ShapeDtypeStruct((bsz, t, D_A), F32))
        out_specs.append(row_spec(D_A))
    out_shape.append(jax.ShapeDtypeStruct((bsz, H_A, HS, HS), F32))
    out_specs.append(pl.BlockSpec((nb, H_A, HS, HS), lambda b, c: (b, 0, 0, 0)))
    outs = pl.pallas_call(
        functools.partial(_rwkv_prompt_kernel, has_vgate=has_vgate),
        grid=(bsz // nb, nc),
        in_specs=in_specs,
        out_specs=out_specs,
        out_shape=out_shape,
        scratch_shapes=[pltpu.VMEM((nb, H_A, HS, HS), F32), pltpu.VMEM((nb, 8, A_COLS), F32)],
        compiler_params=_params(("parallel", "arbitrary")),
    )(*ins)
    o_a = outs[0].reshape(bsz * t, D_A)
    if has_vgate:
        return o_a, vfirst, outs[1]
    return o_a, outs[1].reshape(bsz * t, D_A), outs[2]


def _rwkv_sample_kernel(*refs, has_vgate, bt):
    if has_vgate:
        (fa_ref, prev_ref, s_in_ref, vf_ref, mu_ref, w0_ref, w2_ref, a0_ref, a2_ref, kkp_ref, ka_ref, rk_ref,
         gg_ref, gb_ref, v0_ref, v1_ref, v2_ref, o_ref, s_out_ref, ops_ref, y_ref) = refs
    else:
        (fa_ref, prev_ref, s_in_ref, mu_ref, w0_ref, w2_ref, a0_ref, a2_ref, kkp_ref, ka_ref, rk_ref,
         gg_ref, gb_ref, o_ref, vf_out_ref, s_out_ref, ops_ref, y_ref) = refs
    x = fa_ref[...]
    xm = x + (prev_ref[...] - x) * mu_ref[...]
    if has_vgate:
        vgate = (v0_ref[...], v1_ref[...], v2_ref[...])
        vfirst = vf_ref[...]
    else:
        vgate, vfirst = None, None
    r, lw, k2, v, lr, kkr, z = _rwkv_premix(xm, vfirst, w0_ref[...], w2_ref[...], a0_ref[...], a2_ref[...],
                                            kkp_ref[...], ka_ref[...], vgate)
    if not has_vgate:
        vf_out_ref[...] = v
    w = jnp.exp(lw)
    for h in range(H_A):
        sl = slice(h * HS, (h + 1) * HS)
        kk = kkr[:, sl]
        kk = kk * lax.rsqrt(jnp.maximum(jnp.sum(kk * kk, axis=-1, keepdims=True), 1e-24))
        ops_ref[0, :, sl] = -kk
        ops_ref[1, :, sl] = kk * lr[:, sl]
    ops_ref[2] = w
    ops_ref[3] = k2
    ops_ref[4] = v
    ops_ref[5] = r
    ri = lax.broadcasted_iota(jnp.int32, (HS, HS), 0)
    ci = lax.broadcasted_iota(jnp.int32, (HS, HS), 1)
    eye = jnp.where(ri == ci, 1.0, 0.0).astype(F32)

    heads = range(H_A)
    sls = [slice(h * HS, (h + 1) * HS) for h in heads]
    eye_b = eye.astype(BF16)
    rows_of = lambda x: jnp.broadcast_to(x, (HS, HS)).astype(BF16)
    for b in range(bt):
        row = lambda i: [ops_ref[i, b:b + 1, sl] for sl in sls]
        a_row, b_row, w_row, k_row, v_row, r_row = (row(i) for i in range(6))
        s = [s_in_ref[b, h] for h in heads]
        sa = [_bdot(s[h], rows_of(a_row[h]), NT) for h in heads]
        v_hi = [v_row[h].astype(BF16) for h in heads]
        v_lo = [(v_row[h] - v_hi[h].astype(F32)).astype(BF16) for h in heads]
        v_bc = [_bdot(eye_b, rows_of(v_hi[h]), NT) + _bdot(eye_b, rows_of(v_lo[h]), NT) for h in heads]
        s_new = [s[h] * w_row[h] + sa[h] * b_row[h] + v_bc[h] * k_row[h] for h in heads]
        for h in heads:
            s_out_ref[b, h] = s_new[h]
        y_bc = [_bdot(s_new[h], rows_of(r_row[h]), NT) for h in heads]
        for h in heads:
            y_ref[b:b + 1, sls[h]] = jnp.sum(eye * y_bc[h], axis=0, keepdims=True)
    y = y_ref[...]
    rk = rk_ref[...]
    gg = gg_ref[...]
    gb = gb_ref[...]
    for h in range(H_A):
        sl = slice(h * HS, (h + 1) * HS)
        o_ref[:, sl] = _head_post(y[:, sl], r[:, sl], k2[:, sl], v[:, sl], z[:, sl], rk[:, sl], gg[:, sl],
                                  gb[:, sl]).astype(o_ref.dtype)


def _rwkv_sample(ha, prev, s_in, vfirst, p, bt=8):
    bsz = ha.shape[0]
    has_vgate = vfirst is not None
    row_spec = lambda w: pl.BlockSpec((bt, w), lambda i: (i, 0))
    full = lambda a: pl.BlockSpec(a.shape, lambda i: (0,) * a.ndim)
    st_spec = pl.BlockSpec((bt, H_A, HS, HS), lambda i: (i, 0, 0, 0))
    ins = [ha, prev, s_in]
    in_specs = [row_spec(A_COLS), row_spec(A_COLS), st_spec]
    if has_vgate:
        ins.append(vfirst)
        in_specs.append(row_spec(D_A))
    names = ['mu', 'w0', 'w2', 'a0', 'a2', 'kk', 'ka', 'rk', 'gn_g', 'gn_b'] + (['v0', 'v1', 'v2'] if has_vgate else [])
    for nme in names:
        ins.append(p[nme])
        in_specs.append(full(p[nme]))
    out_shape = [jax.ShapeDtypeStruct((bsz, D_A), BF16)]
    out_specs = [row_spec(D_A)]
    if not has_vgate:
        out_shape.append(jax.ShapeDtypeStruct((bsz, D_A), F32))
        out_specs.append(row_spec(D_A))
    out_shape.append(jax.ShapeDtypeStruct((bsz, H_A, HS, HS), F32))
    out_specs.append(st_spec)
    outs = pl.pallas_call(
        functools.partial(_rwkv_sample_kernel, has_vgate=has_vgate, bt=bt),
        grid=(bsz // bt,),
        in_specs=in_specs,
        out_specs=out_specs,
        out_shape=out_shape,
        scratch_shapes=[pltpu.VMEM((6, bt, D_A), F32), pltpu.VMEM((bt, D_A), F32)],
        compiler_params=_params(("parallel",)),
    )(*ins)
    if has_vgate:
        return outs[0], vfirst, outs[1]
    return outs[0], outs[1], outs[2]


def _compress_rows(load_rows, w1_ref, b1, w2, b2, kv, nch):
    accs = [jnp.zeros((nch, 2 * CMP_HID), F32) for _ in range(G_KV)]
    for tau in range(0, CMP_STRIDE, 2):
        for pair in range(G_KV // 2):
            rows = jnp.concatenate([load_rows(tau, pair), load_rows(tau + 1, pair)], axis=1).astype(BF16)
            for parity in range(2):
                g = 2 * pair + parity
                accs[g] = accs[g] + jnp.dot(rows, w1_ref[kv, tau // 2, parity], preferred_element_type=F32)
    outs = []
    for acc in accs:
        h = acc[:, :CMP_HID] + pltpu.roll(acc[:, CMP_HID:], nch - 1, axis=0) + b1
        outs.append(_bdot(_gelu_tanh(h), w2) + b2)
    return outs


def _kv_prep_kernel(kc_ref, ks_ref, kw_ref, w1_ref, b1_ref, w2_ref, b2_ref,
                    ks_o, vs_o, kw_o, vw_o, kc_o, vc_o, *, nch):
    npair = KV_W // LANES

    def put_transposed(out, pair, x):
        xt = x.T
        out[0, 2 * pair] = xt[0:HD].astype(BF16)
        out[0, 2 * pair + 1] = xt[HD:2 * HD].astype(BF16)

    t = ks_ref.shape[0]
    nsel = t // SEL_LEN
    trow = lax.broadcasted_iota(jnp.int32, (t, HD), 0)
    ccol = lax.broadcasted_iota(jnp.int32, (t, HD), 1)
    extra = jnp.where(((ccol < nsel) & (trow // SEL_LEN == ccol)) | (ccol == nsel) | (ccol == nsel + 1), 1.0, 0.0)
    extra = extra.astype(BF16)
    for g in range(G_KV):
        sl = slice(g * HD, (g + 1) * HD)
        ks_o[0, g] = jnp.concatenate([ks_ref[:, sl].astype(BF16), extra], axis=1)
        kw_o[0, g] = jnp.concatenate([kw_ref[:, sl].astype(BF16), extra], axis=1)
    for pair in range(npair):
        sl2 = slice(KV_W + pair * LANES, KV_W + (pair + 1) * LANES)
        put_transposed(vs_o, pair, ks_ref[:, sl2])
        put_transposed(vw_o, pair, kw_ref[:, sl2])
    for kv in (0, 1):
        load = lambda tau, pair, kv=kv: kc_ref[pl.ds(2 * npair * tau + kv * npair + pair, nch,
                                                     stride=2 * npair * CMP_STRIDE), :]
        res = _compress_rows(load, w1_ref, b1_ref[kv], w2_ref[kv], b2_ref[kv], kv, nch)
        if kv == 0:
            for g in range(G_KV):
                kc_o[0, g] = res[g].astype(BF16)
        else:
            for pair in range(npair):
                put_transposed(vc_o, pair, jnp.concatenate([res[2 * pair], res[2 * pair + 1]], axis=1))


def _kv_prep(hb, hc, cp, bsz, t):
    nch = t // CMP_STRIDE
    blk = lambda j: pl.BlockSpec((t, 2 * KV_W), lambda b: (b, j))
    full = lambda a: pl.BlockSpec(a.shape, lambda b: (0,) * a.ndim)
    c0 = D_B // (2 * KV_W)
    lane_rows = 2 * KV_W // LANES
    hc = hc.reshape(bsz * t * lane_rows, LANES)
    def arr(n, transposed, width=HD):
        shp = (G_KV, HD, n) if transposed else (G_KV, n, width)
        return jax.ShapeDtypeStruct((bsz,) + shp, BF16), pl.BlockSpec((1,) + shp, lambda b: (b, 0, 0, 0))

    outs = [arr(t, False, 2 * HD), arr(t, True), arr(t, False, 2 * HD), arr(t, True), arr(nch, False), arr(nch, True)]
    return pl.pallas_call(
        functools.partial(_kv_prep_kernel, nch=nch),
        grid=(bsz,),
        in_specs=[pl.BlockSpec((t * lane_rows, LANES), lambda b: (b, 0)), blk(c0 + 1), blk(c0 + 2),
                  full(cp['w1']), full(cp['b1']), full(cp['w2']), full(cp['b2'])],
        out_specs=[o[1] for o in outs],
        out_shape=[o[0] for o in outs],
        compiler_params=_params(("parallel",)),
    )(hc, hb, hb, cp['w1'], cp['b1'], cp['w2'], cp['b2'])


def _softmax_pieces(pieces, state):
    return _softmax_update(_piece_scores(pieces), [piece[2] for piece in pieces], state)


def _piece_scores(pieces):
    ss = []
    for k, q, vt, bias, mask in pieces:
        s = jnp.dot(k, q, preferred_element_type=F32)
        if bias is not None:
            s = s + bias
        ss.append(s if mask is None else jnp.where(mask, s, NEG))
    return ss


def _softmax_update(ss, vts, state):
    m_new = jnp.max(ss[0], axis=0, keepdims=True)
    for s in ss[1:]:
        m_new = jnp.maximum(m_new, jnp.max(s, axis=0, keepdims=True))
    if state is not None:
        m_new = jnp.maximum(m_new, state[0])
    ps = [jnp.exp2(s - m_new) for s in ss]
    l_new = jnp.sum(ps[0], axis=0, keepdims=True)
    for p in ps[1:]:
        l_new = l_new + jnp.sum(p, axis=0, keepdims=True)
    p_all = jnp.concatenate([p.astype(BF16) for p in ps], axis=0)
    vt_all = jnp.concatenate(vts, axis=1)
    acc_new = jnp.dot(vt_all, p_all, preferred_element_type=F32)
    if state is not None:
        alpha = jnp.exp2(state[0] - m_new)
        l_new = l_new + alpha * state[1]
        acc_new = acc_new + alpha * state[2]
    return m_new, l_new, acc_new


def _tree_sum(xs):
    while len(xs) > 1:
        xs = [xs[i] + xs[i + 1] for i in range(0, len(xs) - 1, 2)] + ([xs[-1]] if len(xs) % 2 else [])
    return xs[0]


def _nsa_prompt_groups_kernel(q_ref, gl_ref, zb_ref, kc_ref, vc_ref, ks_ref, vs_ref, kw_ref, vw_ref,
                              bc_ref, d0_ref, d1_ref, far_ref, ov_ref, o_ref, *, t, nch, ng):
    gp = pl.program_id(1)
    qt = pl.program_id(2)
    nsel = t // SEL_LEN
    nkb = t // KB
    nwin = WIN // KB
    ncols = R_Q * TQ
    gw = R_Q * HD
    groups = range(ng)
    qpos = qt * TQ + lax.broadcasted_iota(jnp.int32, (1, TQ), 1)
    rep = lambda x: jnp.concatenate([x] * R_Q, axis=1)
    ik = lax.broadcasted_iota(jnp.int32, (KB, TQ), 0)
    iq = lax.broadcasted_iota(jnp.int32, (KB, TQ), 1)
    causal = rep(ik <= iq)
    upper = rep(ik > iq)
    srow = lax.broadcasted_iota(jnp.int32, (HD, ncols), 0)

    q_all = (q_ref[...] * (SCALE * LOG2E)).T
    qst, bias_d0, bias_d1, far_hi, far_lo = [], [], [], [], []
    for gi in groups:
        heads = [gi * R_Q + r for r in range(R_Q)]
        qst.append(jnp.concatenate([q_all[h * HD:(h + 1) * HD, :] for h in heads], axis=1).astype(BF16))
        bias_d0.append(jnp.concatenate([d0_ref[h] for h in heads], axis=1))
        bias_d1.append(jnp.concatenate([d1_ref[h] for h in heads], axis=1))
        far_row = jnp.concatenate([jnp.full((1, TQ), far_ref[gp * ng * R_Q + h], F32) for h in heads], axis=1)
        far_hi.append(far_row.astype(BF16).astype(F32))
        far_lo.append(far_row - far_hi[gi])

    def q_operand(gi, allowed, with_far):
        extra = jnp.where((srow < nsel) & jnp.logical_not(allowed), NEG, 0.0)
        if with_far:
            extra = jnp.where(srow == nsel, far_hi[gi], jnp.where(srow == nsel + 1, far_lo[gi], extra))
        return jnp.concatenate([qst[gi], extra.astype(BF16)], axis=0)

    def piece(kref, vref, gi, kb, q, bias=None, mask=None):
        kbc = jnp.where(kb < 0, nkb - 1, jnp.minimum(kb, nkb - 1))
        off = pl.multiple_of(kbc * KB, KB)
        return kref[0, gi, pl.ds(off, KB), :], q, vref[0, gi, :, pl.ds(off, KB)], bias, mask

    o_w = []
    for gi in groups:
        q_near = q_operand(gi, srow < 2 * (qt + 1), False)
        q_far = q_operand(gi, srow < 2 * (qt - 1), True)
        pieces = [piece(kw_ref, vw_ref, gi, qt, q_near, bias_d0[gi], causal),
                  piece(kw_ref, vw_ref, gi, qt - 1, q_near, bias_d1[gi])]
        for j in range(2, nwin):
            pieces.append(piece(kw_ref, vw_ref, gi, qt - j, q_far))
        pieces.append(piece(kw_ref, vw_ref, gi, qt - nwin, q_far, None, upper))
        _, l_w, acc_w = _softmax_pieces(pieces, None)
        o_w.append(acc_w / l_w)

    nrow = lax.broadcasted_iota(jnp.int32, (nch, TQ), 0)
    mask_c4 = rep((qpos >= nrow * CMP_STRIDE + (CMP_LEN - 1)) & (nrow < nch - 1))
    blk = lax.broadcasted_iota(jnp.int32, (nsel, TQ), 0)
    cur = qpos // SEL_LEN
    ov = ov_ref[...]
    o_c, sel = [], []
    for gi in groups:
        bias_c = jnp.concatenate([bc_ref[gi * R_Q + r] for r in range(R_Q)], axis=1)
        s = jnp.dot(kc_ref[0, gi], qst[gi], preferred_element_type=F32) + bias_c
        s = jnp.where(mask_c4, s, NEG)
        p = jnp.where(mask_c4, jnp.exp2(s - jnp.max(s, axis=0, keepdims=True)), 0.0)
        l = jnp.sum(p, axis=0, keepdims=True)
        p = p / jnp.where(l > 0.0, l, 1.0)
        o_c.append(jnp.dot(vc_ref[0, gi], p.astype(BF16), preferred_element_type=F32))
        psum = p[:, 0:TQ]
        for r in range(1, R_Q):
            psum = psum + p[:, r * TQ:(r + 1) * TQ]
        p_hi = psum.astype(BF16)
        p_lo = (psum - p_hi.astype(F32)).astype(BF16)
        imp = jnp.dot(ov, p_hi, preferred_element_type=F32) + jnp.dot(ov, p_lo, preferred_element_type=F32)
        imp = jnp.where(blk * SEL_LEN <= qpos, imp, NEG)
        imp = jnp.where((blk == 0) | (blk == cur) | (blk == cur - 1), -NEG, imp)
        beaten = []
        for s2 in range(nsel):
            other = imp[s2:s2 + 1, :]
            beaten.append(jnp.where((other > imp) | ((other == imp) & (blk > s2)), 1.0, 0.0))
        chosen = jnp.where(_tree_sum(beaten) < float(min(N_TOP, nsel)), 1.0, 0.0)
        sel.append(rep(jnp.concatenate([chosen, jnp.zeros((HD - nsel, TQ), F32)], axis=0)) > 0.5)

    nfar = jnp.maximum(qt - 1, 0)
    states, q_far = [], []
    for gi in groups:
        q_near = q_operand(gi, sel[gi] & (srow < 2 * (qt + 1)), False)
        states.append(_softmax_pieces([piece(ks_ref, vs_ref, gi, qt, q_near, bias_d0[gi], causal),
                                       piece(ks_ref, vs_ref, gi, qt - 1, q_near, bias_d1[gi])], None))
        q_far.append(q_operand(gi, sel[gi] & (srow < 2 * nfar), True))

    def far_sel(it, states):
        return tuple(_softmax_pieces([piece(ks_ref, vs_ref, gi, it * FAR_GROUP + j, q_far[gi])
                                      for j in range(FAR_GROUP)], states[gi]) for gi in groups)

    states = lax.fori_loop(0, (nfar + FAR_GROUP - 1) // FAR_GROUP, far_sel, tuple(states))

    gate = _sigmoid(gl_ref[...]).T
    outs = []
    for gi in groups:
        o_s = states[gi][2] / states[gi][1]
        for r in range(R_Q):
            cs = slice(r * TQ, (r + 1) * TQ)
            row = gi * GL_PAD + 3 * r
            outs.append(gate[row:row + 1, :] * o_c[gi][:, cs] + gate[row + 1:row + 2, :] * o_s[:, cs]
                        + gate[row + 2:row + 3, :] * o_w[gi][:, cs])
    o = jnp.concatenate(outs, axis=0).T
    o_ref[...] = (o * _silu(zb_ref[...])).astype(o_ref.dtype)


def _nsa_prompt(hb, kvp, tabs, bsz, t):
    ks_t, vs_t, kw_t, vw_t, kc_t, vc_t = kvp
    nch = t // CMP_STRIDE
    nqt = t // TQ
    ng = NSA_GROUPS
    gw = ng * R_Q * HD
    k_spec = lambda n, w=HD: pl.BlockSpec((1, ng, n, w), lambda b, g, i: (b, g, 0, 0))
    vt_spec = lambda n: pl.BlockSpec((1, ng, HD, n), lambda b, g, i: (b, g, 0, 0))
    gl0 = (D_B + 6 * KV_W) // (ng * GL_PAD)
    zb0 = (D_B + 6 * KV_W + G_KV * GL_PAD) // gw
    return pl.pallas_call(
        functools.partial(_nsa_prompt_groups_kernel, t=t, nch=nch, ng=ng),
        grid=(bsz, G_KV // ng, nqt),
        in_specs=[pl.BlockSpec((TQ, gw), lambda b, g, i: (b * nqt + i, g)),
                  pl.BlockSpec((TQ, ng * GL_PAD), lambda b, g, i: (b * nqt + i, gl0 + g)),
                  pl.BlockSpec((TQ, gw), lambda b, g, i: (b * nqt + i, zb0 + g)),
                  k_spec(nch), vt_spec(nch), k_spec(t, 2 * HD), vt_spec(t), k_spec(t, 2 * HD), vt_spec(t),
                  pl.BlockSpec((ng * R_Q, nch, TQ), lambda b, g, i: (g, 0, i)),
                  pl.BlockSpec((ng * R_Q, KB, TQ), lambda b, g, i: (g, 0, 0)),
                  pl.BlockSpec((ng * R_Q, KB, TQ), lambda b, g, i: (g, 0, 0)),
                  pl.BlockSpec(memory_space=pltpu.SMEM),
                  pl.BlockSpec(tabs['overlap'].shape, lambda b, g, i: (0, 0))],
        out_specs=pl.BlockSpec((TQ, gw), lambda b, g, i: (b * nqt + i, g)),
        out_shape=jax.ShapeDtypeStruct((bsz * t, D_B), BF16),
        compiler_params=_params(("parallel", "parallel", "arbitrary")),
    )(hb, hb, hb, kc_t, vc_t, ks_t, vs_t, kw_t, vw_t, tabs['bias_c'], tabs['d0'], tabs['d1'], tabs['far'],
      tabs['overlap'])


def _row_softmax(s, mask, s_new):
    sm = jnp.where(mask, s, NEG)
    m = jnp.maximum(jnp.max(sm, axis=-1, keepdims=True), s_new)
    p = jnp.where(mask, jnp.exp(sm - m), 0.0)
    p_new = jnp.exp(s_new - m)
    l = jnp.sum(p, axis=-1, keepdims=True) + p_new
    return p / l, p_new / l


def _nsa_sample_kernel(pt_ref, *refs, n_pages, past):
    page_refs = refs[:n_pages]
    (win_ref, qbd_ref, new_ref, gl_ref, zb_ref, w1_ref, b1_ref, w2_ref, b2_ref,
     bc_ref, bs_ref, bw_ref, b0_ref, ov_ref, gsum_ref, bdm_ref, ex_ref, o_ref, x_s, kc_s, vc_s) = refs[n_pages:]
    del pt_ref
    nch = past // CMP_STRIDE
    nsel = past // SEL_LEN + 1
    qbd = qbd_ref[0]
    qbd_f = qbd.astype(F32)
    npair = KV_W // LANES

    for pi, pr in enumerate(page_refs):
        for kv in range(2):
            for pair in range(npair):
                x_s[kv, pair, pi * PAGE_SIZE:(pi + 1) * PAGE_SIZE, :] = pr[0, 0, kv, pair * LANES:(pair + 1) * LANES, :].T
    for kv, dst in ((0, kc_s), (1, vc_s)):
        load = lambda tau, pair, kv=kv: x_s[kv, pair, pl.ds(tau, nch, stride=CMP_STRIDE), :]
        res = _compress_rows(load, w1_ref, b1_ref[kv], w2_ref[kv], b2_ref[kv], kv, nch)
        for g in range(G_KV):
            dst[:, g * HD:(g + 1) * HD] = res[g]
    ncol = lax.broadcasted_iota(jnp.int32, (H_B, nch), 1)
    s_c = _bdot(qbd, kc_s[...], NT) + bc_ref[...]
    mask_c = ncol < nch - 1
    sm = jnp.where(mask_c, s_c, NEG)
    p_c = jnp.where(mask_c, jnp.exp(sm - jnp.max(sm, axis=-1, keepdims=True)), 0.0)
    p_c = p_c / jnp.sum(p_c, axis=-1, keepdims=True)
    o_c = _bdot(p_c, vc_s[...])
    p_hi = p_c.astype(BF16)
    p_lo = (p_c - p_hi.astype(F32)).astype(BF16)
    ov = ov_ref[...]
    imp = jnp.dot(p_hi, ov, preferred_element_type=F32) + jnp.dot(p_lo, ov, preferred_element_type=F32)
    i_hi = imp.astype(BF16)
    i_lo = (imp - i_hi.astype(F32)).astype(BF16)
    gs = gsum_ref[...]
    imp = jnp.dot(gs, i_hi, preferred_element_type=F32) + jnp.dot(gs, i_lo, preferred_element_type=F32)
    nsp = imp.shape[1]
    blk = lax.broadcasted_iota(jnp.int32, (H_B, nsp), 1)
    cur = past // SEL_LEN
    imp = jnp.where((blk == 0) | (blk == cur) | (blk == cur - 1), -NEG, imp)
    imp = jnp.where(blk < nsel, imp, 2.0 * NEG)
    rank = jnp.zeros((H_B, nsp), F32)
    for s2 in range(nsel):
        other = imp[:, s2:s2 + 1]
        rank = rank + jnp.where((other > imp) | ((other == imp) & (blk > s2)), 1.0, 0.0)
    sel = jnp.where(rank < float(N_TOP), 1.0, 0.0).astype(BF16)
    mask_s = jnp.dot(sel, ex_ref[...], preferred_element_type=F32) > 0.5

    new = new_ref[0]
    ks_new = new[:, 2 * KV_W:3 * KV_W]
    vs_new = new[:, 3 * KV_W:4 * KV_W]
    kw_new = new[:, 4 * KV_W:5 * KV_W]
    vw_new = new[:, 5 * KV_W:6 * KV_W]
    b0 = b0_ref[...]
    s_s = jnp.concatenate([_bdot(qbd, pr[0, 0, 2]) for pr in page_refs], axis=1) + bs_ref[...]
    s_new = jnp.sum(qbd_f * ks_new, axis=-1, keepdims=True) + b0
    p_s, p_new = _row_softmax(s_s, mask_s, s_new)
    o_s = p_new * vs_new
    for pi, pr in enumerate(page_refs):
        o_s = o_s + _bdot(p_s[:, pi * PAGE_SIZE:(pi + 1) * PAGE_SIZE], pr[0, 0, 3], NT)
    nbuf = win_ref.shape[-1]
    wcol = lax.broadcasted_iota(jnp.int32, (H_B, nbuf), 1)
    s_w = _bdot(qbd, win_ref[0, 0, 0]) + bw_ref[...]
    s_wn = jnp.sum(qbd_f * kw_new, axis=-1, keepdims=True) + b0
    p_w, p_wn = _row_softmax(s_w, wcol >= nbuf + 1 - WIN, s_wn)
    o_w = _bdot(p_w, win_ref[0, 0, 1], NT) + p_wn * vw_new
    gate = _sigmoid(gl_ref[0])
    o = gate[:, 0:1] * o_c + gate[:, 1:2] * o_s + gate[:, 2:3] * o_w
    o = o * bdm_ref[...]
    o16 = o[:, 0:HD]
    for g in range(1, G_KV):
        o16 = o16 + o[:, g * HD:(g + 1) * HD]
    o_ref[0] = (o16 * _silu(zb_ref[0])).astype(o_ref.dtype)


def _nsa_sample(cache_l, l, page_table, win, qbd, new_rows, gl, zb, cp, tabs):
    bsz, n_pages = page_table.shape
    past = n_pages * PAGE_SIZE
    nch = past // CMP_STRIDE
    nbuf = win.shape[-1]
    full = lambda a: pl.BlockSpec(a.shape, lambda b, pt: (0,) * a.ndim)
    page_specs = [pl.BlockSpec((1, 1) + cache_l.shape[2:],
                               functools.partial(lambda b, pt, j: (l, pt[b, j], 0, 0, 0), j=j))
                  for j in range(n_pages)]
    consts = [cp['w1'], cp['b1'], cp['w2'], cp['b2'], tabs['bias_c'], tabs['bias_s'], tabs['bias_w'], tabs['bias_0'],
              tabs['overlap'], tabs['gsum'], tabs['bdmask'], tabs['expand']]
    grid_spec = pltpu.PrefetchScalarGridSpec(
        num_scalar_prefetch=1,
        grid=(bsz,),
        in_specs=page_specs + [
            pl.BlockSpec((1, 1, 2, KV_W, nbuf), lambda b, pt: (l, b, 0, 0, 0)),
            pl.BlockSpec((1, H_B, KV_W), lambda b, pt: (b, 0, 0)),
            pl.BlockSpec((1, 1, 6 * KV_W), lambda b, pt: (b, 0, 0)),
            pl.BlockSpec((1, H_B, 3), lambda b, pt: (b, 0, 0)),
            pl.BlockSpec((1, H_B, HD), lambda b, pt: (b, 0, 0)),
        ] + [full(a) for a in consts],
        out_specs=pl.BlockSpec((1, H_B, HD), lambda b, pt: (b, 0, 0)),
        scratch_shapes=[pltpu.VMEM((2, KV_W // LANES, past, LANES), F32), pltpu.VMEM((nch, KV_W), F32),
                        pltpu.VMEM((nch, KV_W), F32)],
    )
    return pl.pallas_call(
        functools.partial(_nsa_sample_kernel, n_pages=n_pages, past=past),
        grid_spec=grid_spec,
        out_shape=jax.ShapeDtypeStruct((bsz, H_B, HD), BF16),
        compiler_params=_params(("arbitrary",)),
    )(page_table, *([cache_l] * n_pages), win, qbd, new_rows, gl, zb, *consts)


def _win_update_kernel(win_ref, new_ref, o_ref):
    nbuf = win_ref.shape[-1]
    ri = lax.broadcasted_iota(jnp.int32, (KV_W, KV_W), 0)
    ci = lax.broadcasted_iota(jnp.int32, (KV_W, KV_W), 1)
    eye = jnp.where(ri == ci, 1.0, 0.0).astype(BF16)
    lane = lax.broadcasted_iota(jnp.int32, (KV_W, nbuf), 1)
    for b in range(win_ref.shape[1]):
        for c in range(2):
            rem = new_ref[0, b, :, c * KV_W:(c + 1) * KV_W]
            col = jnp.zeros((KV_W, LANES), F32)
            for _ in range(3):
                part = rem.astype(BF16)
                rem = rem - part.astype(F32)
                col = col + lax.dot_general(eye, jnp.broadcast_to(part, (LANES, KV_W)), NT,
                                            preferred_element_type=F32)
            col = jnp.concatenate([col] * (nbuf // LANES), axis=1)
            o_ref[0, b, c] = jnp.where(lane == nbuf - 1, col, pltpu.roll(win_ref[0, b, c], nbuf - 1, axis=1))


def _win_update(win_l, new_win):
    depth, bsz = win_l.shape[:2]
    nb = WIN_ROWS if bsz % WIN_ROWS == 0 else 1
    blk = (1, nb) + win_l.shape[2:]
    return pl.pallas_call(
        _win_update_kernel,
        grid=(depth, bsz // nb),
        in_specs=[pl.BlockSpec(blk, lambda l, b: (l, b, 0, 0, 0)),
                  pl.BlockSpec((1, nb, 1, 2 * KV_W), lambda l, b: (l, b, 0, 0))],
        out_specs=pl.BlockSpec(blk, lambda l, b: (l, b, 0, 0, 0)),
        out_shape=jax.ShapeDtypeStruct(win_l.shape, F32),
        compiler_params=_params(("parallel", "parallel")),
    )(win_l, new_win)


def _t5_bucket(dist):
    n = jnp.maximum(dist, 0)
    max_exact = NUM_BUCKETS // 2
    nf = jnp.maximum(n, 1).astype(F32)
    large = max_exact + (jnp.log(nf / max_exact) / math.log(MAX_DIST / max_exact)
                         * (NUM_BUCKETS - max_exact)).astype(jnp.int32)
    large = jnp.minimum(large, NUM_BUCKETS - 1)
    return jnp.where(n < max_exact, n, large)


def _bias_of(rel_bias, dist):
    return jnp.moveaxis(rel_bias[_t5_bucket(dist)], -1, 0).astype(F32)


def _overlap(nch, nsel):
    ci = np.arange(nch)[:, None] * CMP_STRIDE
    sj = np.arange(nsel)[None, :] * SEL_LEN
    ov = ((ci < sj + SEL_LEN) & (ci + CMP_LEN > sj)).astype(np.float32)
    ov[nch - 1:, :] = 0.0
    return ov


def _skew(g, n, step, length):
    h, L = g.shape
    flat = jnp.tile(g, (1, n))[:, :n * (L - step)]
    return flat.reshape(h, n, L - step)[:, :, :length]


def _prompt_tables(rel_bias, t):
    nch = t // CMP_STRIDE
    nsel = t // SEL_LEN
    f = _bias_of(rel_bias, jnp.arange(max(t, 2 * KB), dtype=jnp.int32)) * LOG2E
    f0 = lambda n: jnp.broadcast_to(f[:, :1], (H_B, n))
    lead = CMP_LEN - 1
    g_c = jnp.concatenate([f0(lead), f[:, :t - lead], f0(CMP_STRIDE * nch)], axis=1)
    g_0 = jnp.concatenate([f[:, :KB], f0(KB)], axis=1)
    g_1 = jnp.concatenate([f[:, KB:2 * KB], f[:, :KB]], axis=1)
    return {
        'bias_c': _skew(g_c, nch, CMP_STRIDE, t),
        'd0': _skew(g_0, KB, 1, TQ),
        'd1': _skew(g_1, KB, 1, TQ),
        'far': rel_bias[NUM_BUCKETS - 1].astype(F32) * LOG2E,
        'overlap': jnp.asarray(_overlap(nch, nsel).T, BF16),
    }


def _sample_tables(rel_bias, past, nbuf):
    nch = past // CMP_STRIDE
    nsel = past // SEL_LEN + 1
    nsp = -(-nsel // LANES) * LANES
    cmp_end = jnp.arange(nch, dtype=jnp.int32) * CMP_STRIDE + CMP_LEN - 1
    ov = np.zeros((nch, nsp), np.float32)
    ov[:, :nsel] = _overlap(nch, nsel)
    hh = np.arange(H_B)
    gsum = (hh[:, None] // R_Q == hh[None, :] // R_Q).astype(np.float32)
    bdm = (hh[:, None] // R_Q == np.arange(KV_W)[None, :] // HD).astype(np.float32)
    expand = (np.arange(nsp)[:, None] == np.arange(past)[None, :] // SEL_LEN).astype(np.float32)
    return {
        'bias_c': _bias_of(rel_bias, past - cmp_end),
        'bias_s': _bias_of(rel_bias, past - jnp.arange(past, dtype=jnp.int32)),
        'bias_w': _bias_of(rel_bias, nbuf - jnp.arange(nbuf, dtype=jnp.int32)),
        'bias_0': _bias_of(rel_bias, jnp.zeros((1,), jnp.int32)),
        'overlap': jnp.asarray(ov, BF16),
        'gsum': jnp.asarray(gsum, BF16),
        'bdmask': jnp.asarray(bdm, F32),
        'expand': jnp.asarray(expand, BF16),
    }


def _layer_params(l, w_in, mu_shift, rw_w0, rw_w2, rw_a0, rw_a2, rw_kk, rw_ka, rw_rk, rw_gn_g, rw_gn_b,
                  rw_v0, rw_v1, rw_v2, cmp_w1, cmp_b1, cmp_w2, cmp_b2, w_up_a, w_up_b, w_out, ln_g, ln_b):
    w = w_in[l]
    b0 = A_COLS
    q_kv = w[:, b0:b0 + D_B + 6 * KV_W]
    gl = w[:, b0 + D_B + 6 * KV_W:b0 + D_B + 6 * KV_W + 3 * H_B].reshape(D_MODEL, G_KV, 3 * R_Q)
    gl = jnp.pad(gl, ((0, 0), (0, 0), (0, GL_PAD - 3 * R_Q))).reshape(D_MODEL, G_KV * GL_PAD)
    zb = w[:, b0 + D_B + 6 * KV_W + 3 * H_B:b0 + D_B + 6 * KV_W + 3 * H_B + D_B]
    half = CMP_STRIDE * HD
    w1 = cmp_w1[l]
    w1r = jnp.concatenate([w1[:, :half].reshape(2, CMP_STRIDE, HD, CMP_HID),
                           w1[:, half:].reshape(2, CMP_STRIDE, HD, CMP_HID)], axis=-1)
    zero = jnp.zeros_like(w1r)
    w1r = jnp.stack([jnp.concatenate([w1r, zero], axis=2), jnp.concatenate([zero, w1r], axis=2)], axis=2)
    w1r = w1r.reshape(2, CMP_STRIDE // 2, 2, 2, LANES, 2 * CMP_HID).transpose(0, 1, 3, 2, 4, 5)
    w1r = w1r.reshape(2, CMP_STRIDE // 2, 2, 2 * LANES, 2 * CMP_HID)
    row = lambda a: a.reshape(1, -1).astype(F32)
    p = {
        'wa': w[:, :A_COLS].astype(BF16),
        'wb': jnp.concatenate([q_kv, gl, zb], axis=1).astype(BF16),
        'wg': w[:, b0 + D_B + 6 * KV_W + 3 * H_B + D_B:].astype(BF16),
        'wc': w[:, b0 + D_B:b0 + D_B + 2 * KV_W].astype(BF16),
        'mu': row(mu_shift[l]), 'w0': row(rw_w0[l]), 'w2': rw_w2[l].astype(BF16), 'a0': row(rw_a0[l]),
        'a2': rw_a2[l].astype(BF16), 'kk': row(rw_kk[l]), 'ka': row(rw_ka[l]), 'rk': row(rw_rk[l]),
        'gn_g': row(rw_gn_g[l]), 'gn_b': row(rw_gn_b[l]),
        'cmp': {'w1': w1r.astype(BF16), 'b1': cmp_b1[l].reshape(2, 1, CMP_HID).astype(F32),
                'w2': cmp_w2[l].astype(BF16), 'b2': cmp_b2[l].reshape(2, 1, HD).astype(F32)},
        'w_up_a': w_up_a[l].astype(BF16), 'w_up_b': w_up_b[l].astype(BF16), 'w_out': w_out[l].astype(BF16),
        'ln_g': row(ln_g[l]), 'ln_b': row(ln_b[l]),
    }
    if l > 0:
        p['v0'] = row(rw_v0[l - 1])
        p['v1'] = rw_v1[l - 1].astype(BF16)
        p['v2'] = rw_v2[l - 1].astype(BF16)
    return p


def _project(x2, p):
    m = x2.shape[0]
    ha = _matmul(x2, p['wa'], PROJ_TM, A_COLS // 3)
    hb = _matmul(x2, p['wb'], PROJ_TM, PROJ_TN)
    hg = _matmul(x2, p['wg'], PROJ_TM, PROJ_TN)
    return ha, hb, hg


def _finish(x2, o_a, o_b, hg, p):
    merged = _up_gate(o_a, o_b, p['w_up_a'], p['w_up_b'], hg)
    return _out_ln(merged, p['w_out'], x2, p['ln_g'], p['ln_b'])


def _prompt_layer(x2, vfirst, p, tabs, bsz, t):
    ha, hb, hg = _project(x2, p)
    o_a, vfirst, wkv = _rwkv_prompt(ha, vfirst, p, bsz, t)
    hc = _matmul(x2, p['wc'], PROJ_TM, 2 * KV_W)
    kvp = _kv_prep(hb, hc, p['cmp'], bsz, t)
    o_b = _nsa_prompt(hb, kvp, tabs, bsz, t)
    y = _finish(x2, o_a, o_b, hg, p)
    kvc = D_B
    nwin = min(WIN, t)
    to_rows = lambda x, n: jnp.transpose(x.reshape(bsz, n, G_KV, HD, x.shape[-1]), (0, 4, 1, 2, 3))
    new_rows = to_rows(_cols_transposed(hb, bsz, t, kvc, 4 * KV_W, t), 4)
    win_state = to_rows(_cols_transposed(hb, bsz, t, kvc + 4 * KV_W, 2 * KV_W, nwin), 2)
    shift = ha.reshape(bsz, t, A_COLS)[:, t - 1]
    return y, vfirst, (new_rows, win_state, wkv, shift)


def _sample_layer(x2, vfirst, l, p, tabs, cache_l, win_l, cache_win_kv, state_wkv, state_shift, page_table):
    bsz = x2.shape[0]
    ha, hb, hg = _project(x2, p)
    o_a, vfirst, wkv = _rwkv_sample(ha, state_shift[l], state_wkv[l], vfirst, p)
    kvc = D_B
    q = hb[:, :D_B].reshape(bsz, G_KV, R_Q, HD) * SCALE
    eye = jnp.eye(G_KV, dtype=F32)
    qbd = (q[:, :, :, None, :] * eye[None, :, None, :, None]).reshape(bsz, H_B, KV_W).astype(BF16)
    new6 = hb[:, kvc:kvc + 6 * KV_W]
    gl0 = kvc + 6 * KV_W
    gl = hb[:, gl0:gl0 + G_KV * GL_PAD].reshape(bsz, G_KV, GL_PAD)[:, :, :3 * R_Q].reshape(bsz, H_B, 3)
    zb = hb[:, gl0 + G_KV * GL_PAD:].reshape(bsz, H_B, HD)
    o_b = _nsa_sample(cache_l, l, page_table, win_l, qbd, new6[:, None, :], gl, zb, p['cmp'], tabs)
    y = _finish(x2, o_a, o_b.reshape(bsz, D_B), hg, p)
    new_rows = new6[:, :4 * KV_W].reshape(bsz, 1, 4, G_KV, HD)
    new_win = new6[:, None, 4 * KV_W:]
    return y, vfirst, (new_rows, new_win, wkv, ha)


def kernel(x_prompt, x_sample, cache_kv, cache_win_kv, state_wkv, state_shift, page_table, w_in, mu_shift, rw_w0, rw_w2, rw_a0, rw_a2, rw_kk, rw_ka, rw_rk, rw_gn_g, rw_gn_b, rw_v0, rw_v1, rw_v2, cmp_w1, cmp_b1, cmp_w2, cmp_b2, rel_bias, w_up_a, w_up_b, w_out, ln_g, ln_b):
    bsz, t, _ = x_prompt.shape
    dec_b = x_sample.shape[0]
    n_pages = page_table.shape[1]
    depth, n_phys = cache_kv.shape[:2]
    cache_l = jnp.transpose(cache_kv, (0, 1, 3, 4, 5, 2)).reshape(depth, n_phys, 4, KV_W, PAGE_SIZE)
    win_l = jnp.transpose(cache_win_kv, (0, 1, 3, 4, 5, 2)).reshape(depth, dec_b, 2, KV_W, cache_win_kv.shape[2])
    tabs_p = _prompt_tables(rel_bias, t)
    tabs_s = _sample_tables(rel_bias, n_pages * PAGE_SIZE, cache_win_kv.shape[2])
    y_p = x_prompt.reshape(bsz * t, D_MODEL)
    y_s = x_sample.reshape(dec_b, D_MODEL)
    vf_p, vf_s = None, None
    st_p, st_s = [], []
    for l in range(depth):
        p = _layer_params(l, w_in, mu_shift, rw_w0, rw_w2, rw_a0, rw_a2, rw_kk, rw_ka, rw_rk, rw_gn_g, rw_gn_b,
                          rw_v0, rw_v1, rw_v2, cmp_w1, cmp_b1, cmp_w2, cmp_b2, w_up_a, w_up_b, w_out, ln_g, ln_b)
        y_p, vf_p, sp = _prompt_layer(y_p, vf_p, p, tabs_p, bsz, t)
        y_s, vf_s, ss = _sample_layer(y_s, vf_s, l, p, tabs_s, cache_l, win_l, cache_win_kv, state_wkv, state_shift,
                                      page_table)
        st_p.append(sp)
        st_s.append(ss)
    stack = lambda st, i: jnp.stack([s[i] for s in st])
    nbuf = cache_win_kv.shape[2]
    win_next = _win_update(win_l, stack(st_s, 1)).reshape(depth, dec_b, 2, G_KV, HD, nbuf)
    win_next = jnp.transpose(win_next, (0, 1, 5, 2, 3, 4))
    return (y_p.reshape(bsz, t, D_MODEL), y_s.reshape(dec_b, 1, D_MODEL),
            stack(st_p, 0), stack(st_p, 1), stack(st_p, 2), stack(st_p, 3),
            stack(st_s, 0), win_next, stack(st_s, 2), stack(st_s, 3))
```

```python
import functools
import math

import numpy as np
import jax
import jax.numpy as jnp
from jax import lax
from jax.experimental import pallas as pl
from jax.experimental.pallas import tpu as pltpu

D_MODEL = 2048
DEPTH = 2
PAGE_SIZE = 128
HS = 64
D_A = D_MODEL // 2
H_A = D_A // HS
R_W = 64
R_A = 64
R_V = 32
GN_EPS = 64e-5
HD = 64
D_B = D_MODEL // 2
H_B = D_B // HD
G_KV = 4
R_Q = H_B // G_KV
KV_W = G_KV * HD
CMP_LEN = 32
CMP_STRIDE = 16
CMP_HID = 128
SEL_LEN = 64
N_TOP = 16
WIN = 512
NUM_BUCKETS = 32
MAX_DIST = 128
SCALE = HD ** -0.5
A_COLS = 4 * D_A + R_W + R_A
ALPHA = (2 * DEPTH) ** 0.25
LN_EPS = 1e-5
NEG = -1e30
LOG2E = 1.4426950408889634

F32 = jnp.float32
BF16 = jnp.bfloat16

LANES = 128
VMEM_LIMIT = 56 * 1024 * 1024
CHUNK = 64
RWKV_ROWS = 1
UNIT_GROUP = 16
TQ = 128
KB = 128
FAR_GROUP = 4
NSA_GROUPS = 4
WIN_ROWS = 4
PROJ_TM = 1024
PROJ_TN = 1024
UP_TM = 512
LN_TM = 512
GL_PAD = LANES
HB_COLS = D_B + 6 * KV_W + G_KV * GL_PAD + D_B

NT = (((1,), (1,)), ((), ()))
TN = (((0,), (0,)), ((), ()))


def _params(sem):
    return pltpu.CompilerParams(dimension_semantics=sem, vmem_limit_bytes=VMEM_LIMIT)


def _bdot(a, b, dims=None):
    a = a.astype(BF16)
    b = b.astype(BF16)
    if dims is None:
        return jnp.dot(a, b, preferred_element_type=F32)
    return lax.dot_general(a, b, dims, preferred_element_type=F32)


def _sigmoid(x):
    return 1.0 / (1.0 + jnp.exp(-x))


def _silu(x):
    return x * _sigmoid(x)


def _gelu_tanh(x):
    return 0.5 * x * (1.0 + jnp.tanh(math.sqrt(2.0 / math.pi) * (x + 0.044715 * (x * x * x))))


def _mm_kernel(x_ref, w_ref, o_ref):
    o_ref[...] = _bdot(x_ref[...], w_ref[...]).astype(o_ref.dtype)


def _matmul(x, w, tm, tn, out_dtype=F32):
    m, k = x.shape
    n = w.shape[1]
    tm = min(tm, m)
    return pl.pallas_call(
        _mm_kernel,
        grid=(m // tm, n // tn),
        in_specs=[pl.BlockSpec((tm, k), lambda i, j: (i, 0)),
                  pl.BlockSpec((k, tn), lambda i, j: (0, j))],
        out_specs=pl.BlockSpec((tm, tn), lambda i, j: (i, j)),
        out_shape=jax.ShapeDtypeStruct((m, n), out_dtype),
        compiler_params=_params(("parallel", "parallel")),
    )(x, w)


def _transpose_kernel(x_ref, o_ref):
    o_ref[0] = x_ref[...].T


def _cols_transposed(h, bsz, t, col0, ncols, last):
    per = t // last
    return pl.pallas_call(
        _transpose_kernel,
        grid=(bsz, ncols // LANES),
        in_specs=[pl.BlockSpec((last, LANES), lambda b, j: (b * per + per - 1, col0 // LANES + j))],
        out_specs=pl.BlockSpec((1, LANES, last), lambda b, j: (b, j, 0)),
        out_shape=jax.ShapeDtypeStruct((bsz, ncols, last), F32),
        compiler_params=_params(("parallel", "parallel")),
    )(h)


def _up_kernel(oa_ref, ob_ref, wa_ref, wb_ref, ga_ref, gb_ref, o_ref):
    ua = _bdot(oa_ref[...], wa_ref[...])
    ub = _bdot(ob_ref[...], wb_ref[...])
    ga = ga_ref[...].astype(F32)
    gb = gb_ref[...].astype(F32)
    o_ref[...] = (_sigmoid(ga) * ua + _sigmoid(gb) * ub).astype(o_ref.dtype)


def _up_gate(o_a, o_b, w_up_a, w_up_b, hg, tm=UP_TM, tn=D_MODEL):
    m = o_a.shape[0]
    tm = min(tm, m)
    nb = D_MODEL // tn
    return pl.pallas_call(
        _up_kernel,
        grid=(m // tm, nb),
        in_specs=[pl.BlockSpec((tm, D_A), lambda i, j: (i, 0)),
                  pl.BlockSpec((tm, D_B), lambda i, j: (i, 0)),
                  pl.BlockSpec((D_A, tn), lambda i, j: (0, j)),
                  pl.BlockSpec((D_B, tn), lambda i, j: (0, j)),
                  pl.BlockSpec((tm, tn), lambda i, j: (i, j)),
                  pl.BlockSpec((tm, tn), lambda i, j: (i, j + nb))],
        out_specs=pl.BlockSpec((tm, tn), lambda i, j: (i, j)),
        out_shape=jax.ShapeDtypeStruct((m, D_MODEL), BF16),
        compiler_params=_params(("parallel", "parallel")),
    )(o_a, o_b, w_up_a, w_up_b, hg, hg)


def _out_ln_kernel(m_ref, w_ref, x_ref, g_ref, b_ref, o_ref):
    u = ALPHA * x_ref[...] + _bdot(m_ref[...], w_ref[...])
    mu = jnp.mean(u, axis=-1, keepdims=True)
    d = u - mu
    var = jnp.mean(d * d, axis=-1, keepdims=True)
    o_ref[...] = d * lax.rsqrt(var + LN_EPS) * g_ref[...] + b_ref[...]


def _out_ln(merged, w_out, x, ln_g, ln_b, tm=LN_TM):
    m = x.shape[0]
    tm = min(tm, m)
    return pl.pallas_call(
        _out_ln_kernel,
        grid=(m // tm,),
        in_specs=[pl.BlockSpec((tm, D_MODEL), lambda i: (i, 0)),
                  pl.BlockSpec((D_MODEL, D_MODEL), lambda i: (0, 0)),
                  pl.BlockSpec((tm, D_MODEL), lambda i: (i, 0)),
                  pl.BlockSpec((1, D_MODEL), lambda i: (0, 0)),
                  pl.BlockSpec((1, D_MODEL), lambda i: (0, 0))],
        out_specs=pl.BlockSpec((tm, D_MODEL), lambda i: (i, 0)),
        out_shape=jax.ShapeDtypeStruct((m, D_MODEL), F32),
        compiler_params=_params(("parallel",)),
    )(merged, w_out, x, ln_g, ln_b)


def _rwkv_premix(xm, vfirst, w0, w2, a0, a2, kkp, ka, vgate):
    r = xm[:, 0:D_A]
    k = xm[:, D_A:2 * D_A]
    v = xm[:, 2 * D_A:3 * D_A]
    w_lo = xm[:, 3 * D_A:3 * D_A + R_W]
    a_lo = xm[:, 3 * D_A + R_W:3 * D_A + R_W + R_A]
    z = xm[:, 3 * D_A + R_W + R_A:A_COLS]
    t = w0 + _bdot(jnp.tanh(w_lo), w2)
    lw = -math.exp(-0.5) * _sigmoid(t)
    if vgate is not None:
        v0, v1, v2 = vgate
        vg = _sigmoid(v0 + _bdot(_bdot(v, v1), v2))
        v = v + (vfirst - v) * vg
    lr = _sigmoid(a0 + _bdot(a_lo, a2))
    kkr = k * kkp
    k2 = k * (1.0 + (lr - 1.0) * ka)
    return r, lw, k2, v, lr, kkr, z


def _head_post(y, r, k2, v, z, rk, gn_g, gn_b):
    mu = jnp.mean(y, axis=-1, keepdims=True)
    d = y - mu
    var = jnp.mean(d * d, axis=-1, keepdims=True)
    yn = d * lax.rsqrt(var + GN_EPS) * gn_g + gn_b
    bonus = jnp.sum(r * k2 * rk, axis=-1, keepdims=True) * v
    return (yn + bonus) * _silu(z)


def _rwkv_prompt_kernel(*refs, has_vgate):
    if has_vgate:
        (fa_ref, vf_ref, mu_ref, w0_ref, w2_ref, a0_ref, a2_ref, kkp_ref, ka_ref, rk_ref, gg_ref, gb_ref,
         v0_ref, v1_ref, v2_ref, o_ref, s_out_ref, s_ref, last_ref) = refs
    else:
        (fa_ref, mu_ref, w0_ref, w2_ref, a0_ref, a2_ref, kkp_ref, ka_ref, rk_ref, gg_ref, gb_ref,
         o_ref, vf_out_ref, s_out_ref, s_ref, last_ref) = refs
    c = pl.program_id(1)
    nc = pl.num_programs(1)
    C = CHUNK

    @pl.when(c == 0)
    def _():
        s_ref[...] = jnp.zeros_like(s_ref)
        last_ref[...] = jnp.zeros_like(last_ref)

    ri = lax.broadcasted_iota(jnp.int32, (C, C), 0)
    ci = lax.broadcasted_iota(jnp.int32, (C, C), 1)
    tri_i = ri >= ci
    tri_s = ri > ci
    tri_b = jnp.where(tri_i, 1.0, 0.0).astype(BF16)
    eye = jnp.where(ri == ci, 1.0, 0.0).astype(F32)
    row = lax.broadcasted_iota(jnp.int32, (C, 1), 0)
    rk = rk_ref[...]
    gg = gg_ref[...]
    gb = gb_ref[...]
    nb = fa_ref.shape[0]

    pre = []
    for bi in range(nb):
        x = fa_ref[bi]
        prev = jnp.where(row == 0, last_ref[bi, 0:1, :], pltpu.roll(x, 1, axis=0))
        last_ref[bi, 0:1, :] = x[C - 1:C, :]
        xm = x + (prev - x) * mu_ref[...]
        if has_vgate:
            vgate = (v0_ref[...], v1_ref[...], v2_ref[...])
            vfirst = vf_ref[bi]
        else:
            vgate, vfirst = None, None
        r, lw, k2, v, lr, kkr, z = _rwkv_premix(xm, vfirst, w0_ref[...], w2_ref[...], a0_ref[...], a2_ref[...],
                                                kkp_ref[...], ka_ref[...], vgate)
        if not has_vgate:
            vf_out_ref[bi] = v
        lw_hi = lw.astype(BF16)
        lw_lo = (lw - lw_hi.astype(F32)).astype(BF16)
        L = jnp.dot(tri_b, lw_hi, preferred_element_type=F32) + jnp.dot(tri_b, lw_lo, preferred_element_type=F32)
        LC = L[C - 1:C, :]
        pre.append(dict(r=r, k2=k2, v=v, lr=lr, kkr=kkr, z=z, e_in=jnp.exp(L), e_ex=jnp.exp(L - lw),
                        e_neg=jnp.exp(-L), e_rem=jnp.exp(LC - L), pc=jnp.exp(LC)))

    def run(units):
        heads = range(len(units))
        sls = [slice(h * HS, (h + 1) * HS) for _, h in units]
        col = lambda name: [pre[bi][name][:, sls[u]] for u, (bi, _) in enumerate(units)]
        bf = lambda xs: [x.astype(BF16) for x in xs]
        kk = []
        for u in col('kkr'):
            kk.append(u * lax.rsqrt(jnp.maximum(jnp.sum(u * u, axis=-1, keepdims=True), 1e-24)))
        r_h, k_h, v_h, lr_h, z_h = col('r'), col('k2'), col('v'), col('lr'), col('z')
        e_in, e_ex, e_neg, e_rem, pc = col('e_in'), col('e_ex'), col('e_neg'), col('e_rem'), col('pc')
        v_b = bf(v_h)
        b_h = [kk[h] * lr_h[h] for h in heads]
        at = [(-kk[h]) * e_ex[h] for h in heads]
        rt = [r_h[h] * e_in[h] for h in heads]
        bt = [b_h[h] * e_neg[h] for h in heads]
        kt = [k_h[h] * e_neg[h] for h in heads]
        bh = bf([b_h[h] * e_rem[h] for h in heads])
        kh = bf([k_h[h] * e_rem[h] for h in heads])
        g = [_bdot(jnp.concatenate([at[h], rt[h]], axis=0), jnp.concatenate([bt[h], kt[h]], axis=0), NT)
             for h in heads]
        a_ab = bf([jnp.where(tri_s, x[:C, :C], 0.0) for x in g])
        a_ak = bf([jnp.where(tri_s, x[:C, C:], 0.0) for x in g])
        a_rb = bf([jnp.where(tri_i, x[C:, :C], 0.0) for x in g])
        a_rk = bf([jnp.where(tri_i, x[C:, C:], 0.0) for x in g])
        tm = [eye + x.astype(F32) for x in a_ab]
        ap = bf([_bdot(x, x) for x in a_ab])
        n = 2
        while n < C:
            tm_next = [tm[h] + _bdot(tm[h], ap[h]) for h in heads]
            if 2 * n < C:
                ap = bf([_bdot(x, x) for x in ap])
            tm = tm_next
            n *= 2
        akv = [_bdot(a_ak[h], v_b[h]) for h in heads]
        wu = bf([_bdot(tm[h], jnp.concatenate([at[h], akv[h]], axis=1)) for h in heads])
        arw = [_bdot(a_rb[h], wu[h]) for h in heads]
        yh = [arw[h][:, HS:] + _bdot(a_rk[h], v_b[h]) for h in heads]
        s_old = [s_ref[bi, h] for bi, h in units]
        s_b = bf(s_old)
        y = [_bdot(rt[h] + arw[h][:, :HS], s_b[h], NT) + yh[h] for h in heads]
        bw = [_bdot(bh[h], wu[h][:, :HS], TN) for h in heads]
        nt = [_bdot(jnp.concatenate([wu[h][:, HS:], v_b[h]], axis=0), jnp.concatenate([bh[h], kh[h]], axis=0), TN)
              for h in heads]
        for u, (bi, h) in enumerate(units):
            s_ref[bi, h] = s_old[u] * pc[u] + _bdot(s_b[u], bw[u], NT) + nt[u]
        for u, (bi, h) in enumerate(units):
            sl = sls[u]
            o_ref[bi, :, sl] = _head_post(y[u], r_h[u], k_h[u], v_h[u], z_h[u], rk[:, sl], gg[:, sl],
                                          gb[:, sl]).astype(o_ref.dtype)

    all_units = [(bi, h) for bi in range(nb) for h in range(H_A)]
    for g0 in range(0, len(all_units), UNIT_GROUP):
        run(all_units[g0:g0 + UNIT_GROUP])

    @pl.when(c == nc - 1)
    def _():
        s_out_ref[...] = s_ref[...]


def _rwkv_prompt(ha, vfirst, p, bsz, t):
    has_vgate = vfirst is not None
    nc = t // CHUNK
    nb = RWKV_ROWS if bsz % RWKV_ROWS == 0 else 1
    row_spec = lambda w: pl.BlockSpec((nb, CHUNK, w), lambda b, c: (b, c, 0))
    full = lambda a: pl.BlockSpec(a.shape, lambda b, c: (0,) * a.ndim)
    ins = [ha.reshape(bsz, t, A_COLS)]
    in_specs = [row_spec(A_COLS)]
    if has_vgate:
        ins.append(vfirst.reshape(bsz, t, D_A))
        in_specs.append(row_spec(D_A))
    names = ['mu', 'w0', 'w2', 'a0', 'a2', 'kk', 'ka', 'rk', 'gn_g', 'gn_b'] + (['v0', 'v1', 'v2'] if has_vgate else [])
    for nme in names:
        ins.append(p[nme])
        in_specs.append(full(p[nme]))
    out_shape = [jax.ShapeDtypeStruct((bsz, t, D_A), BF16)]
    out_specs = [row_spec(D_A)]
    if not has_vgate:
        out_shape.append(jax.ShapeDtypeStruct((bsz, t, D_A), F32))
        out_specs.append(row_spec(D_A))
    out_shape.append(jax.ShapeDtypeStruct((bsz, H_A, HS, HS), F32))
    out_specs.append(pl.BlockSpec((nb, H_A, HS, HS), lambda b, c: (b, 0, 0, 0)))
    outs = pl.pallas_call(
        functools.partial(_rwkv_prompt_kernel, has_vgate=has_vgate),
        grid=(bsz // nb, nc),
        in_specs=in_specs,
        out_specs=out_specs,
        out_shape=out_shape,
        scratch_shapes=[pltpu.VMEM((nb, H_A, HS, HS), F32), pltpu.VMEM((nb, 8, A_COLS), F32)],
        compiler_params=_params(("parallel", "arbitrary")),
    )(*ins)
    o_a = outs[0].reshape(bsz * t, D_A)
    if has_vgate:
        return o_a, vfirst, outs[1]
    return o_a, outs[1].reshape(bsz * t, D_A), outs[2]


def _rwkv_sample_kernel(*refs, has_vgate, bt):
    if has_vgate:
        (fa_ref, prev_ref, s_in_ref, vf_ref, mu_ref, w0_ref, w2_ref, a0_ref, a2_ref, kkp_ref, ka_ref, rk_ref,
         gg_ref, gb_ref, v0_ref, v1_ref, v2_ref, o_ref, s_out_ref, ops_ref, y_ref) = refs
    else:
        (fa_ref, prev_ref, s_in_ref, mu_ref, w0_ref, w2_ref, a0_ref, a2_ref, kkp_ref, ka_ref, rk_ref,
         gg_ref, gb_ref, o_ref, vf_out_ref, s_out_ref, ops_ref, y_ref) = refs
    x = fa_ref[...]
    xm = x + (prev_ref[...] - x) * mu_ref[...]
    if has_vgate:
        vgate = (v0_ref[...], v1_ref[...], v2_ref[...])
        vfirst = vf_ref[...]
    else:
        vgate, vfirst = None, None
    r, lw, k2, v, lr, kkr, z = _rwkv_premix(xm, vfirst, w0_ref[...], w2_ref[...], a0_ref[...], a2_ref[...],
                                            kkp_ref[...], ka_ref[...], vgate)
    if not has_vgate:
        vf_out_ref[...] = v
    w = jnp.exp(lw)
    for h in range(H_A):
        sl = slice(h * HS, (h + 1) * HS)
        kk = kkr[:, sl]
        kk = kk * lax.rsqrt(jnp.maximum(jnp.sum(kk * kk, axis=-1, keepdims=True), 1e-24))
        ops_ref[0, :, sl] = -kk
        ops_ref[1, :, sl] = kk * lr[:, sl]
    ops_ref[2] = w
    ops_ref[3] = k2
    ops_ref[4] = v
    ops_ref[5] = r
    ri = lax.broadcasted_iota(jnp.int32, (HS, HS), 0)
    ci = lax.broadcasted_iota(jnp.int32, (HS, HS), 1)
    eye = jnp.where(ri == ci, 1.0, 0.0).astype(F32)

    heads = range(H_A)
    sls = [slice(h * HS, (h + 1) * HS) for h in heads]
    eye_b = eye.astype(BF16)
    rows_of = lambda x: jnp.broadcast_to(x, (HS, HS)).astype(BF16)
    for b in range(bt):
        row = lambda i: [ops_ref[i, b:b + 1, sl] for sl in sls]
        a_row, b_row, w_row, k_row, v_row, r_row = (row(i) for i in range(6))
        s = [s_in_ref[b, h] for h in heads]
        sa = [_bdot(s[h], rows_of(a_row[h]), NT) for h in heads]
        v_hi = [v_row[h].astype(BF16) for h in heads]
        v_lo = [(v_row[h] - v_hi[h].astype(F32)).astype(BF16) for h in heads]
        v_bc = [_bdot(eye_b, rows_of(v_hi[h]), NT) + _bdot(eye_b, rows_of(v_lo[h]), NT) for h in heads]
        s_new = [s[h] * w_row[h] + sa[h] * b_row[h] + v_bc[h] * k_row[h] for h in heads]
        for h in heads:
            s_out_ref[b, h] = s_new[h]
        y_bc = [_bdot(s_new[h], rows_of(r_row[h]), NT) for h in heads]
        for h in heads:
            y_ref[b:b + 1, sls[h]] = jnp.sum(eye * y_bc[h], axis=0, keepdims=True)
    y = y_ref[...]
    rk = rk_ref[...]
    gg = gg_ref[...]
    gb = gb_ref[...]
    for h in range(H_A):
        sl = slice(h * HS, (h + 1) * HS)
        o_ref[:, sl] = _head_post(y[:, sl], r[:, sl], k2[:, sl], v[:, sl], z[:, sl], rk[:, sl], gg[:, sl],
                                  gb[:, sl]).astype(o_ref.dtype)


def _rwkv_sample(ha, prev, s_in, vfirst, p, bt=8):
    bsz = ha.shape[0]
    has_vgate = vfirst is not None
    row_spec = lambda w: pl.BlockSpec((bt, w), lambda i: (i, 0))
    full = lambda a: pl.BlockSpec(a.shape, lambda i: (0,) * a.ndim)
    st_spec = pl.BlockSpec((bt, H_A, HS, HS), lambda i: (i, 0, 0, 0))
    ins = [ha, prev, s_in]
    in_specs = [row_spec(A_COLS), row_spec(A_COLS), st_spec]
    if has_vgate:
        ins.append(vfirst)
        in_specs.append(row_spec(D_A))
    names = ['mu', 'w0', 'w2', 'a0', 'a2', 'kk', 'ka', 'rk', 'gn_g', 'gn_b'] + (['v0', 'v1', 'v2'] if has_vgate else [])
    for nme in names:
        ins.append(p[nme])
        in_specs.append(full(p[nme]))
    out_shape = [jax.ShapeDtypeStruct((bsz, D_A), BF16)]
    out_specs = [row_spec(D_A)]
    if not has_vgate:
        out_shape.append(jax.ShapeDtypeStruct((bsz, D_A), F32))
        out_specs.append(row_spec(D_A))
    out_shape.append(jax.ShapeDtypeStruct((bsz, H_A, HS, HS), F32))
    out_specs.append(st_spec)
    outs = pl.pallas_call(
        functools.partial(_rwkv_sample_kernel, has_vgate=has_vgate, bt=bt),
        grid=(bsz // bt,),
        in_specs=in_specs,
        out_specs=out_specs,
        out_shape=out_shape,
        scratch_shapes=[pltpu.VMEM((6, bt, D_A), F32), pltpu.VMEM((bt, D_A), F32)],
        compiler_params=_params(("parallel",)),
    )(*ins)
    if has_vgate:
        return outs[0], vfirst, outs[1]
    return outs[0], outs[1], outs[2]


def _compress_rows(load_rows, w1_ref, b1, w2, b2, kv, nch):
    accs = [jnp.zeros((nch, 2 * CMP_HID), F32) for _ in range(G_KV)]
    for tau in range(0, CMP_STRIDE, 2):
        for pair in range(G_KV // 2):
            rows = jnp.concatenate([load_rows(tau, pair), load_rows(tau + 1, pair)], axis=1).astype(BF16)
            for parity in range(2):
                g = 2 * pair + parity
                accs[g] = accs[g] + jnp.dot(rows, w1_ref[kv, tau // 2, parity], preferred_element_type=F32)
    outs = []
    for acc in accs:
        h = acc[:, :CMP_HID] + pltpu.roll(acc[:, CMP_HID:], nch - 1, axis=0) + b1
        outs.append(_bdot(_gelu_tanh(h), w2) + b2)
    return outs


def _kv_prep_kernel(kc_ref, ks_ref, kw_ref, w1_ref, b1_ref, w2_ref, b2_ref,
                    ks_o, vs_o, kw_o, vw_o, kc_o, vc_o, *, nch):
    npair = KV_W // LANES

    def put_transposed(out, pair, x):
        xt = x.T
        out[0, 2 * pair] = xt[0:HD].astype(BF16)
        out[0, 2 * pair + 1] = xt[HD:2 * HD].astype(BF16)

    t = ks_ref.shape[0]
    nsel = t // SEL_LEN
    trow = lax.broadcasted_iota(jnp.int32, (t, HD), 0)
    ccol = lax.broadcasted_iota(jnp.int32, (t, HD), 1)
    extra = jnp.where(((ccol < nsel) & (trow // SEL_LEN == ccol)) | (ccol == nsel) | (ccol == nsel + 1), 1.0, 0.0)
    extra = extra.astype(BF16)
    for g in range(G_KV):
        sl = slice(g * HD, (g + 1) * HD)
        ks_o[0, g] = jnp.concatenate([ks_ref[:, sl].astype(BF16), extra], axis=1)
        kw_o[0, g] = jnp.concatenate([kw_ref[:, sl].astype(BF16), extra], axis=1)
    for pair in range(npair):
        sl2 = slice(KV_W + pair * LANES, KV_W + (pair + 1) * LANES)
        put_transposed(vs_o, pair, ks_ref[:, sl2])
        put_transposed(vw_o, pair, kw_ref[:, sl2])
    for kv in (0, 1):
        load = lambda tau, pair, kv=kv: kc_ref[pl.ds(2 * npair * tau + kv * npair + pair, nch,
                                                     stride=2 * npair * CMP_STRIDE), :]
        res = _compress_rows(load, w1_ref, b1_ref[kv], w2_ref[kv], b2_ref[kv], kv, nch)
        if kv == 0:
            for g in range(G_KV):
                kc_o[0, g] = res[g].astype(BF16)
        else:
            for pair in range(npair):
                put_transposed(vc_o, pair, jnp.concatenate([res[2 * pair], res[2 * pair + 1]], axis=1))


def _kv_prep(hb, hc, cp, bsz, t):
    nch = t // CMP_STRIDE
    blk = lambda j: pl.BlockSpec((t, 2 * KV_W), lambda b: (b, j))
    full = lambda a: pl.BlockSpec(a.shape, lambda b: (0,) * a.ndim)
    c0 = D_B // (2 * KV_W)
    lane_rows = 2 * KV_W // LANES
    hc = hc.reshape(bsz * t * lane_rows, LANES)
    def arr(n, transposed, width=HD):
        shp = (G_KV, HD, n) if transposed else (G_KV, n, width)
        return jax.ShapeDtypeStruct((bsz,) + shp, BF16), pl.BlockSpec((1,) + shp, lambda b: (b, 0, 0, 0))

    outs = [arr(t, False, 2 * HD), arr(t, True), arr(t, False, 2 * HD), arr(t, True), arr(nch, False), arr(nch, True)]
    return pl.pallas_call(
        functools.partial(_kv_prep_kernel, nch=nch),
        grid=(bsz,),
        in_specs=[pl.BlockSpec((t * lane_rows, LANES), lambda b: (b, 0)), blk(c0 + 1), blk(c0 + 2),
                  full(cp['w1']), full(cp['b1']), full(cp['w2']), full(cp['b2'])],
        out_specs=[o[1] for o in outs],
        out_shape=[o[0] for o in outs],
        compiler_params=_params(("parallel",)),
    )(hc, hb, hb, cp['w1'], cp['b1'], cp['w2'], cp['b2'])


def _softmax_pieces(pieces, state):
    return _softmax_update(_piece_scores(pieces), [piece[2] for piece in pieces], state)


def _piece_scores(pieces):
    ss = []
    for k, q, vt, bias, mask in pieces:
        s = jnp.dot(k, q, preferred_element_type=F32)
        if bias is not None:
            s = s + bias
        ss.append(s if mask is None else jnp.where(mask, s, NEG))
    return ss


def _softmax_update(ss, vts, state):
    m_new = jnp.max(ss[0], axis=0, keepdims=True)
    for s in ss[1:]:
        m_new = jnp.maximum(m_new, jnp.max(s, axis=0, keepdims=True))
    if state is not None:
        m_new = jnp.maximum(m_new, state[0])
    ps = [jnp.exp2(s - m_new) for s in ss]
    l_new = jnp.sum(ps[0], axis=0, keepdims=True)
    for p in ps[1:]:
        l_new = l_new + jnp.sum(p, axis=0, keepdims=True)
    p_all = jnp.concatenate([p.astype(BF16) for p in ps], axis=0)
    vt_all = jnp.concatenate(vts, axis=1)
    acc_new = jnp.dot(vt_all, p_all, preferred_element_type=F32)
    if state is not None:
        alpha = jnp.exp2(state[0] - m_new)
        l_new = l_new + alpha * state[1]
        acc_new = acc_new + alpha * state[2]
    return m_new, l_new, acc_new


def _tree_sum(xs):
    while len(xs) > 1:
        xs = [xs[i] + xs[i + 1] for i in range(0, len(xs) - 1, 2)] + ([xs[-1]] if len(xs) % 2 else [])
    return xs[0]


def _nsa_prompt_groups_kernel(q_ref, gl_ref, zb_ref, kc_ref, vc_ref, ks_ref, vs_ref, kw_ref, vw_ref,
                              bc_ref, d0_ref, d1_ref, far_ref, ov_ref, o_ref, *, t, nch, ng):
    gp = pl.program_id(1)
    qt = pl.program_id(2)
    nsel = t // SEL_LEN
    nkb = t // KB
    nwin = WIN // KB
    ncols = R_Q * TQ
    gw = R_Q * HD
    groups = range(ng)
    qpos = qt * TQ + lax.broadcasted_iota(jnp.int32, (1, TQ), 1)
    rep = lambda x: jnp.concatenate([x] * R_Q, axis=1)
    ik = lax.broadcasted_iota(jnp.int32, (KB, TQ), 0)
    iq = lax.broadcasted_iota(jnp.int32, (KB, TQ), 1)
    causal = rep(ik <= iq)
    upper = rep(ik > iq)
    srow = lax.broadcasted_iota(jnp.int32, (HD, ncols), 0)

    q_all = (q_ref[...] * (SCALE * LOG2E)).T
    qst, bias_d0, bias_d1, far_hi, far_lo = [], [], [], [], []
    for gi in groups:
        heads = [gi * R_Q + r for r in range(R_Q)]
        qst.append(jnp.concatenate([q_all[h * HD:(h + 1) * HD, :] for h in heads], axis=1).astype(BF16))
        bias_d0.append(jnp.concatenate([d0_ref[h] for h in heads], axis=1))
        bias_d1.append(jnp.concatenate([d1_ref[h] for h in heads], axis=1))
        far_row = jnp.concatenate([jnp.full((1, TQ), far_ref[gp * ng * R_Q + h], F32) for h in heads], axis=1)
        far_hi.append(far_row.astype(BF16).astype(F32))
        far_lo.append(far_row - far_hi[gi])

    def q_operand(gi, allowed, with_far):
        extra = jnp.where((srow < nsel) & jnp.logical_not(allowed), NEG, 0.0)
        if with_far:
            extra = jnp.where(srow == nsel, far_hi[gi], jnp.where(srow == nsel + 1, far_lo[gi], extra))
        return jnp.concatenate([qst[gi], extra.astype(BF16)], axis=0)

    def piece(kref, vref, gi, kb, q, bias=None, mask=None):
        kbc = jnp.where(kb < 0, nkb - 1, jnp.minimum(kb, nkb - 1))
        off = pl.multiple_of(kbc * KB, KB)
        return kref[0, gi, pl.ds(off, KB), :], q, vref[0, gi, :, pl.ds(off, KB)], bias, mask

    o_w = []
    for gi in groups:
        q_near = q_operand(gi, srow < 2 * (qt + 1), False)
        q_far = q_operand(gi, srow < 2 * (qt - 1), True)
        pieces = [piece(kw_ref, vw_ref, gi, qt, q_near, bias_d0[gi], causal),
                  piece(kw_ref, vw_ref, gi, qt - 1, q_near, bias_d1[gi])]
        for j in range(2, nwin):
            pieces.append(piece(kw_ref, vw_ref, gi, qt - j, q_far))
        pieces.append(piece(kw_ref, vw_ref, gi, qt - nwin, q_far, None, upper))
        _, l_w, acc_w = _softmax_pieces(pieces, None)
        o_w.append(acc_w / l_w)

    nrow = lax.broadcasted_iota(jnp.int32, (nch, TQ), 0)
    mask_c4 = rep((qpos >= nrow * CMP_STRIDE + (CMP_LEN - 1)) & (nrow < nch - 1))
    blk = lax.broadcasted_iota(jnp.int32, (nsel, TQ), 0)
    cur = qpos // SEL_LEN
    ov = ov_ref[...]
    o_c, sel = [], []
    for gi in groups:
        bias_c = jnp.concatenate([bc_ref[gi * R_Q + r] for r in range(R_Q)], axis=1)
        s = jnp.dot(kc_ref[0, gi], qst[gi], preferred_element_type=F32) + bias_c
        s = jnp.where(mask_c4, s, NEG)
        p = jnp.where(mask_c4, jnp.exp2(s - jnp.max(s, axis=0, keepdims=True)), 0.0)
        l = jnp.sum(p, axis=0, keepdims=True)
        p = p / jnp.where(l > 0.0, l, 1.0)
        o_c.append(jnp.dot(vc_ref[0, gi], p.astype(BF16), preferred_element_type=F32))
        psum = p[:, 0:TQ]
        for r in range(1, R_Q):
            psum = psum + p[:, r * TQ:(r + 1) * TQ]
        p_hi = psum.astype(BF16)
        p_lo = (psum - p_hi.astype(F32)).astype(BF16)
        imp = jnp.dot(ov, p_hi, preferred_element_type=F32) + jnp.dot(ov, p_lo, preferred_element_type=F32)
        imp = jnp.where(blk * SEL_LEN <= qpos, imp, NEG)
        imp = jnp.where((blk == 0) | (blk == cur) | (blk == cur - 1), -NEG, imp)
        beaten = []
        for s2 in range(nsel):
            other = imp[s2:s2 + 1, :]
            beaten.append(jnp.where((other > imp) | ((other == imp) & (blk > s2)), 1.0, 0.0))
        chosen = jnp.where(_tree_sum(beaten) < float(min(N_TOP, nsel)), 1.0, 0.0)
        sel.append(rep(jnp.concatenate([chosen, jnp.zeros((HD - nsel, TQ), F32)], axis=0)) > 0.5)

    nfar = jnp.maximum(qt - 1, 0)
    states, q_far = [], []
    for gi in groups:
        q_near = q_operand(gi, sel[gi] & (srow < 2 * (qt + 1)), False)
        states.append(_softmax_pieces([piece(ks_ref, vs_ref, gi, qt, q_near, bias_d0[gi], causal),
                                       piece(ks_ref, vs_ref, gi, qt - 1, q_near, bias_d1[gi])], None))
        q_far.append(q_operand(gi, sel[gi] & (srow < 2 * nfar), True))

    def far_sel(it, states):
        return tuple(_softmax_pieces([piece(ks_ref, vs_ref, gi, it * FAR_GROUP + j, q_far[gi])
                                      for j in range(FAR_GROUP)], states[gi]) for gi in groups)

    states = lax.fori_loop(0, (nfar + FAR_GROUP - 1) // FAR_GROUP, far_sel, tuple(states))

    gate = _sigmoid(gl_ref[...]).T
    outs = []
    for gi in groups:
        o_s = states[gi][2] / states[gi][1]
        for r in range(R_Q):
            cs = slice(r * TQ, (r + 1) * TQ)
            row = gi * GL_PAD + 3 * r
            outs.append(gate[row:row + 1, :] * o_c[gi][:, cs] + gate[row + 1:row + 2, :] * o_s[:, cs]
                        + gate[row + 2:row + 3, :] * o_w[gi][:, cs])
    o = jnp.concatenate(outs, axis=0).T
    o_ref[...] = (o * _silu(zb_ref[...])).astype(o_ref.dtype)


def _nsa_prompt(hb, kvp, tabs, bsz, t):
    ks_t, vs_t, kw_t, vw_t, kc_t, vc_t = kvp
    nch = t // CMP_STRIDE
    nqt = t // TQ
    ng = NSA_GROUPS
    gw = ng * R_Q * HD
    k_spec = lambda n, w=HD: pl.BlockSpec((1, ng, n, w), lambda b, g, i: (b, g, 0, 0))
    vt_spec = lambda n: pl.BlockSpec((1, ng, HD, n), lambda b, g, i: (b, g, 0, 0))
    gl0 = (D_B + 6 * KV_W) // (ng * GL_PAD)
    zb0 = (D_B + 6 * KV_W + G_KV * GL_PAD) // gw
    return pl.pallas_call(
        functools.partial(_nsa_prompt_groups_kernel, t=t, nch=nch, ng=ng),
        grid=(bsz, G_KV // ng, nqt),
        in_specs=[pl.BlockSpec((TQ, gw), lambda b, g, i: (b * nqt + i, g)),
                  pl.BlockSpec((TQ, ng * GL_PAD), lambda b, g, i: (b * nqt + i, gl0 + g)),
                  pl.BlockSpec((TQ, gw), lambda b, g, i: (b * nqt + i, zb0 + g)),
                  k_spec(nch), vt_spec(nch), k_spec(t, 2 * HD), vt_spec(t), k_spec(t, 2 * HD), vt_spec(t),
                  pl.BlockSpec((ng * R_Q, nch, TQ), lambda b, g, i: (g, 0, i)),
                  pl.BlockSpec((ng * R_Q, KB, TQ), lambda b, g, i: (g, 0, 0)),
                  pl.BlockSpec((ng * R_Q, KB, TQ), lambda b, g, i: (g, 0, 0)),
                  pl.BlockSpec(memory_space=pltpu.SMEM),
                  pl.BlockSpec(tabs['overlap'].shape, lambda b, g, i: (0, 0))],
        out_specs=pl.BlockSpec((TQ, gw), lambda b, g, i: (b * nqt + i, g)),
        out_shape=jax.ShapeDtypeStruct((bsz * t, D_B), BF16),
        compiler_params=_params(("parallel", "parallel", "arbitrary")),
    )(hb, hb, hb, kc_t, vc_t, ks_t, vs_t, kw_t, vw_t, tabs['bias_c'], tabs['d0'], tabs['d1'], tabs['far'],
      tabs['overlap'])


def _row_softmax(s, mask, s_new):
    sm = jnp.where(mask, s, NEG)
    m = jnp.maximum(jnp.max(sm, axis=-1, keepdims=True), s_new)
    p = jnp.where(mask, jnp.exp(sm - m), 0.0)
    p_new = jnp.exp(s_new - m)
    l = jnp.sum(p, axis=-1, keepdims=True) + p_new
    return p / l, p_new / l


def _nsa_sample_kernel(pt_ref, *refs, n_pages, past):
    page_refs = refs[:n_pages]
    (win_ref, qbd_ref, new_ref, gl_ref, zb_ref, w1_ref, b1_ref, w2_ref, b2_ref,
     bc_ref, bs_ref, bw_ref, b0_ref, ov_ref, gsum_ref, bdm_ref, ex_ref, o_ref, x_s, kc_s, vc_s) = refs[n_pages:]
    del pt_ref
    nch = past // CMP_STRIDE
    nsel = past // SEL_LEN + 1
    qbd = qbd_ref[0]
    qbd_f = qbd.astype(F32)
    npair = KV_W // LANES

    for pi, pr in enumerate(page_refs):
        for kv in range(2):
            for pair in range(npair):
                x_s[kv, pair, pi * PAGE_SIZE:(pi + 1) * PAGE_SIZE, :] = pr[0, 0, kv, pair * LANES:(pair + 1) * LANES, :].T
    for kv, dst in ((0, kc_s), (1, vc_s)):
        load = lambda tau, pair, kv=kv: x_s[kv, pair, pl.ds(tau, nch, stride=CMP_STRIDE), :]
        res = _compress_rows(load, w1_ref, b1_ref[kv], w2_ref[kv], b2_ref[kv], kv, nch)
        for g in range(G_KV):
            dst[:, g * HD:(g + 1) * HD] = res[g]
    ncol = lax.broadcasted_iota(jnp.int32, (H_B, nch), 1)
    s_c = _bdot(qbd, kc_s[...], NT) + bc_ref[...]
    mask_c = ncol < nch - 1
    sm = jnp.where(mask_c, s_c, NEG)
    p_c = jnp.where(mask_c, jnp.exp(sm - jnp.max(sm, axis=-1, keepdims=True)), 0.0)
    p_c = p_c / jnp.sum(p_c, axis=-1, keepdims=True)
    o_c = _bdot(p_c, vc_s[...])
    p_hi = p_c.astype(BF16)
    p_lo = (p_c - p_hi.astype(F32)).astype(BF16)
    ov = ov_ref[...]
    imp = jnp.dot(p_hi, ov, preferred_element_type=F32) + jnp.dot(p_lo, ov, preferred_element_type=F32)
    i_hi = imp.astype(BF16)
    i_lo = (imp - i_hi.astype(F32)).astype(BF16)
    gs = gsum_ref[...]
    imp = jnp.dot(gs, i_hi, preferred_element_type=F32) + jnp.dot(gs, i_lo, preferred_element_type=F32)
    nsp = imp.shape[1]
    blk = lax.broadcasted_iota(jnp.int32, (H_B, nsp), 1)
    cur = past // SEL_LEN
    imp = jnp.where((blk == 0) | (blk == cur) | (blk == cur - 1), -NEG, imp)
    imp = jnp.where(blk < nsel, imp, 2.0 * NEG)
    rank = jnp.zeros((H_B, nsp), F32)
    for s2 in range(nsel):
        other = imp[:, s2:s2 + 1]
        rank = rank + jnp.where((other > imp) | ((other == imp) & (blk > s2)), 1.0, 0.0)
    sel = jnp.where(rank < float(N_TOP), 1.0, 0.0).astype(BF16)
    mask_s = jnp.dot(sel, ex_ref[...], preferred_element_type=F32) > 0.5

    new = new_ref[0]
    ks_new = new[:, 2 * KV_W:3 * KV_W]
    vs_new = new[:, 3 * KV_W:4 * KV_W]
    kw_new = new[:, 4 * KV_W:5 * KV_W]
    vw_new = new[:, 5 * KV_W:6 * KV_W]
    b0 = b0_ref[...]
    s_s = jnp.concatenate([_bdot(qbd, pr[0, 0, 2]) for pr in page_refs], axis=1) + bs_ref[...]
    s_new = jnp.sum(qbd_f * ks_new, axis=-1, keepdims=True) + b0
    p_s, p_new = _row_softmax(s_s, mask_s, s_new)
    o_s = p_new * vs_new
    for pi, pr in enumerate(page_refs):
        o_s = o_s + _bdot(p_s[:, pi * PAGE_SIZE:(pi + 1) * PAGE_SIZE], pr[0, 0, 3], NT)
    nbuf = win_ref.shape[-1]
    wcol = lax.broadcasted_iota(jnp.int32, (H_B, nbuf), 1)
    s_w = _bdot(qbd, win_ref[0, 0, 0]) + bw_ref[...]
    s_wn = jnp.sum(qbd_f * kw_new, axis=-1, keepdims=True) + b0
    p_w, p_wn = _row_softmax(s_w, wcol >= nbuf + 1 - WIN, s_wn)
    o_w = _bdot(p_w, win_ref[0, 0, 1], NT) + p_wn * vw_new
    gate = _sigmoid(gl_ref[0])
    o = gate[:, 0:1] * o_c + gate[:, 1:2] * o_s + gate[:, 2:3] * o_w
    o = o * bdm_ref[...]
    o16 = o[:, 0:HD]
    for g in range(1, G_KV):
        o16 = o16 + o[:, g * HD:(g + 1) * HD]
    o_ref[0] = (o16 * _silu(zb_ref[0])).astype(o_ref.dtype)


def _nsa_sample(cache_l, l, page_table, win, qbd, new_rows, gl, zb, cp, tabs):
    bsz, n_pages = page_table.shape
    past = n_pages * PAGE_SIZE
    nch = past // CMP_STRIDE
    nbuf = win.shape[-1]
    full = lambda a: pl.BlockSpec(a.shape, lambda b, pt: (0,) * a.ndim)
    page_specs = [pl.BlockSpec((1, 1) + cache_l.shape[2:],
                               functools.partial(lambda b, pt, j: (l, pt[b, j], 0, 0, 0), j=j))
                  for j in range(n_pages)]
    consts = [cp['w1'], cp['b1'], cp['w2'], cp['b2'], tabs['bias_c'], tabs['bias_s'], tabs['bias_w'], tabs['bias_0'],
              tabs['overlap'], tabs['gsum'], tabs['bdmask'], tabs['expand']]
    grid_spec = pltpu.PrefetchScalarGridSpec(
        num_scalar_prefetch=1,
        grid=(bsz,),
        in_specs=page_specs + [
            pl.BlockSpec((1, 1, 2, KV_W, nbuf), lambda b, pt: (l, b, 0, 0, 0)),
            pl.BlockSpec((1, H_B, KV_W), lambda b, pt: (b, 0, 0)),
            pl.BlockSpec((1, 1, 6 * KV_W), lambda b, pt: (b, 0, 0)),
            pl.BlockSpec((1, H_B, 3), lambda b, pt: (b, 0, 0)),
            pl.BlockSpec((1, H_B, HD), lambda b, pt: (b, 0, 0)),
        ] + [full(a) for a in consts],
        out_specs=pl.BlockSpec((1, H_B, HD), lambda b, pt: (b, 0, 0)),
        scratch_shapes=[pltpu.VMEM((2, KV_W // LANES, past, LANES), F32), pltpu.VMEM((nch, KV_W), F32),
                        pltpu.VMEM((nch, KV_W), F32)],
    )
    return pl.pallas_call(
        functools.partial(_nsa_sample_kernel, n_pages=n_pages, past=past),
        grid_spec=grid_spec,
        out_shape=jax.ShapeDtypeStruct((bsz, H_B, HD), BF16),
        compiler_params=_params(("arbitrary",)),
    )(page_table, *([cache_l] * n_pages), win, qbd, new_rows, gl, zb, *consts)


def _win_update_kernel(win_ref, new_ref, o_ref):
    nbuf = win_ref.shape[-1]
    ri = lax.broadcasted_iota(jnp.int32, (KV_W, KV_W), 0)
    ci = lax.broadcasted_iota(jnp.int32, (KV_W, KV_W), 1)
    eye = jnp.where(ri == ci, 1.0, 0.0).astype(BF16)
    lane = lax.broadcasted_iota(jnp.int32, (KV_W, nbuf), 1)
    for b in range(win_ref.shape[1]):
        for c in range(2):
            rem = new_ref[0, b, :, c * KV_W:(c + 1) * KV_W]
            col = jnp.zeros((KV_W, LANES), F32)
            for _ in range(3):
                part = rem.astype(BF16)
                rem = rem - part.astype(F32)
                col = col + lax.dot_general(eye, jnp.broadcast_to(part, (LANES, KV_W)), NT,
                                            preferred_element_type=F32)
            col = jnp.concatenate([col] * (nbuf // LANES), axis=1)
            o_ref[0, b, c] = jnp.where(lane == nbuf - 1, col, pltpu.roll(win_ref[0, b, c], nbuf - 1, axis=1))


def _win_update(win_l, new_win):
    depth, bsz = win_l.shape[:2]
    nb = WIN_ROWS if bsz % WIN_ROWS == 0 else 1
    blk = (1, nb) + win_l.shape[2:]
    return pl.pallas_call(
        _win_update_kernel,
        grid=(depth, bsz // nb),
        in_specs=[pl.BlockSpec(blk, lambda l, b: (l, b, 0, 0, 0)),
                  pl.BlockSpec((1, nb, 1, 2 * KV_W), lambda l, b: (l, b, 0, 0))],
        out_specs=pl.BlockSpec(blk, lambda l, b: (l, b, 0, 0, 0)),
        out_shape=jax.ShapeDtypeStruct(win_l.shape, F32),
        compiler_params=_params(("parallel", "parallel")),
    )(win_l, new_win)


def _t5_bucket(dist):
    n = jnp.maximum(dist, 0)
    max_exact = NUM_BUCKETS // 2
    nf = jnp.maximum(n, 1).astype(F32)
    large = max_exact + (jnp.log(nf / max_exact) / math.log(MAX_DIST / max_exact)
                         * (NUM_BUCKETS - max_exact)).astype(jnp.int32)
    large = jnp.minimum(large, NUM_BUCKETS - 1)
    return jnp.where(n < max_exact, n, large)


def _bias_of(rel_bias, dist):
    return jnp.moveaxis(rel_bias[_t5_bucket(dist)], -1, 0).astype(F32)


def _overlap(nch, nsel):
    ci = np.arange(nch)[:, None] * CMP_STRIDE
    sj = np.arange(nsel)[None, :] * SEL_LEN
    ov = ((ci < sj + SEL_LEN) & (ci + CMP_LEN > sj)).astype(np.float32)
    ov[nch - 1:, :] = 0.0
    return ov


def _skew(g, n, step, length):
    h, L = g.shape
    flat = jnp.tile(g, (1, n))[:, :n * (L - step)]
    return flat.reshape(h, n, L - step)[:, :, :length]


def _prompt_tables(rel_bias, t):
    nch = t // CMP_STRIDE
    nsel = t // SEL_LEN
    f = _bias_of(rel_bias, jnp.arange(max(t, 2 * KB), dtype=jnp.int32)) * LOG2E
    f0 = lambda n: jnp.broadcast_to(f[:, :1], (H_B, n))
    lead = CMP_LEN - 1
    g_c = jnp.concatenate([f0(lead), f[:, :t - lead], f0(CMP_STRIDE * nch)], axis=1)
    g_0 = jnp.concatenate([f[:, :KB], f0(KB)], axis=1)
    g_1 = jnp.concatenate([f[:, KB:2 * KB], f[:, :KB]], axis=1)
    return {
        'bias_c': _skew(g_c, nch, CMP_STRIDE, t),
        'd0': _skew(g_0, KB, 1, TQ),
        'd1': _skew(g_1, KB, 1, TQ),
        'far': rel_bias[NUM_BUCKETS - 1].astype(F32) * LOG2E,
        'overlap': jnp.asarray(_overlap(nch, nsel).T, BF16),
    }


def _sample_tables(rel_bias, past, nbuf):
    nch = past // CMP_STRIDE
    nsel = past // SEL_LEN + 1
    nsp = -(-nsel // LANES) * LANES
    cmp_end = jnp.arange(nch, dtype=jnp.int32) * CMP_STRIDE + CMP_LEN - 1
    ov = np.zeros((nch, nsp), np.float32)
    ov[:, :nsel] = _overlap(nch, nsel)
    hh = np.arange(H_B)
    gsum = (hh[:, None] // R_Q == hh[None, :] // R_Q).astype(np.float32)
    bdm = (hh[:, None] // R_Q == np.arange(KV_W)[None, :] // HD).astype(np.float32)
    expand = (np.arange(nsp)[:, None] == np.arange(past)[None, :] // SEL_LEN).astype(np.float32)
    return {
        'bias_c': _bias_of(rel_bias, past - cmp_end),
        'bias_s': _bias_of(rel_bias, past - jnp.arange(past, dtype=jnp.int32)),
        'bias_w': _bias_of(rel_bias, nbuf - jnp.arange(nbuf, dtype=jnp.int32)),
        'bias_0': _bias_of(rel_bias, jnp.zeros((1,), jnp.int32)),
        'overlap': jnp.asarray(ov, BF16),
        'gsum': jnp.asarray(gsum, BF16),
        'bdmask': jnp.asarray(bdm, F32),
        'expand': jnp.asarray(expand, BF16),
    }


def _layer_params(l, w_in, mu_shift, rw_w0, rw_w2, rw_a0, rw_a2, rw_kk, rw_ka, rw_rk, rw_gn_g, rw_gn_b,
                  rw_v0, rw_v1, rw_v2, cmp_w1, cmp_b1, cmp_w2, cmp_b2, w_up_a, w_up_b, w_out, ln_g, ln_b):
    w = w_in[l]
    b0 = A_COLS
    q_kv = w[:, b0:b0 + D_B + 6 * KV_W]
    gl = w[:, b0 + D_B + 6 * KV_W:b0 + D_B + 6 * KV_W + 3 * H_B].reshape(D_MODEL, G_KV, 3 * R_Q)
    gl = jnp.pad(gl, ((0, 0), (0, 0), (0, GL_PAD - 3 * R_Q))).reshape(D_MODEL, G_KV * GL_PAD)
    zb = w[:, b0 + D_B + 6 * KV_W + 3 * H_B:b0 + D_B + 6 * KV_W + 3 * H_B + D_B]
    half = CMP_STRIDE * HD
    w1 = cmp_w1[l]
    w1r = jnp.concatenate([w1[:, :half].reshape(2, CMP_STRIDE, HD, CMP_HID),
                           w1[:, half:].reshape(2, CMP_STRIDE, HD, CMP_HID)], axis=-1)
    zero = jnp.zeros_like(w1r)
    w1r = jnp.stack([jnp.concatenate([w1r, zero], axis=2), jnp.concatenate([zero, w1r], axis=2)], axis=2)
    w1r = w1r.reshape(2, CMP_STRIDE // 2, 2, 2, LANES, 2 * CMP_HID).transpose(0, 1, 3, 2, 4, 5)
    w1r = w1r.reshape(2, CMP_STRIDE // 2, 2, 2 * LANES, 2 * CMP_HID)
    row = lambda a: a.reshape(1, -1).astype(F32)
    p = {
        'wa': w[:, :A_COLS].astype(BF16),
        'wb': jnp.concatenate([q_kv, gl, zb], axis=1).astype(BF16),
        'wg': w[:, b0 + D_B + 6 * KV_W + 3 * H_B + D_B:].astype(BF16),
        'wc': w[:, b0 + D_B:b0 + D_B + 2 * KV_W].astype(BF16),
        'mu': row(mu_shift[l]), 'w0': row(rw_w0[l]), 'w2': rw_w2[l].astype(BF16), 'a0': row(rw_a0[l]),
        'a2': rw_a2[l].astype(BF16), 'kk': row(rw_kk[l]), 'ka': row(rw_ka[l]), 'rk': row(rw_rk[l]),
        'gn_g': row(rw_gn_g[l]), 'gn_b': row(rw_gn_b[l]),
        'cmp': {'w1': w1r.astype(BF16), 'b1': cmp_b1[l].reshape(2, 1, CMP_HID).astype(F32),
                'w2': cmp_w2[l].astype(BF16), 'b2': cmp_b2[l].reshape(2, 1, HD).astype(F32)},
        'w_up_a': w_up_a[l].astype(BF16), 'w_up_b': w_up_b[l].astype(BF16), 'w_out': w_out[l].astype(BF16),
        'ln_g': row(ln_g[l]), 'ln_b': row(ln_b[l]),
    }
    if l > 0:
        p['v0'] = row(rw_v0[l - 1])
        p['v1'] = rw_v1[l - 1].astype(BF16)
        p['v2'] = rw_v2[l - 1].astype(BF16)
    return p


def _project(x2, p):
    m = x2.shape[0]
    ha = _matmul(x2, p['wa'], PROJ_TM, A_COLS // 3)
    hb = _matmul(x2, p['wb'], PROJ_TM, PROJ_TN)
    hg = _matmul(x2, p['wg'], PROJ_TM, PROJ_TN, BF16)
    return ha, hb, hg


def _finish(x2, o_a, o_b, hg, p):
    merged = _up_gate(o_a, o_b, p['w_up_a'], p['w_up_b'], hg)
    return _out_ln(merged, p['w_out'], x2, p['ln_g'], p['ln_b'])


def _prompt_layer(x2, vfirst, p, tabs, bsz, t):
    ha, hb, hg = _project(x2, p)
    o_a, vfirst, wkv = _rwkv_prompt(ha, vfirst, p, bsz, t)
    hc = _matmul(x2, p['wc'], PROJ_TM, 2 * KV_W)
    kvp = _kv_prep(hb, hc, p['cmp'], bsz, t)
    o_b = _nsa_prompt(hb, kvp, tabs, bsz, t)
    y = _finish(x2, o_a, o_b, hg, p)
    kvc = D_B
    nwin = min(WIN, t)
    to_rows = lambda x, n: jnp.transpose(x.reshape(bsz, n, G_KV, HD, x.shape[-1]), (0, 4, 1, 2, 3))
    new_rows = to_rows(_cols_transposed(hb, bsz, t, kvc, 4 * KV_W, t), 4)
    win_state = to_rows(_cols_transposed(hb, bsz, t, kvc + 4 * KV_W, 2 * KV_W, nwin), 2)
    shift = ha.reshape(bsz, t, A_COLS)[:, t - 1]
    return y, vfirst, (new_rows, win_state, wkv, shift)


def _sample_layer(x2, vfirst, l, p, tabs, cache_l, win_l, cache_win_kv, state_wkv, state_shift, page_table):
    bsz = x2.shape[0]
    ha, hb, hg = _project(x2, p)
    o_a, vfirst, wkv = _rwkv_sample(ha, state_shift[l], state_wkv[l], vfirst, p)
    kvc = D_B
    q = hb[:, :D_B].reshape(bsz, G_KV, R_Q, HD) * SCALE
    eye = jnp.eye(G_KV, dtype=F32)
    qbd = (q[:, :, :, None, :] * eye[None, :, None, :, None]).reshape(bsz, H_B, KV_W).astype(BF16)
    new6 = hb[:, kvc:kvc + 6 * KV_W]
    gl0 = kvc + 6 * KV_W
    gl = hb[:, gl0:gl0 + G_KV * GL_PAD].reshape(bsz, G_KV, GL_PAD)[:, :, :3 * R_Q].reshape(bsz, H_B, 3)
    zb = hb[:, gl0 + G_KV * GL_PAD:].reshape(bsz, H_B, HD)
    o_b = _nsa_sample(cache_l, l, page_table, win_l, qbd, new6[:, None, :], gl, zb, p['cmp'], tabs)
    y = _finish(x2, o_a, o_b.reshape(bsz, D_B), hg, p)
    new_rows = new6[:, :4 * KV_W].reshape(bsz, 1, 4, G_KV, HD)
    new_win = new6[:, None, 4 * KV_W:]
    return y, vfirst, (new_rows, new_win, wkv, ha)


def kernel(x_prompt, x_sample, cache_kv, cache_win_kv, state_wkv, state_shift, page_table, w_in, mu_shift, rw_w0, rw_w2, rw_a0, rw_a2, rw_kk, rw_ka, rw_rk, rw_gn_g, rw_gn_b, rw_v0, rw_v1, rw_v2, cmp_w1, cmp_b1, cmp_w2, cmp_b2, rel_bias, w_up_a, w_up_b, w_out, ln_g, ln_b):
    bsz, t, _ = x_prompt.shape
    dec_b = x_sample.shape[0]
    n_pages = page_table.shape[1]
    depth, n_phys = cache_kv.shape[:2]
    cache_l = jnp.transpose(cache_kv, (0, 1, 3, 4, 5, 2)).reshape(depth, n_phys, 4, KV_W, PAGE_SIZE)
    win_l = jnp.transpose(cache_win_kv, (0, 1, 3, 4, 5, 2)).reshape(depth, dec_b, 2, KV_W, cache_win_kv.shape[2])
    tabs_p = _prompt_tables(rel_bias, t)
    tabs_s = _sample_tables(rel_bias, n_pages * PAGE_SIZE, cache_win_kv.shape[2])
    y_p = x_prompt.reshape(bsz * t, D_MODEL)
    y_s = x_sample.reshape(dec_b, D_MODEL)
    vf_p, vf_s = None, None
    st_p, st_s = [], []
    for l in range(depth):
        p = _layer_params(l, w_in, mu_shift, rw_w0, rw_w2, rw_a0, rw_a2, rw_kk, rw_ka, rw_rk, rw_gn_g, rw_gn_b,
                          rw_v0, rw_v1, rw_v2, cmp_w1, cmp_b1, cmp_w2, cmp_b2, w_up_a, w_up_b, w_out, ln_g, ln_b)
        y_p, vf_p, sp = _prompt_layer(y_p, vf_p, p, tabs_p, bsz, t)
        y_s, vf_s, ss = _sample_layer(y_s, vf_s, l, p, tabs_s, cache_l, win_l, cache_win_kv, state_wkv, state_shift,
                                      page_table)
        st_p.append(sp)
        st_s.append(ss)
    stack = lambda st, i: jnp.stack([s[i] for s in st])
    nbuf = cache_win_kv.shape[2]
    win_next = _win_update(win_l, stack(st_s, 1)).reshape(depth, dec_b, 2, G_KV, HD, nbuf)
    win_next = jnp.transpose(win_next, (0, 1, 5, 2, 3, 4))
    return (y_p.reshape(bsz, t, D_MODEL), y_s.reshape(dec_b, 1, D_MODEL),
            stack(st_p, 0), stack(st_p, 1), stack(st_p, 2), stack(st_p, 3),
            stack(st_s, 0), win_next, stack(st_s, 2), stack(st_s, 3))
```
